```python
import math
import jax, jax.numpy as jnp
from jax import lax
import numpy as np

D_MODEL = 1024
BATCH = 8
SEQ = 2048
DEPTH = 4

CHUNK = 64
N_MIXERS = 4
Q_BLOCK = 128
EPS = 1e-6
SUBLN_EPS = 1e-5
NEG = -1e30

N_BUCKETS = 32
MAX_DISTANCE = 128
N_ATT_HEADS = 8

DA_HEADS = N_ATT_HEADS
DA_DIM = D_MODEL // (2 * DA_HEADS)
DA_VDIM = 2 * DA_DIM

CONV_WIDTH = 31

SA_HEADS = N_ATT_HEADS
SA_DIM = D_MODEL // SA_HEADS
SA_LATENT = 256
IDX_HEADS = 8
IDX_DIM = 64
TOPK_MAX = 256
SA_QBLOCK = 64
SA_IN = D_MODEL + SA_LATENT + IDX_HEADS * IDX_DIM + IDX_DIM + IDX_HEADS

SG_CHUNK = 128
SG_WIDTH = 2 * D_MODEL
SG_GROUPS = 8

FFN_DIM = 2816
FFN_CONV = 3

kernel_name = "hybrid_streaming_interleaved_trunk"


def _n_of_kind(m):
    return len(range(m, DEPTH, N_MIXERS))


def rmsnorm(x, g, eps=EPS):
    x32 = x.astype(jnp.float32)
    y = x32 * lax.rsqrt(jnp.mean(x32 * x32, axis=-1, keepdims=True) + eps) * g.astype(jnp.float32)
    return y.astype(x.dtype)


def layernorm(x, g, b, eps=EPS):
    x32 = x.astype(jnp.float32)
    mu = jnp.mean(x32, axis=-1, keepdims=True)
    var = jnp.mean(jnp.square(x32 - mu), axis=-1, keepdims=True)
    y = (x32 - mu) * lax.rsqrt(var + eps) * g.astype(jnp.float32) + b.astype(jnp.float32)
    return y.astype(x.dtype)


def causal_dwconv(x, w):
    k = w.shape[0]
    return lax.conv_general_dilated(
        x, w[:, None, :].astype(x.dtype), window_strides=(1,), padding=[(k - 1, 0)],
        dimension_numbers=('NWC', 'WIO', 'NWC'), feature_group_count=x.shape[-1])


def rel_bucket(rel):
    nb = N_BUCKETS // 2
    max_exact = nb // 2
    ret = jnp.where(rel > 0, nb, 0)
    n = jnp.abs(rel)
    nf = jnp.maximum(n, 1).astype(jnp.float32)
    large = max_exact + (jnp.log(nf / max_exact) / math.log(MAX_DISTANCE / max_exact)
                         * (nb - max_exact)).astype(jnp.int32)
    large = jnp.minimum(large, nb - 1)
    return ret + jnp.where(n < max_exact, n, large)


def diff_attention(h, w_in, lam, subln_g, w_out, rel_table, lambda_init):
    B, S, _ = h.shape
    qkv = h @ w_in
    q, k, v = jnp.split(qkv, 3, axis=-1)
    q = q.reshape(B, S, DA_HEADS, 2, DA_DIM)
    k = k.reshape(B, S, DA_HEADS, 2, DA_DIM)
    v = v.reshape(B, S, DA_HEADS, DA_VDIM)
    lam32 = lam.astype(jnp.float32)
    lam_full = (jnp.exp(jnp.sum(lam32[0] * lam32[1])) - jnp.exp(jnp.sum(lam32[2] * lam32[3]))
                + lambda_init)
    nb = S // Q_BLOCK
    qb = q.reshape(B, nb, Q_BLOCK, DA_HEADS, 2, DA_DIM).swapaxes(0, 1)
    k_pos = jnp.arange(S)
    table = rel_table.astype(jnp.float32)
    scale = DA_DIM ** -0.5

    def block(args):
        qi, bi = args
        q_pos = bi * Q_BLOCK + jnp.arange(Q_BLOCK)
        logits = jnp.einsum('bqhmd,bkhmd->bhmqk', qi, k,
                            preferred_element_type=jnp.float32) * scale
        bias = table[rel_bucket(k_pos[None, :] - q_pos[:, None])].transpose(2, 0, 1)
        mask = (k_pos[None, :] // CHUNK) <= (q_pos[:, None] // CHUNK)
        logits = jnp.where(mask, logits + bias[:, None], NEG)
        p = jax.nn.softmax(logits, axis=-1)
        p = (p[:, :, 0] - lam_full * p[:, :, 1]).astype(v.dtype)
        return jnp.einsum('bhqk,bkhd->bqhd', p, v)

    o = lax.map(block, (qb, jnp.arange(nb)))
    o = o.swapaxes(0, 1).reshape(B, S, DA_HEADS, DA_VDIM)
    o = rmsnorm(o, subln_g, SUBLN_EPS) * (1.0 - lambda_init)
    return o.reshape(B, S, D_MODEL) @ w_out


def conformer_conv(h, w_pw1, b_pw1, w_dw, b_dw, ln_g, ln_b, w_pw2, b_pw2):
    a = h @ w_pw1 + b_pw1
    a = a[..., :D_MODEL] * jax.nn.sigmoid(a[..., D_MODEL:])
    a = causal_dwconv(a, w_dw) + b_dw
    a = jax.nn.silu(layernorm(a, ln_g, ln_b))
    return a @ w_pw2 + b_pw2


def sparse_attention(h, w_in, kv_g, w_uk, w_uv, w_out, rel_table):
    B, S, _ = h.shape
    proj = h @ w_in
    o1 = D_MODEL
    o2 = o1 + SA_LATENT
    o3 = o2 + IDX_HEADS * IDX_DIM
    o4 = o3 + IDX_DIM
    q = proj[..., :o1].reshape(B, S, SA_HEADS, SA_DIM)
    ckv = rmsnorm(proj[..., o1:o2], kv_g)
    iq = proj[..., o2:o3].reshape(B, S, IDX_HEADS, IDX_DIM) * (IDX_DIM ** -0.5)
    ik = proj[..., o3:o4]
    iw = proj[..., o4:] * (IDX_HEADS ** -0.5)
    q_lat = jnp.einsum('bshd,chd->bshc', q, w_uk) * (SA_DIM ** -0.5)
    topk = min(TOPK_MAX, S // 4)
    nb = S // SA_QBLOCK
    ql_b = q_lat.reshape(B, nb, SA_QBLOCK, SA_HEADS, SA_LATENT).swapaxes(0, 1)
    iq_b = iq.reshape(B, nb, SA_QBLOCK, IDX_HEADS, IDX_DIM).swapaxes(0, 1)
    iw_b = iw.reshape(B, nb, SA_QBLOCK, IDX_HEADS).swapaxes(0, 1)
    k_pos = jnp.arange(S)
    table = rel_table.astype(jnp.float32)

    def block(args):
        ql, iqb, iwb, bi = args
        q_pos = bi * SA_QBLOCK + jnp.arange(SA_QBLOCK)
        vis = (k_pos[None, :] // CHUNK) <= (q_pos[:, None] // CHUNK)
        rel = jax.nn.relu(jnp.einsum('bqhd,bkd->bqhk', iqb, ik,
                                     preferred_element_type=jnp.float32))
        score = jnp.einsum('bqhk,bqh->bqk', rel, iwb.astype(jnp.float32))
        score = jnp.where(vis[None], score, -jnp.inf)
        _, idx = lax.top_k(score, topk)
        c_sel = jax.vmap(lambda cb, ib: cb[ib])(ckv, idx)
        logits = jnp.einsum('bqhc,bqkc->bhqk', ql, c_sel, preferred_element_type=jnp.float32)
        bias = table[rel_bucket(idx - q_pos[None, :, None])].transpose(0, 3, 1, 2)
        valid = (idx // CHUNK) <= (q_pos[None, :, None] // CHUNK)
        logits = jnp.where(valid[:, None], logits + bias, NEG)
        p = jax.nn.softmax(logits, axis=-1).astype(c_sel.dtype)
        return jnp.einsum('bhqk,bqkc->bqhc', p, c_sel)

    o_lat = lax.map(block, (ql_b, iq_b, iw_b, jnp.arange(nb)))
    o_lat = o_lat.swapaxes(0, 1).reshape(B, S, SA_HEADS, SA_LATENT)
    o = jnp.einsum('bshc,chd->bshd', o_lat, w_uv).reshape(B, S, D_MODEL)
    return o @ w_out


def spatial_gating(h, w_in, b_in, ln_g, ln_b, w_s, b_s, w_out, b_out):
    B, S, _ = h.shape
    a = jax.nn.gelu(h @ w_in + b_in)
    u, v = jnp.split(a, 2, axis=-1)
    v = layernorm(v, ln_g, ln_b)
    n = S // SG_CHUNK
    v = v.reshape(B, n, SG_CHUNK, SG_GROUPS, SG_WIDTH // SG_GROUPS)
    tril = jnp.tril(jnp.ones((SG_CHUNK, SG_CHUNK), dtype=w_s.dtype))
    sv = jnp.einsum('gts,bnsgc->bntgc', w_s * tril, v) + b_s.T[None, None, :, :, None]
    return (u * sv.reshape(B, S, SG_WIDTH)) @ w_out + b_out


def conv_ffn(h, w_up, w_dw, b_dw, w_down):
    a = causal_dwconv(h @ w_up, w_dw) + b_dw
    g, val = jnp.split(a, 2, axis=-1)
    return (jax.nn.silu(g) * val) @ w_down


def setup_inputs(seed: int = 0) -> dict:
    key = jax.random.key(seed)
    ks = iter(jax.random.split(key, 64))
    D = D_MODEL
    nA, nB, nC, nD = _n_of_kind(0), _n_of_kind(1), _n_of_kind(2), _n_of_kind(3)

    def nrm(shape, scale):
        return jax.random.normal(next(ks), shape, jnp.float32) * scale

    def gain(shape):
        return 1.0 + nrm(shape, 0.02)

    return {
        "x": nrm((BATCH, SEQ, D), 1.0),
        "c": nrm((BATCH, D), 1.0),
        "rel_table": nrm((N_BUCKETS, N_ATT_HEADS), 0.5),
        "ada_w": nrm((DEPTH, D, 6 * D), 0.5 * D ** -0.5),
        "ada_b": nrm((DEPTH, 6 * D), 0.02),
        "norm_g": gain((DEPTH, 2, D)),
        "final_g": gain((D,)),
        "da_w_in": nrm((nA, D, 3 * D), D ** -0.5),
        "da_lam": nrm((nA, 4, DA_DIM), 0.1),
        "da_subln_g": gain((nA, DA_VDIM)),
        "da_w_out": nrm((nA, D, D), D ** -0.5),
        "cv_w_pw1": nrm((nB, D, 2 * D), D ** -0.5),
        "cv_b_pw1": nrm((nB, 2 * D), 0.02),
        "cv_w_dw": nrm((nB, CONV_WIDTH, D), CONV_WIDTH ** -0.5),
        "cv_b_dw": nrm((nB, D), 0.02),
        "cv_ln_g": gain((nB, D)),
        "cv_ln_b": nrm((nB, D), 0.02),
        "cv_w_pw2": nrm((nB, D, D), D ** -0.5),
        "cv_b_pw2": nrm((nB, D), 0.02),
        "sa_w_in": nrm((nC, D, SA_IN), D ** -0.5),
        "sa_kv_g": gain((nC, SA_LATENT)),
        "sa_w_uk": nrm((nC, SA_LATENT, SA_HEADS, SA_DIM), SA_LATENT ** -0.5),
        "sa_w_uv": nrm((nC, SA_LATENT, SA_HEADS, SA_DIM), SA_LATENT ** -0.5),
        "sa_w_out": nrm((nC, D, D), D ** -0.5),
        "sg_w_in": nrm((nD, D, 2 * SG_WIDTH), D ** -0.5),
        "sg_b_in": nrm((nD, 2 * SG_WIDTH), 0.02),
        "sg_ln_g": gain((nD, SG_WIDTH)),
        "sg_ln_b": nrm((nD, SG_WIDTH), 0.02),
        "sg_w_s": nrm((nD, SG_GROUPS, SG_CHUNK, SG_CHUNK), SG_CHUNK ** -0.5),
        "sg_b_s": gain((nD, SG_GROUPS, SG_CHUNK)),
        "sg_w_out": nrm((nD, SG_WIDTH, D), SG_WIDTH ** -0.5),
        "sg_b_out": nrm((nD, D), 0.02),
        "ff_w_up": nrm((DEPTH, D, 2 * FFN_DIM), D ** -0.5),
        "ff_w_dw": nrm((DEPTH, FFN_CONV, 2 * FFN_DIM), FFN_CONV ** -0.5),
        "ff_b_dw": nrm((DEPTH, 2 * FFN_DIM), 0.02),
        "ff_w_down": nrm((DEPTH, FFN_DIM, D), FFN_DIM ** -0.5),
    }


def reference(x, c, rel_table, ada_w, ada_b, norm_g, final_g,
              da_w_in, da_lam, da_subln_g, da_w_out,
              cv_w_pw1, cv_b_pw1, cv_w_dw, cv_b_dw, cv_ln_g, cv_ln_b, cv_w_pw2, cv_b_pw2,
              sa_w_in, sa_kv_g, sa_w_uk, sa_w_uv, sa_w_out,
              sg_w_in, sg_b_in, sg_ln_g, sg_ln_b, sg_w_s, sg_b_s, sg_w_out, sg_b_out,
              ff_w_up, ff_w_dw, ff_b_dw, ff_w_down):
    c_act = jax.nn.silu(c)
    for layer in range(DEPTH):
        kind = layer % N_MIXERS
        j = layer // N_MIXERS
        mod = c_act @ ada_w[layer] + ada_b[layer]
        sh1, sc1, g1, sh2, sc2, g2 = jnp.split(mod, 6, axis=-1)
        h = rmsnorm(x, norm_g[layer, 0]) * (1.0 + sc1[:, None]) + sh1[:, None]
        if kind == 0:
            lambda_init = 0.8 - 0.6 * math.exp(-0.3 * layer)
            y = diff_attention(h, da_w_in[j], da_lam[j], da_subln_g[j], da_w_out[j],
                               rel_table, lambda_init)
        elif kind == 1:
            y = conformer_conv(h, cv_w_pw1[j], cv_b_pw1[j], cv_w_dw[j], cv_b_dw[j],
                               cv_ln_g[j], cv_ln_b[j], cv_w_pw2[j], cv_b_pw2[j])
        elif kind == 2:
            y = sparse_attention(h, sa_w_in[j], sa_kv_g[j], sa_w_uk[j], sa_w_uv[j],
                                 sa_w_out[j], rel_table)
        else:
            y = spatial_gating(h, sg_w_in[j], sg_b_in[j], sg_ln_g[j], sg_ln_b[j],
                               sg_w_s[j], sg_b_s[j], sg_w_out[j], sg_b_out[j])
        x = x + g1[:, None] * y
        h = rmsnorm(x, norm_g[layer, 1]) * (1.0 + sc2[:, None]) + sh2[:, None]
        x = x + g2[:, None] * conv_ffn(h, ff_w_up[layer], ff_w_dw[layer], ff_b_dw[layer],
                                       ff_w_down[layer])
    return rmsnorm(x, final_g)
```

```python
import functools
import math

import jax
import jax.numpy as jnp
import numpy as np
from jax import lax
from jax.experimental import pallas as pl
from jax.experimental.pallas import tpu as pltpu

_BF = jnp.bfloat16
_F32 = jnp.float32
_I32 = jnp.int32

EPS = 1e-6
SUBLN_EPS = 1e-5
NEG = -1e30
CHUNK = 64
CHUNK_SHIFT = 6
N_HEADS = 8
HEAD_DIM = 128
DA_DIM = 64
SA_LATENT = 256
IDX_DIM = 64
TOPK = 256
N_BUCKETS = 32
MAX_DISTANCE = 128
SG_CHUNK = 128
SG_GROUPS = 8
CONV_WIDTH = 31
CONV_HALO = 32
FFN_HALO = 8
ATT_TILE = 256
VMEM_LIMIT = 56 * 1024 * 1024


def _dot(a, b):
    return jnp.dot(a, b, preferred_element_type=_F32)


def _dot_nt(a, b):
    return lax.dot_general(a, b, (((1,), (1,)), ((), ())), preferred_element_type=_F32)


def _rms_mod(x, g, sc, sh):
    ms = jnp.mean(x * x, axis=-1, keepdims=True)
    return (x * lax.rsqrt(ms + EPS) * g) * (1.0 + sc) + sh


def _layernorm(x, g, b):
    mu = jnp.mean(x, axis=-1, keepdims=True)
    xc = x - mu
    var = jnp.mean(xc * xc, axis=-1, keepdims=True)
    return xc * lax.rsqrt(var + EPS) * g + b


def _silu(x):
    return x * jax.nn.sigmoid(x)


def _params(sem):
    return pltpu.CompilerParams(dimension_semantics=sem, vmem_limit_bytes=VMEM_LIMIT)


def _full(shape):
    n = len(shape)
    return pl.BlockSpec(shape, lambda *_: (0,) * n, pipeline_mode=pl.Buffered(1))


def _ada_kernel(c_ref, w_ref, b_ref, o_ref):
    ca = _silu(c_ref[...]).astype(_BF)
    o_ref[0] = _dot(ca, w_ref[0].astype(_BF)) + b_ref[0]


def _ada(c, ada_w, ada_b):
    L, D, N = ada_w.shape
    B = c.shape[0]
    tn = N // 4
    return pl.pallas_call(
        _ada_kernel,
        grid=(L, N // tn),
        in_specs=[pl.BlockSpec((B, D), lambda l, j: (0, 0)),
                  pl.BlockSpec((1, D, tn), lambda l, j: (l, 0, j)),
                  pl.BlockSpec((1, 1, tn), lambda l, j: (l, 0, j))],
        out_specs=pl.BlockSpec((1, B, tn), lambda l, j: (l, 0, j)),
        out_shape=jax.ShapeDtypeStruct((L, B, N), _F32),
        compiler_params=_params(("arbitrary", "arbitrary")),
        name="ada",
    )(c, ada_w, ada_b.reshape(L, 1, N))


def _bias_kernel(tbl_ref, bd_ref, bs_ref):
    h = pl.program_id(0)
    T = bd_ref.shape[1]
    i = lax.broadcasted_iota(_I32, (T, T), 0)
    j = lax.broadcasted_iota(_I32, (T, T), 1)
    nb = N_BUCKETS // 2
    max_exact = nb // 2

    def bias_of(rel):
        ret = jnp.where(rel > 0, nb, 0)
        n = jnp.abs(rel)
        nf = jnp.maximum(n, 1).astype(_F32)
        large = max_exact + (jnp.log(nf / max_exact) / math.log(MAX_DISTANCE / max_exact)
                             * (nb - max_exact)).astype(_I32)
        large = jnp.minimum(large, nb - 1)
        bucket = ret + jnp.where(n < max_exact, n, large)
        out = jnp.zeros((T, T), _F32)
        for bk in range(N_BUCKETS):
            out = jnp.where(bucket == bk, tbl_ref[bk, h], out)
        return out

    rel = j - i
    visible = (j >> CHUNK_SHIFT) <= (i >> CHUNK_SHIFT)
    bd_ref[0] = jnp.where(visible, bias_of(rel), NEG)
    bs_ref[0] = bias_of(rel - T)


def _bias_tiles(rel_table):
    T = ATT_TILE
    H = rel_table.shape[1]
    shp = jax.ShapeDtypeStruct((H, T, T), _F32)
    return pl.pallas_call(
        _bias_kernel,
        grid=(H,),
        in_specs=[pl.BlockSpec(memory_space=pltpu.SMEM)],
        out_specs=[pl.BlockSpec((1, T, T), lambda h: (h, 0, 0))] * 2,
        out_shape=[shp, shp],
        compiler_params=_params(("arbitrary",)),
        name="bias_tiles",
    )(rel_table)


def _in_specs_common(tm, D):
    return [pl.BlockSpec((1, tm, D), lambda b, i: (b, i, 0)),
            pl.BlockSpec((1, D), lambda b, i: (0, 0)),
            pl.BlockSpec((1, 1, D), lambda b, i: (b, 0, 0)),
            pl.BlockSpec((1, 1, D), lambda b, i: (b, 0, 0))]


def _in_da_kernel(x_ref, g_ref, sc_ref, sh_ref, w_ref, o_ref):
    hn = _rms_mod(x_ref[0], g_ref[...], sc_ref[0], sh_ref[0]).astype(_BF)
    N = w_ref.shape[1]
    tn = 512
    for c in range(N // tn):
        o_ref[0, :, c * tn:(c + 1) * tn] = _dot(hn, w_ref[:, c * tn:(c + 1) * tn]).astype(_BF)


def _in_da(x, g, sc, sh, w):
    B, S, D = x.shape
    N = w.shape[1]
    tm = 512
    return pl.pallas_call(
        _in_da_kernel,
        grid=(B, S // tm),
        in_specs=_in_specs_common(tm, D) + [_full((D, N))],
        out_specs=pl.BlockSpec((1, tm, N), lambda b, i: (b, i, 0)),
        out_shape=jax.ShapeDtypeStruct((B, S, N), _BF),
        compiler_params=_params(("parallel", "parallel")),
        name="in_da",
    )(x, g, sc, sh, w)


def _in_cv_kernel(x_ref, g_ref, sc_ref, sh_ref, w_ref, b_ref, o_ref):
    hn = _rms_mod(x_ref[0], g_ref[...], sc_ref[0], sh_ref[0]).astype(_BF)
    D = o_ref.shape[2]
    tn = 512
    for c in range(D // tn):
        lo = slice(c * tn, (c + 1) * tn)
        hi = slice(D + c * tn, D + (c + 1) * tn)
        a = _dot(hn, w_ref[:, lo]) + b_ref[:, lo]
        gate = _dot(hn, w_ref[:, hi]) + b_ref[:, hi]
        o_ref[0, :, lo] = a * jax.nn.sigmoid(gate)


def _in_cv(x, g, sc, sh, w, b):
    B, S, D = x.shape
    N = w.shape[1]
    tm = 512
    return pl.pallas_call(
        _in_cv_kernel,
        grid=(B, S // tm),
        in_specs=_in_specs_common(tm, D) + [_full((D, N)), _full((1, N))],
        out_specs=pl.BlockSpec((1, tm, D), lambda b, i: (b, i, 0)),
        out_shape=jax.ShapeDtypeStruct((B, S, D), _F32),
        compiler_params=_params(("parallel", "parallel")),
        name="in_cv",
    )(x, g, sc, sh, w, b)


def _in_sa_kernel(x_ref, g_ref, sc_ref, sh_ref, w_ref, kvg_ref,
                  q_ref, ckv_ref, iq_ref, ikk_ref, iw_ref):
    hn = _rms_mod(x_ref[0], g_ref[...], sc_ref[0], sh_ref[0]).astype(_BF)
    H = q_ref.shape[1]
    for c in range(H // 2):
        r = _dot(hn, w_ref[:, c * 256:(c + 1) * 256])
        q_ref[0, 2 * c] = r[:, :HEAD_DIM].astype(_BF)
        q_ref[0, 2 * c + 1] = r[:, HEAD_DIM:].astype(_BF)
    o = H * HEAD_DIM
    ckv = _dot(hn, w_ref[:, o:o + SA_LATENT])
    ms = jnp.mean(ckv * ckv, axis=-1, keepdims=True)
    ckv_ref[0] = (ckv * lax.rsqrt(ms + EPS) * kvg_ref[...]).astype(_BF)
    o += SA_LATENT
    iq_ref[0] = (_dot(hn, w_ref[:, o:o + 512]) * (IDX_DIM ** -0.5)).astype(_BF)
    o += 512
    r = _dot(hn, w_ref[:, o:o + 256])
    ikk_ref[0] = r[:, :128].astype(_BF)
    iw_ref[0] = r[:, 128:] * (N_HEADS ** -0.5)


def _in_sa(x, g, sc, sh, w, kv_g):
    B, S, D = x.shape
    N = w.shape[1]
    H = N_HEADS
    tm = 512
    row = lambda n: pl.BlockSpec((1, tm, n), lambda b, i: (b, i, 0))
    return pl.pallas_call(
        _in_sa_kernel,
        grid=(B, S // tm),
        in_specs=_in_specs_common(tm, D) + [_full((D, N)), _full((1, SA_LATENT))],
        out_specs=[pl.BlockSpec((1, H, tm, HEAD_DIM), lambda b, i: (b, 0, i, 0)),
                   row(SA_LATENT), row(512), row(128), row(128)],
        out_shape=[jax.ShapeDtypeStruct((B, H, S, HEAD_DIM), _BF),
                   jax.ShapeDtypeStruct((B, S, SA_LATENT), _BF),
                   jax.ShapeDtypeStruct((B, S, 512), _BF),
                   jax.ShapeDtypeStruct((B, S, 128), _BF),
                   jax.ShapeDtypeStruct((B, S, 128), _F32)],
        compiler_params=_params(("parallel", "parallel")),
        name="in_sa",
    )(x, g, sc, sh, w, kv_g)


def _in_sg_kernel(x_ref, g_ref, sc_ref, sh_ref, w_ref, b_ref, lg_ref, lb_ref, u_ref, v_ref, a_scr):
    hn = _rms_mod(x_ref[0], g_ref[...], sc_ref[0], sh_ref[0]).astype(_BF)
    W = u_ref.shape[2]
    tn = 512
    for c in range(W // tn):
        lo = slice(c * tn, (c + 1) * tn)
        hi = slice(W + c * tn, W + (c + 1) * tn)
        u_ref[0, :, lo] = jax.nn.gelu(_dot(hn, w_ref[:, lo]) + b_ref[:, lo], approximate=True)
        a_scr[:, lo] = jax.nn.gelu(_dot(hn, w_ref[:, hi]) + b_ref[:, hi], approximate=True)
    v_ref[0] = _layernorm(a_scr[...], lg_ref[...], lb_ref[...]).astype(_BF)


def _in_sg(x, g, sc, sh, w, b, ln_g, ln_b):
    B, S, D = x.shape
    N = w.shape[1]
    W = N // 2
    tm = 256
    row = lambda: pl.BlockSpec((1, tm, W), lambda b, i: (b, i, 0))
    return pl.pallas_call(
        _in_sg_kernel,
        grid=(B, S // tm),
        in_specs=_in_specs_common(tm, D) + [_full((D, N)), _full((1, N)), _full((1, W)), _full((1, W))],
        out_specs=[row(), row()],
        out_shape=[jax.ShapeDtypeStruct((B, S, W), _F32), jax.ShapeDtypeStruct((B, S, W), _BF)],
        scratch_shapes=[pltpu.VMEM((tm, W), _F32)],
        compiler_params=_params(("parallel", "parallel")),
        name="in_sg",
    )(x, g, sc, sh, w, b, ln_g, ln_b)


def _softmax_step(s, vt, m_ref, l_ref, acc_ref):
    m_prev = m_ref[...]
    m_new = jnp.maximum(m_prev, jnp.max(s, axis=1, keepdims=True))
    alpha = jnp.exp(m_prev - m_new)
    p = jnp.exp(s - m_new)
    l_ref[...] = alpha * l_ref[...] + jnp.sum(p, axis=1, keepdims=True)
    acc_ref[...] = alpha * acc_ref[...] + _dot(p.astype(_BF), vt)
    m_ref[...] = m_new


def _da_kernel(tbl_ref, q_ref, k_ref, v_ref, bd_ref, bs_ref, lam_ref, g_ref, o_ref,
               qs_ref, m_ref, l_ref, acc_ref, *, lambda_init):
    h = pl.program_id(1)
    qi = pl.program_id(2)
    T = q_ref.shape[1]
    lane = lax.broadcasted_iota(_I32, (T, HEAD_DIM), 1)
    q = q_ref[0] * (DA_DIM ** -0.5)
    zero = jnp.zeros_like(q)
    qs_ref[0] = jnp.where(lane < DA_DIM, q, zero)
    qs_ref[1] = jnp.where(lane >= DA_DIM, q, zero)
    m_ref[...] = jnp.full(m_ref.shape, -jnp.inf, _F32)
    l_ref[...] = jnp.zeros(l_ref.shape, _F32)
    acc_ref[...] = jnp.zeros(acc_ref.shape, _F32)

    def step(ki, bias):
        rows = pl.ds(pl.multiple_of(ki * T, T), T)
        kt = k_ref[0, rows, :]
        vt = v_ref[0, rows, :]
        for c in range(2):
            s = _dot_nt(qs_ref[c], kt) + bias
            _softmax_step(s, vt, m_ref.at[c], l_ref.at[c], acc_ref.at[c])

    far = tbl_ref[N_BUCKETS // 2 - 1, h]

    def far_body(ki, carry):
        step(ki, far)
        return carry

    lax.fori_loop(0, jnp.maximum(qi - 1, 0), far_body, 0)

    @pl.when(qi >= 1)
    def _():
        step(qi - 1, bs_ref[0])

    step(qi, bd_ref[0])

    lam = lam_ref[...]
    lam_full = (jnp.exp(jnp.sum(lam[0:1] * lam[1:2], axis=1, keepdims=True))
                - jnp.exp(jnp.sum(lam[2:3] * lam[3:4], axis=1, keepdims=True)) + lambda_init)
    o = acc_ref[0] / l_ref[0] - lam_full * (acc_ref[1] / l_ref[1])
    ms = jnp.mean(o * o, axis=-1, keepdims=True)
    o = (o * lax.rsqrt(ms + SUBLN_EPS) * g_ref[...]) * (1.0 - lambda_init)
    o_ref[0] = o.astype(_BF)


def _diff_attn(qkv, rel_table, bd, bs, lam, subln_g, lambda_init):
    B, S, _ = qkv.shape
    H = N_HEADS
    T = ATT_TILE
    kern = functools.partial(_da_kernel, lambda_init=lambda_init)
    return pl.pallas_call(
        kern,
        grid=(B, H, S // T),
        in_specs=[pl.BlockSpec(memory_space=pltpu.SMEM),
                  pl.BlockSpec((1, T, HEAD_DIM), lambda b, h, i: (b, i, h)),
                  pl.BlockSpec((1, S, HEAD_DIM), lambda b, h, i: (b, 0, H + h)),
                  pl.BlockSpec((1, S, HEAD_DIM), lambda b, h, i: (b, 0, 2 * H + h)),
                  pl.BlockSpec((1, T, T), lambda b, h, i: (h, 0, 0)),
                  pl.BlockSpec((1, T, T), lambda b, h, i: (h, 0, 0)),
                  pl.BlockSpec((4, DA_DIM), lambda b, h, i: (0, 0)),
                  pl.BlockSpec((1, HEAD_DIM), lambda b, h, i: (0, 0))],
        out_specs=pl.BlockSpec((1, T, HEAD_DIM), lambda b, h, i: (b, i, h)),
        out_shape=jax.ShapeDtypeStruct((B, S, H * HEAD_DIM), _BF),
        scratch_shapes=[pltpu.VMEM((2, T, HEAD_DIM), _BF),
                        pltpu.VMEM((2, T, 1), _F32),
                        pltpu.VMEM((2, T, 1), _F32),
                        pltpu.VMEM((2, T, HEAD_DIM), _F32)],
        compiler_params=_params(("parallel", "parallel", "arbitrary")),
        name="diff_attn",
    )(rel_table, qkv, qkv, qkv, bd, bs, lam, subln_g)


_INT_MIN = np.int32(-2 ** 31)


def _sa_kernel(tbl_ref, q_ref, ckv_ref, iq_ref, ikk_ref, iw_ref, wuk_ref, wuv_ref, bd_ref, bs_ref,
               o_ref, keys_ref, madd_ref, oh_ref, m_ref, l_ref, acc_ref):
    qi = pl.program_id(1)
    T = iq_ref.shape[1]
    H = q_ref.shape[1]
    nkv = qi + 1
    lane = lax.broadcasted_iota(_I32, (T, 128), 1)
    row = lax.broadcasted_iota(_I32, (T, T), 0)
    col = lax.broadcasted_iota(_I32, (T, T), 1)
    iw = iw_ref[0]

    def score_body(ki, carry):
        ikt = ikk_ref[0, pl.ds(pl.multiple_of(ki * T, T), T), :]
        score = jnp.zeros((T, T), _F32)
        for p in range(H // 2):
            iqp = iq_ref[0, :, p * 128:(p + 1) * 128]
            zero = jnp.zeros_like(iqp)
            r_lo = _dot_nt(jnp.where(lane < IDX_DIM, iqp, zero), ikt)
            r_hi = _dot_nt(jnp.where(lane >= IDX_DIM, iqp, zero), ikt)
            score = score + jnp.maximum(r_lo, 0.0) * iw[:, 2 * p:2 * p + 1]
            score = score + jnp.maximum(r_hi, 0.0) * iw[:, 2 * p + 1:2 * p + 2]
        score = jnp.where(score == 0.0, 0.0, score)
        visible = ((col + ki * T) >> CHUNK_SHIFT) <= ((row + qi * T) >> CHUNK_SHIFT)
        score = jnp.where(visible, score, -jnp.inf)
        bits = lax.bitcast_convert_type(score, _I32)
        keys_ref[ki] = bits ^ ((bits >> 31) & np.int32(0x7FFFFFFF))
        return carry

    lax.fori_loop(0, nkv, score_body, 0)

    def count(pred):
        def body(ki, c):
            return c + jnp.sum(jnp.where(pred(ki, keys_ref[ki]), 1.0, 0.0), axis=1, keepdims=True)
        return lax.fori_loop(0, nkv, body, jnp.zeros((T, 1), _F32))

    def bit_body(it, thr_u):
        cand_u = thr_u | lax.shift_left(np.int32(1), np.int32(31) - it)
        cand = cand_u ^ _INT_MIN
        cnt = count(lambda ki, k: k >= cand)
        return jnp.where(cnt >= float(TOPK), cand_u, thr_u)

    thr = lax.fori_loop(0, 32, bit_body, jnp.zeros((T, 1), _I32)) ^ _INT_MIN

    cnt_gt = count(lambda ki, k: k > thr)
    cnt_ge = count(lambda ki, k: k >= thr)
    need = float(TOPK) - cnt_gt
    m_ref[...] = jnp.full((T, 1), float(2 ** 30), _F32)

    @pl.when(jnp.max(cnt_ge - cnt_gt - need) > 0.0)
    def _():
        def idx_body(it, v):
            cand = v | lax.shift_left(np.int32(1), np.int32(10) - it)
            below = count(lambda ki, k: jnp.logical_and(k == thr, col + ki * T < cand))
            return jnp.where(below < need, cand, v)
        v = lax.fori_loop(0, 11, idx_body, jnp.zeros((T, 1), _I32))
        m_ref[...] = v.astype(_F32)

    last = m_ref[...].astype(_I32)

    def mask_body(ki, carry):
        k = keys_ref[ki]
        sel = jnp.logical_or(k > thr, jnp.logical_and(k == thr, col + ki * T <= last))
        madd_ref[ki] = jnp.where(sel, 0.0, NEG)
        return carry

    lax.fori_loop(0, nkv, mask_body, 0)

    def head_body(h, carry):
        qlat = (_dot(q_ref[0, h], wuk_ref[h]) * (HEAD_DIM ** -0.5)).astype(_BF)
        m_ref[...] = jnp.full((T, 1), -jnp.inf, _F32)
        l_ref[...] = jnp.zeros((T, 1), _F32)
        acc_ref[...] = jnp.zeros((T, SA_LATENT), _F32)

        def step(ki, bias):
            ct = ckv_ref[0, pl.ds(pl.multiple_of(ki * T, T), T), :]
            s = _dot_nt(qlat, ct) + bias + madd_ref[ki]
            _softmax_step(s, ct, m_ref, l_ref, acc_ref)

        far = tbl_ref[N_BUCKETS // 2 - 1, h]

        def far_body(ki, c):
            step(ki, far)
            return c

        lax.fori_loop(0, jnp.maximum(qi - 1, 0), far_body, 0)

        @pl.when(qi >= 1)
        def _():
            step(qi - 1, bs_ref[h])

        step(qi, bd_ref[h])
        olat = (acc_ref[...] / l_ref[...]).astype(_BF)
        oh_ref[h] = _dot(olat, wuv_ref[h])
        return carry

    lax.fori_loop(0, H, head_body, 0)
    for h in range(H):
        o_ref[0, :, h * HEAD_DIM:(h + 1) * HEAD_DIM] = oh_ref[h].astype(_BF)


def _sparse_attn(q, ckv, iq, ikk, iw, wuk, wuv, rel_table, bd, bs):
    B, H, S, _ = q.shape
    T = ATT_TILE
    nt = S // T
    return pl.pallas_call(
        _sa_kernel,
        grid=(B, nt),
        in_specs=[pl.BlockSpec(memory_space=pltpu.SMEM),
                  pl.BlockSpec((1, H, T, HEAD_DIM), lambda b, i: (b, 0, i, 0)),
                  pl.BlockSpec((1, S, SA_LATENT), lambda b, i: (b, 0, 0)),
                  pl.BlockSpec((1, T, 512), lambda b, i: (b, i, 0)),
                  pl.BlockSpec((1, S, 128), lambda b, i: (b, 0, 0)),
                  pl.BlockSpec((1, T, 128), lambda b, i: (b, i, 0)),
                  _full((H, HEAD_DIM, SA_LATENT)),
                  _full((H, SA_LATENT, HEAD_DIM)),
                  _full((H, T, T)),
                  _full((H, T, T))],
        out_specs=pl.BlockSpec((1, T, H * HEAD_DIM), lambda b, i: (b, i, 0)),
        out_shape=jax.ShapeDtypeStruct((B, S, H * HEAD_DIM), _BF),
        scratch_shapes=[pltpu.VMEM((nt, T, T), _I32),
                        pltpu.VMEM((nt, T, T), _F32),
                        pltpu.VMEM((H, T, HEAD_DIM), _F32),
                        pltpu.VMEM((T, 1), _F32),
                        pltpu.VMEM((T, 1), _F32),
                        pltpu.VMEM((T, SA_LATENT), _F32)],
        compiler_params=_params(("parallel", "arbitrary")),
        name="sparse_attn",
    )(rel_table, q, ckv, iq, ikk, iw, wuk, wuv, bd, bs)


def _cv_kernel(a_ref, ah_ref, wdw_ref, bdw_ref, lg_ref, lb_ref, w2_ref, b2_ref, x_ref, g1_ref, o_ref,
               ext_ref, y_ref):
    i = pl.program_id(1)
    tm = a_ref.shape[1]
    ext_ref[CONV_HALO:CONV_HALO + tm] = a_ref[0]
    halo = ah_ref[0]
    ext_ref[0:CONV_HALO] = jnp.where(i > 0, halo, jnp.zeros_like(halo))
    rb = 32
    base = CONV_HALO - (CONV_WIDTH - 1)
    for r in range(tm // rb):
        acc = wdw_ref[0:1, :] * ext_ref[r * rb + base:r * rb + base + rb, :] + bdw_ref[...]
        for k in range(1, CONV_WIDTH):
            acc = acc + wdw_ref[k:k + 1, :] * ext_ref[r * rb + base + k:r * rb + base + k + rb, :]
        y_ref[r * rb:(r + 1) * rb] = _silu(_layernorm(acc, lg_ref[...], lb_ref[...])).astype(_BF)
    y = _dot(y_ref[...], w2_ref[...]) + b2_ref[...]
    o_ref[0] = x_ref[0] + g1_ref[0] * y


def _cv_tail(a, w_dw, b_dw, ln_g, ln_b, w2, b2, x, g1):
    B, S, D = a.shape
    tm = 256
    hb = tm // CONV_HALO
    vec = lambda: _full((1, D))
    return pl.pallas_call(
        _cv_kernel,
        grid=(B, S // tm),
        in_specs=[pl.BlockSpec((1, tm, D), lambda b, i: (b, i, 0)),
                  pl.BlockSpec((1, CONV_HALO, D), lambda b, i: (b, jnp.maximum(i * hb - 1, 0), 0)),
                  _full((CONV_WIDTH, D)), vec(), vec(), vec(), _full((D, D)), vec(),
                  pl.BlockSpec((1, tm, D), lambda b, i: (b, i, 0)),
                  pl.BlockSpec((1, 1, D), lambda b, i: (b, 0, 0))],
        out_specs=pl.BlockSpec((1, tm, D), lambda b, i: (b, i, 0)),
        out_shape=jax.ShapeDtypeStruct((B, S, D), _F32),
        scratch_shapes=[pltpu.VMEM((CONV_HALO + tm, D), _F32), pltpu.VMEM((tm, D), _BF)],
        compiler_params=_params(("parallel", "parallel")),
        name="cv_tail",
    )(a, a, w_dw, b_dw, ln_g, ln_b, w2, b2, x, g1)


def _sg_kernel(u_ref, v_ref, ws_ref, bs_ref, wo_ref, bo_ref, x_ref, g1_ref, o_ref, z_ref):
    tm = u_ref.shape[1]
    G = ws_ref.shape[0]
    C = SG_CHUNK
    gw = u_ref.shape[2] // G
    t = lax.broadcasted_iota(_I32, (C, C), 0)
    s = lax.broadcasted_iota(_I32, (C, C), 1)
    for g in range(G):
        wg = jnp.where(t >= s, ws_ref[g], 0.0).astype(_BF)
        bcol = bs_ref[:, g:g + 1]
        cols = slice(g * gw, (g + 1) * gw)
        for n in range(tm // C):
            rows = slice(n * C, (n + 1) * C)
            sv = _dot(wg, v_ref[0, rows, cols]) + bcol
            z_ref[rows, cols] = (u_ref[0, rows, cols] * sv).astype(_BF)
    y = _dot(z_ref[...], wo_ref[...]) + bo_ref[...]
    o_ref[0] = x_ref[0] + g1_ref[0] * y


def _sg_tail(u, v, w_s, b_s_t, w_out, b_out, x, g1):
    B, S, W = u.shape
    D = x.shape[2]
    G = w_s.shape[0]
    tm = 256
    return pl.pallas_call(
        _sg_kernel,
        grid=(B, S // tm),
        in_specs=[pl.BlockSpec((1, tm, W), lambda b, i: (b, i, 0)),
                  pl.BlockSpec((1, tm, W), lambda b, i: (b, i, 0)),
                  _full((G, SG_CHUNK, SG_CHUNK)), _full((SG_CHUNK, G)),
                  _full((W, D)), _full((1, D)),
                  pl.BlockSpec((1, tm, D), lambda b, i: (b, i, 0)),
                  pl.BlockSpec((1, 1, D), lambda b, i: (b, 0, 0))],
        out_specs=pl.BlockSpec((1, tm, D), lambda b, i: (b, i, 0)),
        out_shape=jax.ShapeDtypeStruct((B, S, D), _F32),
        scratch_shapes=[pltpu.VMEM((tm, W), _BF)],
        compiler_params=_params(("parallel", "parallel")),
        name="sg_tail",
    )(u, v, w_s, b_s_t, w_out, b_out, x, g1)


def _out_kernel(a_ref, w_ref, x_ref, g1_ref, o_ref):
    o_ref[0] = x_ref[0] + g1_ref[0] * _dot(a_ref[0], w_ref[...])


def _out_proj(a, w, x, g1):
    B, S, D = x.shape
    K = a.shape[2]
    tm = 512
    return pl.pallas_call(
        _out_kernel,
        grid=(B, S // tm),
        in_specs=[pl.BlockSpec((1, tm, K), lambda b, i: (b, i, 0)),
                  _full((K, D)),
                  pl.BlockSpec((1, tm, D), lambda b, i: (b, i, 0)),
                  pl.BlockSpec((1, 1, D), lambda b, i: (b, 0, 0))],
        out_specs=pl.BlockSpec((1, tm, D), lambda b, i: (b, i, 0)),
        out_shape=jax.ShapeDtypeStruct((B, S, D), _F32),
        compiler_params=_params(("parallel", "parallel")),
        name="out_proj",
    )(a, w, x, g1)


def _ffn_kernel(x_ref, xh_ref, g_ref, sc_ref, sh_ref, g2_ref, wup_ref, wdw_ref, bdw_ref, wdn_ref,
                fg_ref, o_ref, he_ref, acc_ref, *, final):
    i = pl.program_id(1)
    tm = x_ref.shape[1]
    F = wdn_ref.shape[0]
    x = x_ref[0]
    he_ref[FFN_HALO:FFN_HALO + tm] = _rms_mod(x, g_ref[...], sc_ref[0], sh_ref[0]).astype(_BF)
    hh = _rms_mod(xh_ref[0], g_ref[...], sc_ref[0], sh_ref[0])
    he_ref[0:FFN_HALO] = jnp.where(i > 0, hh, jnp.zeros_like(hh)).astype(_BF)
    tf = 256

    def conv(col0):
        cols = slice(col0, col0 + tf)
        a = _dot(he_ref[...], wup_ref[:, cols])
        w = wdw_ref[:, cols]
        return (a[FFN_HALO - 2:FFN_HALO - 2 + tm] * w[0:1]
                + a[FFN_HALO - 1:FFN_HALO - 1 + tm] * w[1:2]
                + a[FFN_HALO:FFN_HALO + tm] * w[2:3] + bdw_ref[:, cols])

    for c in range(F // tf):
        gate = conv(c * tf)
        val = conv(F + c * tf)
        act = (_silu(gate) * val).astype(_BF)
        part = _dot(act, wdn_ref[c * tf:(c + 1) * tf, :])
        if c == 0:
            acc_ref[...] = part
        else:
            acc_ref[...] += part
    xn = x + g2_ref[0] * acc_ref[...]
    if final:
        ms = jnp.mean(xn * xn, axis=-1, keepdims=True)
        xn = xn * lax.rsqrt(ms + EPS) * fg_ref[...]
    o_ref[0] = xn


def _ffn(x, g, sc, sh, g2, w_up, w_dw, b_dw, w_down, final_g, final):
    B, S, D = x.shape
    F = w_down.shape[0]
    tm = 512
    hb = tm // FFN_HALO
    kern = functools.partial(_ffn_kernel, final=final)
    mod = lambda: pl.BlockSpec((1, 1, D), lambda b, i: (b, 0, 0))
    return pl.pallas_call(
        kern,
        grid=(B, S // tm),
        in_specs=[pl.BlockSpec((1, tm, D), lambda b, i: (b, i, 0)),
                  pl.BlockSpec((1, FFN_HALO, D), lambda b, i: (b, jnp.maximum(i * hb - 1, 0), 0)),
                  _full((1, D)), mod(), mod(), mod(),
                  _full((D, 2 * F)), _full((3, 2 * F)), _full((1, 2 * F)), _full((F, D)),
                  _full((1, D))],
        out_specs=pl.BlockSpec((1, tm, D), lambda b, i: (b, i, 0)),
        out_shape=jax.ShapeDtypeStruct((B, S, D), _F32),
        scratch_shapes=[pltpu.VMEM((FFN_HALO + tm, D), _BF), pltpu.VMEM((tm, D), _F32)],
        compiler_params=_params(("parallel", "parallel")),
        name="ffn",
    )(x, x, g, sc, sh, g2, w_up, w_dw, b_dw, w_down, final_g)


def kernel(x, c, rel_table, ada_w, ada_b, norm_g, final_g, da_w_in, da_lam, da_subln_g, da_w_out, cv_w_pw1, cv_b_pw1, cv_w_dw, cv_b_dw, cv_ln_g, cv_ln_b, cv_w_pw2, cv_b_pw2, sa_w_in, sa_kv_g, sa_w_uk, sa_w_uv, sa_w_out, sg_w_in, sg_b_in, sg_ln_g, sg_ln_b, sg_w_s, sg_b_s, sg_w_out, sg_b_out, ff_w_up, ff_w_dw, ff_b_dw, ff_w_down):
    B, S, D = x.shape
    depth = ada_w.shape[0]
    n_mixers = 4
    H = N_HEADS
    mods = _ada(c, ada_w, ada_b)
    bd, bs = _bias_tiles(rel_table)
    row = lambda v: v.reshape(1, -1)

    for layer in range(depth):
        kind = layer % n_mixers
        j = layer // n_mixers
        sh1, sc1, g1, sh2, sc2, g2 = [m.reshape(B, 1, D) for m in jnp.split(mods[layer], 6, axis=-1)]
        ng1 = row(norm_g[layer, 0])
        if kind == 0:
            lambda_init = 0.8 - 0.6 * math.exp(-0.3 * layer)
            qkv = _in_da(x, ng1, sc1, sh1, da_w_in[j].astype(_BF))
            o = _diff_attn(qkv, rel_table, bd, bs, da_lam[j], row(da_subln_g[j]), lambda_init)
            x = _out_proj(o, da_w_out[j].astype(_BF), x, g1)
        elif kind == 1:
            a = _in_cv(x, ng1, sc1, sh1, cv_w_pw1[j].astype(_BF), row(cv_b_pw1[j]))
            x = _cv_tail(a, cv_w_dw[j], row(cv_b_dw[j]), row(cv_ln_g[j]), row(cv_ln_b[j]),
                         cv_w_pw2[j].astype(_BF), row(cv_b_pw2[j]), x, g1)
        elif kind == 2:
            w = sa_w_in[j]
            o1 = H * HEAD_DIM
            o2 = o1 + SA_LATENT
            o3 = o2 + H * IDX_DIM
            o4 = o3 + IDX_DIM
            w_ik = w[:, o3:o4]
            w_iw = jnp.pad(w[:, o4:], ((0, 0), (0, 128 - (w.shape[1] - o4))))
            w_cat = jnp.concatenate([w[:, :o3], w_ik, w_ik, w_iw], axis=1).astype(_BF)
            q, ckv, iq, ikk, iw = _in_sa(x, ng1, sc1, sh1, w_cat, row(sa_kv_g[j]))
            wuk = jnp.transpose(sa_w_uk[j], (1, 2, 0)).astype(_BF)
            wuv = jnp.transpose(sa_w_uv[j], (1, 0, 2)).astype(_BF)
            o = _sparse_attn(q, ckv, iq, ikk, iw, wuk, wuv, rel_table, bd, bs)
            x = _out_proj(o, sa_w_out[j].astype(_BF), x, g1)
        else:
            u, v = _in_sg(x, ng1, sc1, sh1, sg_w_in[j].astype(_BF), row(sg_b_in[j]),
                          row(sg_ln_g[j]), row(sg_ln_b[j]))
            x = _sg_tail(u, v, sg_w_s[j], sg_b_s[j].T, sg_w_out[j].astype(_BF), row(sg_b_out[j]), x, g1)
        x = _ffn(x, row(norm_g[layer, 1]), sc2, sh2, g2, ff_w_up[layer].astype(_BF), ff_w_dw[layer],
                 row(ff_b_dw[layer]), ff_w_down[layer].astype(_BF), row(final_g), layer == depth - 1)
    return x
```

```python
import functools
import math

import jax
import jax.numpy as jnp
import numpy as np
from jax import lax
from jax.experimental import pallas as pl
from jax.experimental.pallas import tpu as pltpu

_BF = jnp.bfloat16
_F32 = jnp.float32
_I32 = jnp.int32

EPS = 1e-6
SUBLN_EPS = 1e-5
NEG = -1e30
CHUNK = 64
CHUNK_SHIFT = 6
N_HEADS = 8
HEAD_DIM = 128
DA_DIM = 64
SA_LATENT = 256
IDX_DIM = 64
TOPK = 256
N_BUCKETS = 32
MAX_DISTANCE = 128
SG_CHUNK = 128
SG_GROUPS = 8
CONV_WIDTH = 31
CONV_HALO = 32
FFN_HALO = 8
ATT_TILE = 512
NEAR_W = 768
VMEM_LIMIT = 56 * 1024 * 1024


def _dot(a, b):
    return jnp.dot(a, b, preferred_element_type=_F32)


def _dot_nt(a, b):
    return lax.dot_general(a, b, (((1,), (1,)), ((), ())), preferred_element_type=_F32)


def _rms_mod(x, g, sc, sh):
    ms = jnp.mean(x * x, axis=-1, keepdims=True)
    return (x * lax.rsqrt(ms + EPS) * g) * (1.0 + sc) + sh


def _layernorm(x, g, b):
    mu = jnp.mean(x, axis=-1, keepdims=True)
    xc = x - mu
    var = jnp.mean(xc * xc, axis=-1, keepdims=True)
    return xc * lax.rsqrt(var + EPS) * g + b


def _silu(x):
    return x * jax.nn.sigmoid(x)


def _params(sem):
    return pltpu.CompilerParams(dimension_semantics=sem, vmem_limit_bytes=VMEM_LIMIT)


def _full(shape):
    n = len(shape)
    return pl.BlockSpec(shape, lambda *_: (0,) * n, pipeline_mode=pl.Buffered(1))


def _ada_kernel(c_ref, w_ref, b_ref, o_ref):
    ca = _silu(c_ref[...]).astype(_BF)
    o_ref[0] = _dot(ca, w_ref[0].astype(_BF)) + b_ref[0]


def _ada(c, ada_w, ada_b):
    L, D, N = ada_w.shape
    B = c.shape[0]
    tn = N // 4
    return pl.pallas_call(
        _ada_kernel,
        grid=(L, N // tn),
        in_specs=[pl.BlockSpec((B, D), lambda l, j: (0, 0)),
                  pl.BlockSpec((1, D, tn), lambda l, j: (l, 0, j)),
                  pl.BlockSpec((1, 1, tn), lambda l, j: (l, 0, j))],
        out_specs=pl.BlockSpec((1, B, tn), lambda l, j: (l, 0, j)),
        out_shape=jax.ShapeDtypeStruct((L, B, N), _F32),
        compiler_params=_params(("arbitrary", "arbitrary")),
        name="ada",
    )(c, ada_w, ada_b.reshape(L, 1, N))


def _bias_kernel(tbl_ref, bn_ref):
    h = pl.program_id(0)
    shape = bn_ref.shape[1:]
    i = lax.broadcasted_iota(_I32, shape, 0)
    j = lax.broadcasted_iota(_I32, shape, 1) - (NEAR_W - ATT_TILE)
    nb = N_BUCKETS // 2
    max_exact = nb // 2
    rel = j - i
    ret = jnp.where(rel > 0, nb, 0)
    n = jnp.abs(rel)
    nf = jnp.maximum(n, 1).astype(_F32)
    large = max_exact + (jnp.log(nf / max_exact) / math.log(MAX_DISTANCE / max_exact)
                         * (nb - max_exact)).astype(_I32)
    large = jnp.minimum(large, nb - 1)
    bucket = ret + jnp.where(n < max_exact, n, large)
    out = jnp.zeros(shape, _F32)
    for bk in range(N_BUCKETS):
        out = jnp.where(bucket == bk, tbl_ref[bk, h], out)
    visible = (j >> CHUNK_SHIFT) <= (i >> CHUNK_SHIFT)
    bn_ref[0] = jnp.where(visible, out, NEG)


def _bias_tiles(rel_table):
    H = rel_table.shape[1]
    return pl.pallas_call(
        _bias_kernel,
        grid=(H,),
        in_specs=[pl.BlockSpec(memory_space=pltpu.SMEM)],
        out_specs=pl.BlockSpec((1, ATT_TILE, NEAR_W), lambda h: (h, 0, 0)),
        out_shape=jax.ShapeDtypeStruct((H, ATT_TILE, NEAR_W), _F32),
        compiler_params=_params(("arbitrary",)),
        name="bias_tiles",
    )(rel_table)


def _in_specs_common(tm, D):
    return [pl.BlockSpec((1, tm, D), lambda b, i: (b, i, 0)),
            pl.BlockSpec((1, D), lambda b, i: (0, 0)),
            pl.BlockSpec((1, 1, D), lambda b, i: (b, 0, 0)),
            pl.BlockSpec((1, 1, D), lambda b, i: (b, 0, 0))]


def _in_da_kernel(x_ref, g_ref, sc_ref, sh_ref, w_ref, o_ref):
    hn = _rms_mod(x_ref[0], g_ref[...], sc_ref[0], sh_ref[0]).astype(_BF)
    N = w_ref.shape[1]
    tn = 512
    for c in range(N // tn):
        o_ref[0, :, c * tn:(c + 1) * tn] = _dot(hn, w_ref[:, c * tn:(c + 1) * tn]).astype(_BF)


def _in_da(x, g, sc, sh, w):
    B, S, D = x.shape
    N = w.shape[1]
    tm = 512
    return pl.pallas_call(
        _in_da_kernel,
        grid=(B, S // tm),
        in_specs=_in_specs_common(tm, D) + [_full((D, N))],
        out_specs=pl.BlockSpec((1, tm, N), lambda b, i: (b, i, 0)),
        out_shape=jax.ShapeDtypeStruct((B, S, N), _BF),
        compiler_params=_params(("parallel", "parallel")),
        name="in_da",
    )(x, g, sc, sh, w)


def _in_cv_kernel(x_ref, g_ref, sc_ref, sh_ref, w_ref, b_ref, o_ref):
    hn = _rms_mod(x_ref[0], g_ref[...], sc_ref[0], sh_ref[0]).astype(_BF)
    D = o_ref.shape[2]
    tn = 512
    for c in range(D // tn):
        lo = slice(c * tn, (c + 1) * tn)
        hi = slice(D + c * tn, D + (c + 1) * tn)
        a = _dot(hn, w_ref[:, lo]) + b_ref[:, lo]
        gate = _dot(hn, w_ref[:, hi]) + b_ref[:, hi]
        o_ref[0, :, lo] = a * jax.nn.sigmoid(gate)


def _in_cv(x, g, sc, sh, w, b):
    B, S, D = x.shape
    N = w.shape[1]
    tm = 512
    return pl.pallas_call(
        _in_cv_kernel,
        grid=(B, S // tm),
        in_specs=_in_specs_common(tm, D) + [_full((D, N)), _full((1, N))],
        out_specs=pl.BlockSpec((1, tm, D), lambda b, i: (b, i, 0)),
        out_shape=jax.ShapeDtypeStruct((B, S, D), _F32),
        compiler_params=_params(("parallel", "parallel")),
        name="in_cv",
    )(x, g, sc, sh, w, b)


def _in_sa_kernel(x_ref, g_ref, sc_ref, sh_ref, w_ref, kvg_ref,
                  q_ref, ckv_ref, iq_ref, ikk_ref, iw_ref):
    hn = _rms_mod(x_ref[0], g_ref[...], sc_ref[0], sh_ref[0]).astype(_BF)
    H = q_ref.shape[1]
    for c in range(H // 2):
        r = _dot(hn, w_ref[:, c * 256:(c + 1) * 256])
        q_ref[0, 2 * c] = r[:, :HEAD_DIM].astype(_BF)
        q_ref[0, 2 * c + 1] = r[:, HEAD_DIM:].astype(_BF)
    o = H * HEAD_DIM
    ckv = _dot(hn, w_ref[:, o:o + SA_LATENT])
    ms = jnp.mean(ckv * ckv, axis=-1, keepdims=True)
    ckv_ref[0] = (ckv * lax.rsqrt(ms + EPS) * kvg_ref[...]).astype(_BF)
    o += SA_LATENT
    iq_ref[0] = (_dot(hn, w_ref[:, o:o + 512]) * (IDX_DIM ** -0.5)).astype(_BF)
    o += 512
    r = _dot(hn, w_ref[:, o:o + 256])
    ikk_ref[0] = r[:, :128].astype(_BF)
    iw_ref[0] = r[:, 128:] * (N_HEADS ** -0.5)


def _in_sa(x, g, sc, sh, w, kv_g):
    B, S, D = x.shape
    N = w.shape[1]
    H = N_HEADS
    tm = 512
    row = lambda n: pl.BlockSpec((1, tm, n), lambda b, i: (b, i, 0))
    return pl.pallas_call(
        _in_sa_kernel,
        grid=(B, S // tm),
        in_specs=_in_specs_common(tm, D) + [_full((D, N)), _full((1, SA_LATENT))],
        out_specs=[pl.BlockSpec((1, H, tm, HEAD_DIM), lambda b, i: (b, 0, i, 0)),
                   row(SA_LATENT), row(512), row(128), row(128)],
        out_shape=[jax.ShapeDtypeStruct((B, H, S, HEAD_DIM), _BF),
                   jax.ShapeDtypeStruct((B, S, SA_LATENT), _BF),
                   jax.ShapeDtypeStruct((B, S, 512), _BF),
                   jax.ShapeDtypeStruct((B, S, 128), _BF),
                   jax.ShapeDtypeStruct((B, S, 128), _F32)],
        compiler_params=_params(("parallel", "parallel")),
        name="in_sa",
    )(x, g, sc, sh, w, kv_g)


def _in_sg_kernel(x_ref, g_ref, sc_ref, sh_ref, w_ref, b_ref, lg_ref, lb_ref, u_ref, v_ref, a_scr):
    hn = _rms_mod(x_ref[0], g_ref[...], sc_ref[0], sh_ref[0]).astype(_BF)
    W = u_ref.shape[2]
    tn = 512
    for c in range(W // tn):
        lo = slice(c * tn, (c + 1) * tn)
        hi = slice(W + c * tn, W + (c + 1) * tn)
        u_ref[0, :, lo] = jax.nn.gelu(_dot(hn, w_ref[:, lo]) + b_ref[:, lo], approximate=True)
        a_scr[:, lo] = jax.nn.gelu(_dot(hn, w_ref[:, hi]) + b_ref[:, hi], approximate=True)
    v_ref[0] = _layernorm(a_scr[...], lg_ref[...], lb_ref[...]).astype(_BF)


def _in_sg(x, g, sc, sh, w, b, ln_g, ln_b):
    B, S, D = x.shape
    N = w.shape[1]
    W = N // 2
    tm = 256
    row = lambda: pl.BlockSpec((1, tm, W), lambda b, i: (b, i, 0))
    return pl.pallas_call(
        _in_sg_kernel,
        grid=(B, S // tm),
        in_specs=_in_specs_common(tm, D) + [_full((D, N)), _full((1, N)), _full((1, W)), _full((1, W))],
        out_specs=[row(), row()],
        out_shape=[jax.ShapeDtypeStruct((B, S, W), _F32), jax.ShapeDtypeStruct((B, S, W), _BF)],
        scratch_shapes=[pltpu.VMEM((tm, W), _F32)],
        compiler_params=_params(("parallel", "parallel")),
        name="in_sg",
    )(x, g, sc, sh, w, b, ln_g, ln_b)


def _attend(q, k_ref, v_ref, W, far, near_bias, madd_ref=None):
    nw = min(W, NEAR_W)
    fw = W - nw
    s_n = _dot_nt(q, k_ref[0, fw:W, :]) + near_bias(NEAR_W - nw)
    if madd_ref is not None:
        s_n = s_n + madd_ref[:, fw:W]
    m = jnp.max(s_n, axis=1, keepdims=True)
    if fw:
        s_f = _dot_nt(q, k_ref[0, 0:fw, :]) + far
        if madd_ref is not None:
            s_f = s_f + madd_ref[:, 0:fw]
        m = jnp.maximum(m, jnp.max(s_f, axis=1, keepdims=True))
        p_f = jnp.exp(s_f - m)
    p_n = jnp.exp(s_n - m)
    l = jnp.sum(p_n, axis=1, keepdims=True)
    o = _dot(p_n.astype(_BF), v_ref[0, fw:W, :])
    if fw:
        l = l + jnp.sum(p_f, axis=1, keepdims=True)
        o = o + _dot(p_f.astype(_BF), v_ref[0, 0:fw, :])
    return o * (1.0 / l)


def _da_kernel(tbl_ref, q_ref, k_ref, v_ref, bn_ref, lam_ref, g_ref, o_ref, *, lambda_init):
    h = pl.program_id(1)
    a = pl.program_id(2)
    T = q_ref.shape[1]
    S = k_ref.shape[1]
    lane = lax.broadcasted_iota(_I32, (T, HEAD_DIM), 1)
    q = q_ref[0] * (DA_DIM ** -0.5)
    zero = jnp.zeros_like(q)
    q1 = jnp.where(lane < DA_DIM, q, zero)
    q2 = jnp.where(lane >= DA_DIM, q, zero)
    far = tbl_ref[N_BUCKETS // 2 - 1, h]
    lam = lam_ref[...]
    lam_full = (jnp.exp(jnp.sum(lam[0:1] * lam[1:2], axis=1, keepdims=True))
                - jnp.exp(jnp.sum(lam[2:3] * lam[3:4], axis=1, keepdims=True)) + lambda_init)
    near_bias = lambda lo: bn_ref[0, :, lo:]

    for br in range(S // T):
        @pl.when(a == br)
        def _(br=br):
            W = (br + 1) * T
            o = (_attend(q1, k_ref, v_ref, W, far, near_bias)
                 - lam_full * _attend(q2, k_ref, v_ref, W, far, near_bias))
            ms = jnp.mean(o * o, axis=-1, keepdims=True)
            o = (o * lax.rsqrt(ms + SUBLN_EPS) * g_ref[...]) * (1.0 - lambda_init)
            o_ref[0] = o.astype(_BF)


def _diff_attn(qkv, rel_table, bn, lam, subln_g, lambda_init):
    B, S, _ = qkv.shape
    H = N_HEADS
    T = ATT_TILE
    kern = functools.partial(_da_kernel, lambda_init=lambda_init)
    return pl.pallas_call(
        kern,
        grid=(B, H, S // T),
        in_specs=[pl.BlockSpec(memory_space=pltpu.SMEM),
                  pl.BlockSpec((1, T, HEAD_DIM), lambda b, h, i: (b, i, h)),
                  pl.BlockSpec((1, S, HEAD_DIM), lambda b, h, i: (b, 0, H + h)),
                  pl.BlockSpec((1, S, HEAD_DIM), lambda b, h, i: (b, 0, 2 * H + h)),
                  pl.BlockSpec((1, T, NEAR_W), lambda b, h, i: (h, 0, 0)),
                  pl.BlockSpec((4, DA_DIM), lambda b, h, i: (0, 0)),
                  pl.BlockSpec((1, HEAD_DIM), lambda b, h, i: (0, 0))],
        out_specs=pl.BlockSpec((1, T, HEAD_DIM), lambda b, h, i: (b, i, h)),
        out_shape=jax.ShapeDtypeStruct((B, S, H * HEAD_DIM), _BF),
        compiler_params=_params(("parallel", "parallel", "arbitrary")),
        name="diff_attn",
    )(rel_table, qkv, qkv, qkv, bn, lam, subln_g)


_INT_MIN = np.int32(-2 ** 31)


_KEY_NEG_INF = np.int32(np.array(-np.inf, np.float32).view(np.int32) ^ np.int32(0x7FFFFFFF))
SEARCH_GROUPS = 4


def _sa_branch(W, tbl_ref, q_ref, ckv_ref, iw_ref, ikk_ref, wuk_ref, wuv_ref, bn_ref,
               iqm_ref, keys_ref, madd_ref, oh_ref, last_ref):
    T = iw_ref.shape[1]
    H = q_ref.shape[1]
    iw = iw_ref[0]

    for cb in range(W // T):
        cols = slice(cb * T, (cb + 1) * T)
        ikt = ikk_ref[0, cols, :]
        score = jnp.zeros((T, T), _F32)
        for hh in range(H):
            score = score + jnp.maximum(_dot_nt(iqm_ref[hh], ikt), 0.0) * iw[:, hh:hh + 1]
        score = jnp.where(score == 0.0, 0.0, score)
        if cb == W // T - 1:
            row = lax.broadcasted_iota(_I32, (T, T), 0)
            col = lax.broadcasted_iota(_I32, (T, T), 1)
            score = jnp.where((col >> CHUNK_SHIFT) <= (row >> CHUNK_SHIFT), score, -jnp.inf)
        bits = lax.bitcast_convert_type(score, _I32)
        keys_ref[:, cols] = bits ^ ((bits >> 31) & np.int32(0x7FFFFFFF))

    def count(pred, rows=slice(None)):
        return jnp.sum(jnp.where(pred(keys_ref[rows, 0:W]), 1.0, 0.0), axis=1, keepdims=True)

    R = T // SEARCH_GROUPS

    def bit_body(it, thrs):
        bit = lax.shift_left(np.int32(1), np.int32(31) - it)
        out = []
        for g in range(SEARCH_GROUPS):
            cand_u = thrs[g] | bit
            cand = cand_u ^ _INT_MIN
            cnt = count(lambda k: k >= cand, slice(g * R, (g + 1) * R))
            out.append(jnp.where(cnt >= float(TOPK), cand_u, thrs[g]))
        return tuple(out)

    thrs = lax.fori_loop(0, 32, bit_body, tuple(jnp.zeros((R, 1), _I32) for _ in range(SEARCH_GROUPS)))
    thr = jnp.concatenate(thrs, axis=0) ^ _INT_MIN

    cnt_gt = count(lambda k: k > thr)
    cnt_ge = count(lambda k: k >= thr)
    need = float(TOPK) - cnt_gt
    last_ref[...] = jnp.full((T, 1), 2 ** 30, _I32)
    tied = jnp.logical_and(cnt_ge - cnt_gt > need, thr > _KEY_NEG_INF)

    @pl.when(jnp.max(jnp.where(tied, 1.0, 0.0)) > 0.0)
    def _():
        col = lax.broadcasted_iota(_I32, (T, W), 1)

        def idx_body(it, v):
            cand = v | lax.shift_left(np.int32(1), np.int32(10) - it)
            below = count(lambda k: jnp.logical_and(k == thr, col < cand))
            return jnp.where(below < need, cand, v)
        last_ref[...] = lax.fori_loop(0, 11, idx_body, jnp.zeros((T, 1), _I32))

    last = last_ref[...]
    k = keys_ref[:, 0:W]
    col = lax.broadcasted_iota(_I32, (T, W), 1)
    sel = jnp.logical_or(k > thr, jnp.logical_and(k == thr, col <= last))
    madd_ref[:, 0:W] = jnp.where(sel, 0.0, NEG)

    def head_body(h, carry):
        qlat = (_dot(q_ref[0, h], wuk_ref[h]) * (HEAD_DIM ** -0.5)).astype(_BF)
        olat = _attend(qlat, ckv_ref, ckv_ref, W, tbl_ref[N_BUCKETS // 2 - 1, h],
                       lambda lo: bn_ref[h, :, lo:], madd_ref)
        oh_ref[h] = _dot(olat.astype(_BF), wuv_ref[h])
        return carry

    lax.fori_loop(0, H, head_body, 0)


def _sa_kernel(tbl_ref, q_ref, ckv_ref, iq_ref, ikk_ref, iw_ref, wuk_ref, wuv_ref, bn_ref,
               o_ref, iqm_ref, keys_ref, madd_ref, oh_ref, last_ref):
    a = pl.program_id(1)
    T = iq_ref.shape[1]
    S = ckv_ref.shape[1]
    H = q_ref.shape[1]
    lane = lax.broadcasted_iota(_I32, (T, 128), 1)
    for p in range(H // 2):
        iqp = iq_ref[0, :, p * 128:(p + 1) * 128]
        zero = jnp.zeros_like(iqp)
        iqm_ref[2 * p] = jnp.where(lane < IDX_DIM, iqp, zero)
        iqm_ref[2 * p + 1] = jnp.where(lane >= IDX_DIM, iqp, zero)

    for br in range(S // T):
        @pl.when(a == br)
        def _(br=br):
            _sa_branch((br + 1) * T, tbl_ref, q_ref, ckv_ref, iw_ref, ikk_ref, wuk_ref, wuv_ref, bn_ref,
                       iqm_ref, keys_ref, madd_ref, oh_ref, last_ref)

    for h in range(H):
        o_ref[0, :, h * HEAD_DIM:(h + 1) * HEAD_DIM] = oh_ref[h].astype(_BF)


def _sparse_attn(q, ckv, iq, ikk, iw, wuk, wuv, rel_table, bn):
    B, H, S, _ = q.shape
    T = ATT_TILE
    return pl.pallas_call(
        _sa_kernel,
        grid=(B, S // T),
        in_specs=[pl.BlockSpec(memory_space=pltpu.SMEM),
                  pl.BlockSpec((1, H, T, HEAD_DIM), lambda b, i: (b, 0, i, 0)),
                  pl.BlockSpec((1, S, SA_LATENT), lambda b, i: (b, 0, 0)),
                  pl.BlockSpec((1, T, 512), lambda b, i: (b, i, 0)),
                  pl.BlockSpec((1, S, 128), lambda b, i: (b, 0, 0)),
                  pl.BlockSpec((1, T, 128), lambda b, i: (b, i, 0)),
                  _full((H, HEAD_DIM, SA_LATENT)),
                  _full((H, SA_LATENT, HEAD_DIM)),
                  _full((H, T, NEAR_W))],
        out_specs=pl.BlockSpec((1, T, H * HEAD_DIM), lambda b, i: (b, i, 0)),
        out_shape=jax.ShapeDtypeStruct((B, S, H * HEAD_DIM), _BF),
        scratch_shapes=[pltpu.VMEM((H, T, 128), _BF),
                        pltpu.VMEM((T, S), _I32),
                        pltpu.VMEM((T, S), _F32),
                        pltpu.VMEM((H, T, HEAD_DIM), _F32),
                        pltpu.VMEM((T, 1), _I32)],
        compiler_params=_params(("parallel", "arbitrary")),
        name="sparse_attn",
    )(rel_table, q, ckv, iq, ikk, iw, wuk, wuv, bn)


def _cv_kernel(a_ref, ah_ref, wdw_ref, bdw_ref, lg_ref, lb_ref, w2_ref, b2_ref, x_ref, g1_ref, o_ref,
               ext_ref, y_ref):
    i = pl.program_id(1)
    tm = a_ref.shape[1]
    ext_ref[CONV_HALO:CONV_HALO + tm] = a_ref[0]
    halo = ah_ref[0]
    ext_ref[0:CONV_HALO] = jnp.where(i > 0, halo, jnp.zeros_like(halo))
    rb = 32
    base = CONV_HALO - (CONV_WIDTH - 1)
    for r in range(tm // rb):
        acc = wdw_ref[0:1, :] * ext_ref[r * rb + base:r * rb + base + rb, :] + bdw_ref[...]
        for k in range(1, CONV_WIDTH):
            acc = acc + wdw_ref[k:k + 1, :] * ext_ref[r * rb + base + k:r * rb + base + k + rb, :]
        y_ref[r * rb:(r + 1) * rb] = _silu(_layernorm(acc, lg_ref[...], lb_ref[...])).astype(_BF)
    y = _dot(y_ref[...], w2_ref[...]) + b2_ref[...]
    o_ref[0] = x_ref[0] + g1_ref[0] * y


def _cv_tail(a, w_dw, b_dw, ln_g, ln_b, w2, b2, x, g1):
    B, S, D = a.shape
    tm = 256
    hb = tm // CONV_HALO
    vec = lambda: _full((1, D))
    return pl.pallas_call(
        _cv_kernel,
        grid=(B, S // tm),
        in_specs=[pl.BlockSpec((1, tm, D), lambda b, i: (b, i, 0)),
                  pl.BlockSpec((1, CONV_HALO, D), lambda b, i: (b, jnp.maximum(i * hb - 1, 0), 0)),
                  _full((CONV_WIDTH, D)), vec(), vec(), vec(), _full((D, D)), vec(),
                  pl.BlockSpec((1, tm, D), lambda b, i: (b, i, 0)),
                  pl.BlockSpec((1, 1, D), lambda b, i: (b, 0, 0))],
        out_specs=pl.BlockSpec((1, tm, D), lambda b, i: (b, i, 0)),
        out_shape=jax.ShapeDtypeStruct((B, S, D), _F32),
        scratch_shapes=[pltpu.VMEM((CONV_HALO + tm, D), _F32), pltpu.VMEM((tm, D), _BF)],
        compiler_params=_params(("parallel", "parallel")),
        name="cv_tail",
    )(a, a, w_dw, b_dw, ln_g, ln_b, w2, b2, x, g1)


def _sg_kernel(u_ref, v_ref, ws_ref, bs_ref, wo_ref, bo_ref, x_ref, g1_ref, o_ref, z_ref):
    tm = u_ref.shape[1]
    G = ws_ref.shape[0]
    C = SG_CHUNK
    gw = u_ref.shape[2] // G
    t = lax.broadcasted_iota(_I32, (C, C), 0)
    s = lax.broadcasted_iota(_I32, (C, C), 1)
    for g in range(G):
        wg = jnp.where(t >= s, ws_ref[g], 0.0).astype(_BF)
        bcol = bs_ref[:, g:g + 1]
        cols = slice(g * gw, (g + 1) * gw)
        for n in range(tm // C):
            rows = slice(n * C, (n + 1) * C)
            sv = _dot(wg, v_ref[0, rows, cols]) + bcol
            z_ref[rows, cols] = (u_ref[0, rows, cols] * sv).astype(_BF)
    y = _dot(z_ref[...], wo_ref[...]) + bo_ref[...]
    o_ref[0] = x_ref[0] + g1_ref[0] * y


def _sg_tail(u, v, w_s, b_s_t, w_out, b_out, x, g1):
    B, S, W = u.shape
    D = x.shape[2]
    G = w_s.shape[0]
    tm = 256
    return pl.pallas_call(
        _sg_kernel,
        grid=(B, S // tm),
        in_specs=[pl.BlockSpec((1, tm, W), lambda b, i: (b, i, 0)),
                  pl.BlockSpec((1, tm, W), lambda b, i: (b, i, 0)),
                  _full((G, SG_CHUNK, SG_CHUNK)), _full((SG_CHUNK, G)),
                  _full((W, D)), _full((1, D)),
                  pl.BlockSpec((1, tm, D), lambda b, i: (b, i, 0)),
                  pl.BlockSpec((1, 1, D), lambda b, i: (b, 0, 0))],
        out_specs=pl.BlockSpec((1, tm, D), lambda b, i: (b, i, 0)),
        out_shape=jax.ShapeDtypeStruct((B, S, D), _F32),
        scratch_shapes=[pltpu.VMEM((tm, W), _BF)],
        compiler_params=_params(("parallel", "parallel")),
        name="sg_tail",
    )(u, v, w_s, b_s_t, w_out, b_out, x, g1)


def _out_kernel(a_ref, w_ref, x_ref, g1_ref, o_ref):
    o_ref[0] = x_ref[0] + g1_ref[0] * _dot(a_ref[0], w_ref[...])


def _out_proj(a, w, x, g1):
    B, S, D = x.shape
    K = a.shape[2]
    tm = 512
    return pl.pallas_call(
        _out_kernel,
        grid=(B, S // tm),
        in_specs=[pl.BlockSpec((1, tm, K), lambda b, i: (b, i, 0)),
                  _full((K, D)),
                  pl.BlockSpec((1, tm, D), lambda b, i: (b, i, 0)),
                  pl.BlockSpec((1, 1, D), lambda b, i: (b, 0, 0))],
        out_specs=pl.BlockSpec((1, tm, D), lambda b, i: (b, i, 0)),
        out_shape=jax.ShapeDtypeStruct((B, S, D), _F32),
        compiler_params=_params(("parallel", "parallel")),
        name="out_proj",
    )(a, w, x, g1)


def _ffn_kernel(x_ref, xh_ref, g_ref, sc_ref, sh_ref, g2_ref, wup_ref, wdw_ref, bdw_ref, wdn_ref,
                fg_ref, o_ref, he_ref, acc_ref, *, final):
    i = pl.program_id(1)
    tm = x_ref.shape[1]
    F = wdn_ref.shape[0]
    x = x_ref[0]
    he_ref[FFN_HALO:FFN_HALO + tm] = _rms_mod(x, g_ref[...], sc_ref[0], sh_ref[0]).astype(_BF)
    hh = _rms_mod(xh_ref[0], g_ref[...], sc_ref[0], sh_ref[0])
    he_ref[0:FFN_HALO] = jnp.where(i > 0, hh, jnp.zeros_like(hh)).astype(_BF)
    tf = 256

    def conv(col0):
        cols = slice(col0, col0 + tf)
        a = _dot(he_ref[...], wup_ref[:, cols])
        w = wdw_ref[:, cols]
        return (a[FFN_HALO - 2:FFN_HALO - 2 + tm] * w[0:1]
                + a[FFN_HALO - 1:FFN_HALO - 1 + tm] * w[1:2]
                + a[FFN_HALO:FFN_HALO + tm] * w[2:3] + bdw_ref[:, cols])

    for c in range(F // tf):
        gate = conv(c * tf)
        val = conv(F + c * tf)
        act = (_silu(gate) * val).astype(_BF)
        part = _dot(act, wdn_ref[c * tf:(c + 1) * tf, :])
        if c == 0:
            acc_ref[...] = part
        else:
            acc_ref[...] += part
    xn = x + g2_ref[0] * acc_ref[...]
    if final:
        ms = jnp.mean(xn * xn, axis=-1, keepdims=True)
        xn = xn * lax.rsqrt(ms + EPS) * fg_ref[...]
    o_ref[0] = xn


def _ffn(x, g, sc, sh, g2, w_up, w_dw, b_dw, w_down, final_g, final):
    B, S, D = x.shape
    F = w_down.shape[0]
    tm = 512
    hb = tm // FFN_HALO
    kern = functools.partial(_ffn_kernel, final=final)
    mod = lambda: pl.BlockSpec((1, 1, D), lambda b, i: (b, 0, 0))
    return pl.pallas_call(
        kern,
        grid=(B, S // tm),
        in_specs=[pl.BlockSpec((1, tm, D), lambda b, i: (b, i, 0)),
                  pl.BlockSpec((1, FFN_HALO, D), lambda b, i: (b, jnp.maximum(i * hb - 1, 0), 0)),
                  _full((1, D)), mod(), mod(), mod(),
                  _full((D, 2 * F)), _full((3, 2 * F)), _full((1, 2 * F)), _full((F, D)),
                  _full((1, D))],
        out_specs=pl.BlockSpec((1, tm, D), lambda b, i: (b, i, 0)),
        out_shape=jax.ShapeDtypeStruct((B, S, D), _F32),
        scratch_shapes=[pltpu.VMEM((FFN_HALO + tm, D), _BF), pltpu.VMEM((tm, D), _F32)],
        compiler_params=_params(("parallel", "parallel")),
        name="ffn",
    )(x, x, g, sc, sh, g2, w_up, w_dw, b_dw, w_down, final_g)


def kernel(x, c, rel_table, ada_w, ada_b, norm_g, final_g, da_w_in, da_lam, da_subln_g, da_w_out, cv_w_pw1, cv_b_pw1, cv_w_dw, cv_b_dw, cv_ln_g, cv_ln_b, cv_w_pw2, cv_b_pw2, sa_w_in, sa_kv_g, sa_w_uk, sa_w_uv, sa_w_out, sg_w_in, sg_b_in, sg_ln_g, sg_ln_b, sg_w_s, sg_b_s, sg_w_out, sg_b_out, ff_w_up, ff_w_dw, ff_b_dw, ff_w_down):
    B, S, D = x.shape
    depth = ada_w.shape[0]
    n_mixers = 4
    H = N_HEADS
    mods = _ada(c, ada_w, ada_b)
    bn = _bias_tiles(rel_table)
    row = lambda v: v.reshape(1, -1)

    for layer in range(depth):
        kind = layer % n_mixers
        j = layer // n_mixers
        sh1, sc1, g1, sh2, sc2, g2 = [m.reshape(B, 1, D) for m in jnp.split(mods[layer], 6, axis=-1)]
        ng1 = row(norm_g[layer, 0])
        if kind == 0:
            lambda_init = 0.8 - 0.6 * math.exp(-0.3 * layer)
            qkv = _in_da(x, ng1, sc1, sh1, da_w_in[j].astype(_BF))
            o = _diff_attn(qkv, rel_table, bn, da_lam[j], row(da_subln_g[j]), lambda_init)
            x = _out_proj(o, da_w_out[j].astype(_BF), x, g1)
        elif kind == 1:
            a = _in_cv(x, ng1, sc1, sh1, cv_w_pw1[j].astype(_BF), row(cv_b_pw1[j]))
            x = _cv_tail(a, cv_w_dw[j], row(cv_b_dw[j]), row(cv_ln_g[j]), row(cv_ln_b[j]),
                         cv_w_pw2[j].astype(_BF), row(cv_b_pw2[j]), x, g1)
        elif kind == 2:
            w = sa_w_in[j]
            o1 = H * HEAD_DIM
            o2 = o1 + SA_LATENT
            o3 = o2 + H * IDX_DIM
            o4 = o3 + IDX_DIM
            w_ik = w[:, o3:o4]
            w_iw = jnp.pad(w[:, o4:], ((0, 0), (0, 128 - (w.shape[1] - o4))))
            w_cat = jnp.concatenate([w[:, :o3], w_ik, w_ik, w_iw], axis=1).astype(_BF)
            q, ckv, iq, ikk, iw = _in_sa(x, ng1, sc1, sh1, w_cat, row(sa_kv_g[j]))
            wuk = jnp.transpose(sa_w_uk[j], (1, 2, 0)).astype(_BF)
            wuv = jnp.transpose(sa_w_uv[j], (1, 0, 2)).astype(_BF)
            o = _sparse_attn(q, ckv, iq, ikk, iw, wuk, wuv, rel_table, bn)
            x = _out_proj(o, sa_w_out[j].astype(_BF), x, g1)
        else:
            u, v = _in_sg(x, ng1, sc1, sh1, sg_w_in[j].astype(_BF), row(sg_b_in[j]),
                          row(sg_ln_g[j]), row(sg_ln_b[j]))
            x = _sg_tail(u, v, sg_w_s[j], sg_b_s[j].T, sg_w_out[j].astype(_BF), row(sg_b_out[j]), x, g1)
        x = _ffn(x, row(norm_g[layer, 1]), sc2, sh2, g2, ff_w_up[layer].astype(_BF), ff_w_dw[layer],
                 row(ff_b_dw[layer]), ff_w_down[layer].astype(_BF), row(final_g), layer == depth - 1)
    return x
```

```python
import functools
import math

import jax
import jax.numpy as jnp
import numpy as np
from jax import lax
from jax.experimental import pallas as pl
from jax.experimental.pallas import tpu as pltpu

_BF = jnp.bfloat16
_F32 = jnp.float32
_I32 = jnp.int32

EPS = 1e-6
SUBLN_EPS = 1e-5
NEG = -1e30
CHUNK = 64
CHUNK_SHIFT = 6
N_HEADS = 8
HEAD_DIM = 128
DA_DIM = 64
SA_LATENT = 256
IDX_DIM = 64
TOPK = 256
N_BUCKETS = 32
MAX_DISTANCE = 128
SG_CHUNK = 128
SG_GROUPS = 8
CONV_WIDTH = 31
CONV_HALO = 32
CONV_ROWS = 32
SUBLANES = 8
FFN_HALO = 8
FFN_CHUNK = 256
FFN_DOWN_GROUP = 4
ATT_TILE = 512
NEAR_W = 768
VMEM_LIMIT = 56 * 1024 * 1024


def _dot(a, b):
    return jnp.dot(a, b, preferred_element_type=_F32)


def _dot_nt(a, b):
    return lax.dot_general(a, b, (((1,), (1,)), ((), ())), preferred_element_type=_F32)


def _rms_mod(x, g, sc, sh):
    ms = jnp.mean(x * x, axis=-1, keepdims=True)
    return (x * lax.rsqrt(ms + EPS) * g) * (1.0 + sc) + sh


def _layernorm(x, g, b):
    mu = jnp.mean(x, axis=-1, keepdims=True)
    xc = x - mu
    var = jnp.mean(xc * xc, axis=-1, keepdims=True)
    return xc * lax.rsqrt(var + EPS) * g + b


def _silu(x):
    return x * jax.nn.sigmoid(x)


def _params(sem):
    return pltpu.CompilerParams(dimension_semantics=sem, vmem_limit_bytes=VMEM_LIMIT)


def _full(shape):
    n = len(shape)
    return pl.BlockSpec(shape, lambda *_: (0,) * n, pipeline_mode=pl.Buffered(1))


def _ada_kernel(c_ref, w_ref, b_ref, o_ref):
    ca = _silu(c_ref[...]).astype(_BF)
    o_ref[0] = _dot(ca, w_ref[0].astype(_BF)) + b_ref[0]


def _ada(c, ada_w, ada_b):
    L, D, N = ada_w.shape
    B = c.shape[0]
    tn = N // 4
    return pl.pallas_call(
        _ada_kernel,
        grid=(L, N // tn),
        in_specs=[pl.BlockSpec((B, D), lambda l, j: (0, 0)),
                  pl.BlockSpec((1, D, tn), lambda l, j: (l, 0, j)),
                  pl.BlockSpec((1, 1, tn), lambda l, j: (l, 0, j))],
        out_specs=pl.BlockSpec((1, B, tn), lambda l, j: (l, 0, j)),
        out_shape=jax.ShapeDtypeStruct((L, B, N), _F32),
        compiler_params=_params(("arbitrary", "arbitrary")),
        name="ada",
    )(c, ada_w, ada_b.reshape(L, 1, N))


def _bias_kernel(tbl_ref, bn_ref):
    h = pl.program_id(0)
    shape = bn_ref.shape[1:]
    i = lax.broadcasted_iota(_I32, shape, 0)
    j = lax.broadcasted_iota(_I32, shape, 1) - (NEAR_W - ATT_TILE)
    nb = N_BUCKETS // 2
    max_exact = nb // 2
    rel = j - i
    ret = jnp.where(rel > 0, nb, 0)
    n = jnp.abs(rel)
    nf = jnp.maximum(n, 1).astype(_F32)
    large = max_exact + (jnp.log(nf / max_exact) / math.log(MAX_DISTANCE / max_exact)
                         * (nb - max_exact)).astype(_I32)
    large = jnp.minimum(large, nb - 1)
    bucket = ret + jnp.where(n < max_exact, n, large)
    out = jnp.zeros(shape, _F32)
    for bk in range(N_BUCKETS):
        out = jnp.where(bucket == bk, tbl_ref[bk, h], out)
    visible = (j >> CHUNK_SHIFT) <= (i >> CHUNK_SHIFT)
    bn_ref[0] = jnp.where(visible, out, NEG)


def _bias_tiles(rel_table):
    H = rel_table.shape[1]
    return pl.pallas_call(
        _bias_kernel,
        grid=(H,),
        in_specs=[pl.BlockSpec(memory_space=pltpu.SMEM)],
        out_specs=pl.BlockSpec((1, ATT_TILE, NEAR_W), lambda h: (h, 0, 0)),
        out_shape=jax.ShapeDtypeStruct((H, ATT_TILE, NEAR_W), _F32),
        compiler_params=_params(("arbitrary",)),
        name="bias_tiles",
    )(rel_table)


def _in_specs_common(tm, D):
    return [pl.BlockSpec((1, tm, D), lambda b, i: (b, i, 0)),
            pl.BlockSpec((1, D), lambda b, i: (0, 0)),
            pl.BlockSpec((1, 1, D), lambda b, i: (b, 0, 0)),
            pl.BlockSpec((1, 1, D), lambda b, i: (b, 0, 0))]


def _in_da_kernel(x_ref, g_ref, sc_ref, sh_ref, w_ref, o_ref):
    hn = _rms_mod(x_ref[0], g_ref[...], sc_ref[0], sh_ref[0]).astype(_BF)
    N = w_ref.shape[1]
    tn = 512
    for c in range(N // tn):
        o_ref[0, :, c * tn:(c + 1) * tn] = _dot(hn, w_ref[:, c * tn:(c + 1) * tn]).astype(_BF)


def _in_da(x, g, sc, sh, w):
    B, S, D = x.shape
    N = w.shape[1]
    tm = 512
    return pl.pallas_call(
        _in_da_kernel,
        grid=(B, S // tm),
        in_specs=_in_specs_common(tm, D) + [_full((D, N))],
        out_specs=pl.BlockSpec((1, tm, N), lambda b, i: (b, i, 0)),
        out_shape=jax.ShapeDtypeStruct((B, S, N), _BF),
        compiler_params=_params(("parallel", "parallel")),
        name="in_da",
    )(x, g, sc, sh, w)


def _in_cv_kernel(x_ref, g_ref, sc_ref, sh_ref, w_ref, b_ref, o_ref):
    hn = _rms_mod(x_ref[0], g_ref[...], sc_ref[0], sh_ref[0]).astype(_BF)
    D = o_ref.shape[2]
    tn = 512
    n = D // tn
    lo = lambda c: slice(c * tn, (c + 1) * tn)
    hi = lambda c: slice(D + c * tn, D + (c + 1) * tn)
    dots = lambda c: (_dot(hn, w_ref[:, lo(c)]), _dot(hn, w_ref[:, hi(c)]))
    nxt = dots(0)
    for c in range(n):
        cur = nxt
        if c + 1 < n:
            nxt = dots(c + 1)
        o_ref[0, :, lo(c)] = (cur[0] + b_ref[:, lo(c)]) * jax.nn.sigmoid(cur[1] + b_ref[:, hi(c)])


def _in_cv(x, g, sc, sh, w, b):
    B, S, D = x.shape
    N = w.shape[1]
    tm = 512
    return pl.pallas_call(
        _in_cv_kernel,
        grid=(B, S // tm),
        in_specs=_in_specs_common(tm, D) + [_full((D, N)), _full((1, N))],
        out_specs=pl.BlockSpec((1, tm, D), lambda b, i: (b, i, 0)),
        out_shape=jax.ShapeDtypeStruct((B, S, D), _F32),
        compiler_params=_params(("parallel", "parallel")),
        name="in_cv",
    )(x, g, sc, sh, w, b)


def _in_sa_kernel(x_ref, g_ref, sc_ref, sh_ref, w_ref, kvg_ref,
                  q_ref, ckv_ref, iq_ref, ikk_ref, iw_ref):
    hn = _rms_mod(x_ref[0], g_ref[...], sc_ref[0], sh_ref[0]).astype(_BF)
    H = q_ref.shape[1]
    for c in range(H // 2):
        r = _dot(hn, w_ref[:, c * 256:(c + 1) * 256])
        q_ref[0, 2 * c] = r[:, :HEAD_DIM].astype(_BF)
        q_ref[0, 2 * c + 1] = r[:, HEAD_DIM:].astype(_BF)
    o = H * HEAD_DIM
    ckv = _dot(hn, w_ref[:, o:o + SA_LATENT])
    ms = jnp.mean(ckv * ckv, axis=-1, keepdims=True)
    ckv_ref[0] = (ckv * lax.rsqrt(ms + EPS) * kvg_ref[...]).astype(_BF)
    o += SA_LATENT
    iq_ref[0] = (_dot(hn, w_ref[:, o:o + 512]) * (IDX_DIM ** -0.5)).astype(_BF)
    o += 512
    r = _dot(hn, w_ref[:, o:o + 256])
    ikk_ref[0] = r[:, :128].astype(_BF)
    iw_ref[0] = r[:, 128:] * (N_HEADS ** -0.5)


def _in_sa(x, g, sc, sh, w, kv_g):
    B, S, D = x.shape
    N = w.shape[1]
    H = N_HEADS
    tm = 512
    row = lambda n: pl.BlockSpec((1, tm, n), lambda b, i: (b, i, 0))
    return pl.pallas_call(
        _in_sa_kernel,
        grid=(B, S // tm),
        in_specs=_in_specs_common(tm, D) + [_full((D, N)), _full((1, SA_LATENT))],
        out_specs=[pl.BlockSpec((1, H, tm, HEAD_DIM), lambda b, i: (b, 0, i, 0)),
                   row(SA_LATENT), row(512), row(128), row(128)],
        out_shape=[jax.ShapeDtypeStruct((B, H, S, HEAD_DIM), _BF),
                   jax.ShapeDtypeStruct((B, S, SA_LATENT), _BF),
                   jax.ShapeDtypeStruct((B, S, 512), _BF),
                   jax.ShapeDtypeStruct((B, S, 128), _BF),
                   jax.ShapeDtypeStruct((B, S, 128), _F32)],
        compiler_params=_params(("parallel", "parallel")),
        name="in_sa",
    )(x, g, sc, sh, w, kv_g)


def _in_sg_kernel(x_ref, g_ref, sc_ref, sh_ref, w_ref, b_ref, lg_ref, lb_ref, u_ref, v_ref, a_scr):
    hn = _rms_mod(x_ref[0], g_ref[...], sc_ref[0], sh_ref[0]).astype(_BF)
    W = u_ref.shape[2]
    tn = 512
    n = W // tn
    lo = lambda c: slice(c * tn, (c + 1) * tn)
    hi = lambda c: slice(W + c * tn, W + (c + 1) * tn)
    cols = [hi(c) for c in range(n)] + [lo(c) for c in range(n)]
    nxt = _dot(hn, w_ref[:, cols[0]])
    for t in range(2 * n):
        cur = nxt
        if t + 1 < 2 * n:
            nxt = _dot(hn, w_ref[:, cols[t + 1]])
        act = jax.nn.gelu(cur + b_ref[:, cols[t]], approximate=True)
        if t < n:
            a_scr[:, lo(t)] = act
            if t == n - 1:
                v_ref[0] = _layernorm(a_scr[...], lg_ref[...], lb_ref[...]).astype(_BF)
        else:
            u_ref[0, :, lo(t - n)] = act


def _in_sg(x, g, sc, sh, w, b, ln_g, ln_b):
    B, S, D = x.shape
    N = w.shape[1]
    W = N // 2
    tm = 512
    row = lambda: pl.BlockSpec((1, tm, W), lambda b, i: (b, i, 0))
    return pl.pallas_call(
        _in_sg_kernel,
        grid=(B, S // tm),
        in_specs=_in_specs_common(tm, D) + [_full((D, N)), _full((1, N)), _full((1, W)), _full((1, W))],
        out_specs=[row(), row()],
        out_shape=[jax.ShapeDtypeStruct((B, S, W), _F32), jax.ShapeDtypeStruct((B, S, W), _BF)],
        scratch_shapes=[pltpu.VMEM((tm, W), _F32)],
        compiler_params=_params(("parallel", "parallel")),
        name="in_sg",
    )(x, g, sc, sh, w, b, ln_g, ln_b)


def _attend(qs, k_ref, v_ref, W, fars, near_biases, madd_ref=None):
    nw = min(W, NEAR_W)
    fw = W - nw
    logits = []
    for q, far, near_bias in zip(qs, fars, near_biases):
        s_n = _dot_nt(q, k_ref[0, fw:W, :]) + near_bias(NEAR_W - nw)
        s_f = None
        if fw:
            s_f = _dot_nt(q, k_ref[0, 0:fw, :]) + far
        if madd_ref is not None:
            s_n = s_n + madd_ref[:, fw:W]
            if fw:
                s_f = s_f + madd_ref[:, 0:fw]
        logits.append((s_n, s_f))
    outs = []
    for s_n, s_f in logits:
        m = jnp.max(s_n, axis=1, keepdims=True)
        if fw:
            m = jnp.maximum(m, jnp.max(s_f, axis=1, keepdims=True))
            p_f = jnp.exp(s_f - m)
        p_n = jnp.exp(s_n - m)
        l = jnp.sum(p_n, axis=1, keepdims=True)
        o = _dot(p_n.astype(_BF), v_ref[0, fw:W, :])
        if fw:
            l = l + jnp.sum(p_f, axis=1, keepdims=True)
            o = o + _dot(p_f.astype(_BF), v_ref[0, 0:fw, :])
        outs.append(o * (1.0 / l))
    return outs


def _da_kernel(tbl_ref, q_ref, k_ref, v_ref, bn_ref, lam_ref, g_ref, o_ref, *, lambda_init):
    h = pl.program_id(1)
    a = pl.program_id(2)
    T = q_ref.shape[1]
    S = k_ref.shape[1]
    lane = lax.broadcasted_iota(_I32, (T, HEAD_DIM), 1)
    q = q_ref[0] * (DA_DIM ** -0.5)
    zero = jnp.zeros_like(q)
    q1 = jnp.where(lane < DA_DIM, q, zero)
    q2 = jnp.where(lane >= DA_DIM, q, zero)
    far = tbl_ref[N_BUCKETS // 2 - 1, h]
    lam = lam_ref[...]
    lam_full = (jnp.exp(jnp.sum(lam[0:1] * lam[1:2], axis=1, keepdims=True))
                - jnp.exp(jnp.sum(lam[2:3] * lam[3:4], axis=1, keepdims=True)) + lambda_init)
    near_bias = lambda lo: bn_ref[0, :, lo:]

    for br in range(S // T):
        @pl.when(a == br)
        def _(br=br):
            W = (br + 1) * T
            o1, o2 = _attend([q1, q2], k_ref, v_ref, W, [far, far], [near_bias, near_bias])
            o = o1 - lam_full * o2
            ms = jnp.mean(o * o, axis=-1, keepdims=True)
            o = (o * lax.rsqrt(ms + SUBLN_EPS) * g_ref[...]) * (1.0 - lambda_init)
            o_ref[0] = o.astype(_BF)


def _diff_attn(qkv, rel_table, bn, lam, subln_g, lambda_init):
    B, S, _ = qkv.shape
    H = N_HEADS
    T = ATT_TILE
    kern = functools.partial(_da_kernel, lambda_init=lambda_init)
    return pl.pallas_call(
        kern,
        grid=(B, H, S // T),
        in_specs=[pl.BlockSpec(memory_space=pltpu.SMEM),
                  pl.BlockSpec((1, T, HEAD_DIM), lambda b, h, i: (b, i, h)),
                  pl.BlockSpec((1, S, HEAD_DIM), lambda b, h, i: (b, 0, H + h)),
                  pl.BlockSpec((1, S, HEAD_DIM), lambda b, h, i: (b, 0, 2 * H + h)),
                  pl.BlockSpec((1, T, NEAR_W), lambda b, h, i: (h, 0, 0)),
                  pl.BlockSpec((4, DA_DIM), lambda b, h, i: (0, 0)),
                  pl.BlockSpec((1, HEAD_DIM), lambda b, h, i: (0, 0))],
        out_specs=pl.BlockSpec((1, T, HEAD_DIM), lambda b, h, i: (b, i, h)),
        out_shape=jax.ShapeDtypeStruct((B, S, H * HEAD_DIM), _BF),
        compiler_params=_params(("parallel", "parallel", "arbitrary")),
        name="diff_attn",
    )(rel_table, qkv, qkv, qkv, bn, lam, subln_g)


_INT_MIN = np.int32(-2 ** 31)


_KEY_NEG_INF = np.int32(np.array(-np.inf, np.float32).view(np.int32) ^ np.int32(0x7FFFFFFF))
SEARCH_GROUPS = 4
HEADS_PER_STEP = 2


def _sa_branch(W, tbl_ref, q_ref, ckv_ref, iw_ref, ikk_ref, wuk_ref, wuv_ref, bn_ref,
               iqm_ref, keys_ref, madd_ref, oh_ref, last_ref):
    T = iw_ref.shape[1]
    H = q_ref.shape[1]
    iw = iw_ref[0]

    for cb in range(W // T):
        cols = slice(cb * T, (cb + 1) * T)
        ikt = ikk_ref[0, cols, :]
        score = jnp.zeros((T, T), _F32)
        for hh in range(H):
            score = score + jnp.maximum(_dot_nt(iqm_ref[hh], ikt), 0.0) * iw[:, hh:hh + 1]
        score = jnp.where(score == 0.0, 0.0, score)
        if cb == W // T - 1:
            row = lax.broadcasted_iota(_I32, (T, T), 0)
            col = lax.broadcasted_iota(_I32, (T, T), 1)
            score = jnp.where((col >> CHUNK_SHIFT) <= (row >> CHUNK_SHIFT), score, -jnp.inf)
        bits = lax.bitcast_convert_type(score, _I32)
        keys_ref[:, cols] = bits ^ ((bits >> 31) & np.int32(0x7FFFFFFF))

    def count(pred, rows=slice(None)):
        return jnp.sum(jnp.where(pred(keys_ref[rows, 0:W]), 1.0, 0.0), axis=1, keepdims=True)

    R = T // SEARCH_GROUPS

    def bit_body(it, thrs):
        bit = lax.shift_left(np.int32(1), np.int32(31) - it)
        out = []
        for g in range(SEARCH_GROUPS):
            cand_u = thrs[g] | bit
            cand = cand_u ^ _INT_MIN
            cnt = count(lambda k: k >= cand, slice(g * R, (g + 1) * R))
            out.append(jnp.where(cnt >= float(TOPK), cand_u, thrs[g]))
        return tuple(out)

    thrs = lax.fori_loop(0, 32, bit_body, tuple(jnp.zeros((R, 1), _I32) for _ in range(SEARCH_GROUPS)))
    thr = jnp.concatenate(thrs, axis=0) ^ _INT_MIN

    cnt_gt = count(lambda k: k > thr)
    cnt_ge = count(lambda k: k >= thr)
    need = float(TOPK) - cnt_gt
    last_ref[...] = jnp.full((T, 1), 2 ** 30, _I32)
    tied = jnp.logical_and(cnt_ge - cnt_gt > need, thr > _KEY_NEG_INF)

    @pl.when(jnp.max(jnp.where(tied, 1.0, 0.0)) > 0.0)
    def _():
        col = lax.broadcasted_iota(_I32, (T, W), 1)

        def idx_body(it, v):
            cand = v | lax.shift_left(np.int32(1), np.int32(10) - it)
            below = count(lambda k: jnp.logical_and(k == thr, col < cand))
            return jnp.where(below < need, cand, v)
        last_ref[...] = lax.fori_loop(0, 11, idx_body, jnp.zeros((T, 1), _I32))

    last = last_ref[...]
    k = keys_ref[:, 0:W]
    col = lax.broadcasted_iota(_I32, (T, W), 1)
    sel = jnp.logical_or(k > thr, jnp.logical_and(k == thr, col <= last))
    madd_ref[:, 0:W] = jnp.where(sel, 0.0, NEG)

    def head_body(hp, carry):
        hs = [hp * HEADS_PER_STEP + u for u in range(HEADS_PER_STEP)]
        qlats = [(_dot(q_ref[0, h], wuk_ref[h]) * (HEAD_DIM ** -0.5)).astype(_BF) for h in hs]
        olats = _attend(qlats, ckv_ref, ckv_ref, W,
                        [tbl_ref[N_BUCKETS // 2 - 1, h] for h in hs],
                        [functools.partial(lambda h, lo: bn_ref[h, :, lo:], h) for h in hs], madd_ref)
        for h, olat in zip(hs, olats):
            oh_ref[h] = _dot(olat.astype(_BF), wuv_ref[h])
        return carry

    lax.fori_loop(0, H // HEADS_PER_STEP, head_body, 0)


def _sa_kernel(tbl_ref, q_ref, ckv_ref, iq_ref, ikk_ref, iw_ref, wuk_ref, wuv_ref, bn_ref,
               o_ref, iqm_ref, keys_ref, madd_ref, oh_ref, last_ref):
    a = pl.program_id(1)
    T = iq_ref.shape[1]
    S = ckv_ref.shape[1]
    H = q_ref.shape[1]
    lane = lax.broadcasted_iota(_I32, (T, 128), 1)
    for p in range(H // 2):
        iqp = iq_ref[0, :, p * 128:(p + 1) * 128]
        zero = jnp.zeros_like(iqp)
        iqm_ref[2 * p] = jnp.where(lane < IDX_DIM, iqp, zero)
        iqm_ref[2 * p + 1] = jnp.where(lane >= IDX_DIM, iqp, zero)

    for br in range(S // T):
        @pl.when(a == br)
        def _(br=br):
            _sa_branch((br + 1) * T, tbl_ref, q_ref, ckv_ref, iw_ref, ikk_ref, wuk_ref, wuv_ref, bn_ref,
                       iqm_ref, keys_ref, madd_ref, oh_ref, last_ref)

    for h in range(H):
        o_ref[0, :, h * HEAD_DIM:(h + 1) * HEAD_DIM] = oh_ref[h].astype(_BF)


def _sparse_attn(q, ckv, iq, ikk, iw, wuk, wuv, rel_table, bn):
    B, H, S, _ = q.shape
    T = ATT_TILE
    return pl.pallas_call(
        _sa_kernel,
        grid=(B, S // T),
        in_specs=[pl.BlockSpec(memory_space=pltpu.SMEM),
                  pl.BlockSpec((1, H, T, HEAD_DIM), lambda b, i: (b, 0, i, 0)),
                  pl.BlockSpec((1, S, SA_LATENT), lambda b, i: (b, 0, 0)),
                  pl.BlockSpec((1, T, 512), lambda b, i: (b, i, 0)),
                  pl.BlockSpec((1, S, 128), lambda b, i: (b, 0, 0)),
                  pl.BlockSpec((1, T, 128), lambda b, i: (b, i, 0)),
                  _full((H, HEAD_DIM, SA_LATENT)),
                  _full((H, SA_LATENT, HEAD_DIM)),
                  _full((H, T, NEAR_W))],
        out_specs=pl.BlockSpec((1, T, H * HEAD_DIM), lambda b, i: (b, i, 0)),
        out_shape=jax.ShapeDtypeStruct((B, S, H * HEAD_DIM), _BF),
        scratch_shapes=[pltpu.VMEM((H, T, 128), _BF),
                        pltpu.VMEM((T, S), _I32),
                        pltpu.VMEM((T, S), _F32),
                        pltpu.VMEM((H, T, HEAD_DIM), _F32),
                        pltpu.VMEM((T, 1), _I32)],
        compiler_params=_params(("parallel", "arbitrary")),
        name="sparse_attn",
    )(rel_table, q, ckv, iq, ikk, iw, wuk, wuv, bn)


def _cv_kernel(a_ref, ah_ref, wdw_ref, bdw_ref, lg_ref, lb_ref, w2_ref, b2_ref, x_ref, g1_ref, o_ref,
               ext_ref, y_ref):
    i = pl.program_id(1)
    tm = a_ref.shape[1]
    n_ext = CONV_HALO + tm
    ext_ref[0, CONV_HALO:n_ext] = a_ref[0]
    halo = ah_ref[0]
    ext_ref[0, 0:CONV_HALO] = jnp.where(i > 0, halo, jnp.zeros_like(halo))
    for r in range(1, SUBLANES):
        ext_ref[r, 0:n_ext - SUBLANES] = ext_ref[0, r:r + n_ext - SUBLANES]
    rb = CONV_ROWS
    base = CONV_HALO - (CONV_WIDTH - 1)

    def block(blk, carry):
        row0 = pl.multiple_of(blk * rb, rb)
        acc = jnp.broadcast_to(bdw_ref[...], (rb, bdw_ref.shape[1]))
        for k in range(CONV_WIDTH):
            r = (base + k) % SUBLANES
            al = base + k - r
            acc = acc + wdw_ref[k:k + 1, :] * ext_ref[r, pl.ds(row0 + al, rb), :]
        y_ref[pl.ds(row0, rb)] = _silu(_layernorm(acc, lg_ref[...], lb_ref[...])).astype(_BF)
        return carry

    lax.fori_loop(0, tm // rb, block, 0)
    y = _dot(y_ref[...], w2_ref[...]) + b2_ref[...]
    o_ref[0] = x_ref[0] + g1_ref[0] * y


def _cv_tail(a, w_dw, b_dw, ln_g, ln_b, w2, b2, x, g1):
    B, S, D = a.shape
    tm = 512
    hb = tm // CONV_HALO
    vec = lambda: _full((1, D))
    return pl.pallas_call(
        _cv_kernel,
        grid=(B, S // tm),
        in_specs=[pl.BlockSpec((1, tm, D), lambda b, i: (b, i, 0)),
                  pl.BlockSpec((1, CONV_HALO, D), lambda b, i: (b, jnp.maximum(i * hb - 1, 0), 0)),
                  _full((CONV_WIDTH, D)), vec(), vec(), vec(), _full((D, D)), vec(),
                  pl.BlockSpec((1, tm, D), lambda b, i: (b, i, 0)),
                  pl.BlockSpec((1, 1, D), lambda b, i: (b, 0, 0))],
        out_specs=pl.BlockSpec((1, tm, D), lambda b, i: (b, i, 0)),
        out_shape=jax.ShapeDtypeStruct((B, S, D), _F32),
        scratch_shapes=[pltpu.VMEM((SUBLANES, CONV_HALO + tm, D), _F32), pltpu.VMEM((tm, D), _BF)],
        compiler_params=_params(("parallel", "parallel")),
        name="cv_tail",
    )(a, a, w_dw, b_dw, ln_g, ln_b, w2, b2, x, g1)


def _sg_kernel(u_ref, v_ref, ws_ref, bs_ref, wo_ref, bo_ref, x_ref, g1_ref, o_ref, z_ref):
    tm = u_ref.shape[1]
    G = ws_ref.shape[0]
    C = SG_CHUNK
    gw = u_ref.shape[2] // G
    t = lax.broadcasted_iota(_I32, (C, C), 0)
    s = lax.broadcasted_iota(_I32, (C, C), 1)
    for g in range(G):
        wg = jnp.where(t >= s, ws_ref[g], 0.0).astype(_BF)
        bcol = bs_ref[:, g:g + 1]
        cols = slice(g * gw, (g + 1) * gw)
        for n in range(tm // C):
            rows = slice(n * C, (n + 1) * C)
            sv = _dot(wg, v_ref[0, rows, cols]) + bcol
            z_ref[rows, cols] = (u_ref[0, rows, cols] * sv).astype(_BF)
    y = _dot(z_ref[...], wo_ref[...]) + bo_ref[...]
    o_ref[0] = x_ref[0] + g1_ref[0] * y


def _sg_tail(u, v, w_s, b_s_t, w_out, b_out, x, g1):
    B, S, W = u.shape
    D = x.shape[2]
    G = w_s.shape[0]
    tm = 256
    return pl.pallas_call(
        _sg_kernel,
        grid=(B, S // tm),
        in_specs=[pl.BlockSpec((1, tm, W), lambda b, i: (b, i, 0)),
                  pl.BlockSpec((1, tm, W), lambda b, i: (b, i, 0)),
                  _full((G, SG_CHUNK, SG_CHUNK)), _full((SG_CHUNK, G)),
                  _full((W, D)), _full((1, D)),
                  pl.BlockSpec((1, tm, D), lambda b, i: (b, i, 0)),
                  pl.BlockSpec((1, 1, D), lambda b, i: (b, 0, 0))],
        out_specs=pl.BlockSpec((1, tm, D), lambda b, i: (b, i, 0)),
        out_shape=jax.ShapeDtypeStruct((B, S, D), _F32),
        scratch_shapes=[pltpu.VMEM((tm, W), _BF)],
        compiler_params=_params(("parallel", "parallel")),
        name="sg_tail",
    )(u, v, w_s, b_s_t, w_out, b_out, x, g1)


def _out_kernel(a_ref, w_ref, x_ref, g1_ref, o_ref):
    o_ref[0] = x_ref[0] + g1_ref[0] * _dot(a_ref[0], w_ref[...])


def _out_proj(a, w, x, g1):
    B, S, D = x.shape
    K = a.shape[2]
    tm = 512
    return pl.pallas_call(
        _out_kernel,
        grid=(B, S // tm),
        in_specs=[pl.BlockSpec((1, tm, K), lambda b, i: (b, i, 0)),
                  _full((K, D)),
                  pl.BlockSpec((1, tm, D), lambda b, i: (b, i, 0)),
                  pl.BlockSpec((1, 1, D), lambda b, i: (b, 0, 0))],
        out_specs=pl.BlockSpec((1, tm, D), lambda b, i: (b, i, 0)),
        out_shape=jax.ShapeDtypeStruct((B, S, D), _F32),
        compiler_params=_params(("parallel", "parallel")),
        name="out_proj",
    )(a, w, x, g1)


def _ffn_kernel(x_ref, xh_ref, g_ref, sc_ref, sh_ref, g2_ref, wup_ref, wdw_ref, bdw_ref, wdn_ref,
                fg_ref, o_ref, he_ref, act_ref, acc_ref, *, final):
    i = pl.program_id(1)
    tm = x_ref.shape[1]
    F = wdn_ref.shape[0]
    x = x_ref[0]
    he_ref[FFN_HALO:FFN_HALO + tm] = _rms_mod(x, g_ref[...], sc_ref[0], sh_ref[0]).astype(_BF)
    hh = _rms_mod(xh_ref[0], g_ref[...], sc_ref[0], sh_ref[0])
    he_ref[0:FFN_HALO] = jnp.where(i > 0, hh, jnp.zeros_like(hh)).astype(_BF)
    tf = FFN_CHUNK
    n_chunks = F // tf

    def up(c):
        he = he_ref[...]
        return (_dot(he, wup_ref[:, c * tf:(c + 1) * tf]), _dot(he, wup_ref[:, F + c * tf:F + (c + 1) * tf]))

    def conv(a, col0):
        cols = slice(col0, col0 + tf)
        w = wdw_ref[:, cols]
        return (a[FFN_HALO - 2:FFN_HALO - 2 + tm] * w[0:1]
                + a[FFN_HALO - 1:FFN_HALO - 1 + tm] * w[1:2]
                + a[FFN_HALO:FFN_HALO + tm] * w[2:3] + bdw_ref[:, cols])

    nxt = up(0)
    for c in range(n_chunks):
        cur = nxt
        if c + 1 < n_chunks:
            nxt = up(c + 1)
        act = _silu(conv(cur[0], c * tf)) * conv(cur[1], F + c * tf)
        act_ref[:, c * tf:(c + 1) * tf] = act.astype(_BF)
        if (c + 1) % FFN_DOWN_GROUP == 0 or c + 1 == n_chunks:
            lo = (c // FFN_DOWN_GROUP) * FFN_DOWN_GROUP * tf
            part = _dot(act_ref[:, lo:(c + 1) * tf], wdn_ref[lo:(c + 1) * tf, :])
            if lo == 0:
                acc_ref[...] = part
            else:
                acc_ref[...] += part
    xn = x + g2_ref[0] * acc_ref[...]
    if final:
        ms = jnp.mean(xn * xn, axis=-1, keepdims=True)
        xn = xn * lax.rsqrt(ms + EPS) * fg_ref[...]
    o_ref[0] = xn


def _ffn(x, g, sc, sh, g2, w_up, w_dw, b_dw, w_down, final_g, final):
    B, S, D = x.shape
    F = w_down.shape[0]
    tm = 512
    hb = tm // FFN_HALO
    kern = functools.partial(_ffn_kernel, final=final)
    mod = lambda: pl.BlockSpec((1, 1, D), lambda b, i: (b, 0, 0))
    return pl.pallas_call(
        kern,
        grid=(B, S // tm),
        in_specs=[pl.BlockSpec((1, tm, D), lambda b, i: (b, i, 0)),
                  pl.BlockSpec((1, FFN_HALO, D), lambda b, i: (b, jnp.maximum(i * hb - 1, 0), 0)),
                  _full((1, D)), mod(), mod(), mod(),
                  _full((D, 2 * F)), _full((3, 2 * F)), _full((1, 2 * F)), _full((F, D)),
                  _full((1, D))],
        out_specs=pl.BlockSpec((1, tm, D), lambda b, i: (b, i, 0)),
        out_shape=jax.ShapeDtypeStruct((B, S, D), _F32),
        scratch_shapes=[pltpu.VMEM((FFN_HALO + tm, D), _BF), pltpu.VMEM((tm, F), _BF),
                        pltpu.VMEM((tm, D), _F32)],
        compiler_params=_params(("parallel", "parallel")),
        name="ffn",
    )(x, x, g, sc, sh, g2, w_up, w_dw, b_dw, w_down, final_g)


def kernel(x, c, rel_table, ada_w, ada_b, norm_g, final_g, da_w_in, da_lam, da_subln_g, da_w_out, cv_w_pw1, cv_b_pw1, cv_w_dw, cv_b_dw, cv_ln_g, cv_ln_b, cv_w_pw2, cv_b_pw2, sa_w_in, sa_kv_g, sa_w_uk, sa_w_uv, sa_w_out, sg_w_in, sg_b_in, sg_ln_g, sg_ln_b, sg_w_s, sg_b_s, sg_w_out, sg_b_out, ff_w_up, ff_w_dw, ff_b_dw, ff_w_down):
    B, S, D = x.shape
    depth = ada_w.shape[0]
    n_mixers = 4
    H = N_HEADS
    mods = _ada(c, ada_w, ada_b)
    bn = _bias_tiles(rel_table)
    row = lambda v: v.reshape(1, -1)

    for layer in range(depth):
        kind = layer % n_mixers
        j = layer // n_mixers
        sh1, sc1, g1, sh2, sc2, g2 = [m.reshape(B, 1, D) for m in jnp.split(mods[layer], 6, axis=-1)]
        ng1 = row(norm_g[layer, 0])
        if kind == 0:
            lambda_init = 0.8 - 0.6 * math.exp(-0.3 * layer)
            qkv = _in_da(x, ng1, sc1, sh1, da_w_in[j].astype(_BF))
            o = _diff_attn(qkv, rel_table, bn, da_lam[j], row(da_subln_g[j]), lambda_init)
            x = _out_proj(o, da_w_out[j].astype(_BF), x, g1)
        elif kind == 1:
            a = _in_cv(x, ng1, sc1, sh1, cv_w_pw1[j].astype(_BF), row(cv_b_pw1[j]))
            x = _cv_tail(a, cv_w_dw[j], row(cv_b_dw[j]), row(cv_ln_g[j]), row(cv_ln_b[j]),
                         cv_w_pw2[j].astype(_BF), row(cv_b_pw2[j]), x, g1)
        elif kind == 2:
            w = sa_w_in[j]
            o1 = H * HEAD_DIM
            o2 = o1 + SA_LATENT
            o3 = o2 + H * IDX_DIM
            o4 = o3 + IDX_DIM
            w_ik = w[:, o3:o4]
            w_iw = jnp.pad(w[:, o4:], ((0, 0), (0, 128 - (w.shape[1] - o4))))
            w_cat = jnp.concatenate([w[:, :o3], w_ik, w_ik, w_iw], axis=1).astype(_BF)
            q, ckv, iq, ikk, iw = _in_sa(x, ng1, sc1, sh1, w_cat, row(sa_kv_g[j]))
            wuk = jnp.transpose(sa_w_uk[j], (1, 2, 0)).astype(_BF)
            wuv = jnp.transpose(sa_w_uv[j], (1, 0, 2)).astype(_BF)
            o = _sparse_attn(q, ckv, iq, ikk, iw, wuk, wuv, rel_table, bn)
            x = _out_proj(o, sa_w_out[j].astype(_BF), x, g1)
        else:
            u, v = _in_sg(x, ng1, sc1, sh1, sg_w_in[j].astype(_BF), row(sg_b_in[j]),
                          row(sg_ln_g[j]), row(sg_ln_b[j]))
            x = _sg_tail(u, v, sg_w_s[j], sg_b_s[j].T, sg_w_out[j].astype(_BF), row(sg_b_out[j]), x, g1)
        x = _ffn(x, row(norm_g[layer, 1]), sc2, sh2, g2, ff_w_up[layer].astype(_BF), ff_w_dw[layer],
                 row(ff_b_dw[layer]), ff_w_down[layer].astype(_BF), row(final_g), layer == depth - 1)
    return x
```

```python
import functools
import math

import jax
import jax.numpy as jnp
import numpy as np
from jax import lax
from jax.experimental import pallas as pl
from jax.experimental.pallas import tpu as pltpu

_BF = jnp.bfloat16
_F32 = jnp.float32
_I32 = jnp.int32

EPS = 1e-6
SUBLN_EPS = 1e-5
NEG = -1e30
CHUNK = 64
CHUNK_SHIFT = 6
N_HEADS = 8
HEAD_DIM = 128
DA_DIM = 64
SA_LATENT = 256
IDX_DIM = 64
TOPK = 256
N_BUCKETS = 32
MAX_DISTANCE = 128
SG_CHUNK = 128
SG_GROUPS = 8
CONV_WIDTH = 31
CONV_HALO = 32
CONV_ROWS = 32
SUBLANES = 8
FFN_HALO = 8
FFN_CHUNK = 256
FFN_DOWN_GROUP = 4
ATT_TILE = 512
NEAR_W = 768
VMEM_LIMIT = 56 * 1024 * 1024


def _dot(a, b):
    return jnp.dot(a, b, preferred_element_type=_F32)


def _dot_nt(a, b):
    return lax.dot_general(a, b, (((1,), (1,)), ((), ())), preferred_element_type=_F32)


def _rms_mod(x, g, sc, sh):
    ms = jnp.mean(x * x, axis=-1, keepdims=True)
    return (x * lax.rsqrt(ms + EPS) * g) * (1.0 + sc) + sh


def _layernorm(x, g, b):
    mu = jnp.mean(x, axis=-1, keepdims=True)
    xc = x - mu
    var = jnp.mean(xc * xc, axis=-1, keepdims=True)
    return xc * lax.rsqrt(var + EPS) * g + b


def _silu(x):
    return x * jax.nn.sigmoid(x)


def _params(sem):
    return pltpu.CompilerParams(dimension_semantics=sem, vmem_limit_bytes=VMEM_LIMIT)


def _full(shape):
    n = len(shape)
    return pl.BlockSpec(shape, lambda *_: (0,) * n, pipeline_mode=pl.Buffered(1))


def _ada_kernel(c_ref, w_ref, b_ref, o_ref):
    ca = _silu(c_ref[...]).astype(_BF)
    o_ref[0] = _dot(ca, w_ref[0].astype(_BF)) + b_ref[0]


def _ada(c, ada_w, ada_b):
    L, D, N = ada_w.shape
    B = c.shape[0]
    tn = N // 4
    return pl.pallas_call(
        _ada_kernel,
        grid=(L, N // tn),
        in_specs=[pl.BlockSpec((B, D), lambda l, j: (0, 0)),
                  pl.BlockSpec((1, D, tn), lambda l, j: (l, 0, j)),
                  pl.BlockSpec((1, 1, tn), lambda l, j: (l, 0, j))],
        out_specs=pl.BlockSpec((1, B, tn), lambda l, j: (l, 0, j)),
        out_shape=jax.ShapeDtypeStruct((L, B, N), _F32),
        compiler_params=_params(("arbitrary", "arbitrary")),
        name="ada",
    )(c, ada_w, ada_b.reshape(L, 1, N))


def _bias_kernel(tbl_ref, bn_ref):
    h = pl.program_id(0)
    shape = bn_ref.shape[1:]
    i = lax.broadcasted_iota(_I32, shape, 0)
    j = lax.broadcasted_iota(_I32, shape, 1) - (NEAR_W - ATT_TILE)
    nb = N_BUCKETS // 2
    max_exact = nb // 2
    rel = j - i
    ret = jnp.where(rel > 0, nb, 0)
    n = jnp.abs(rel)
    nf = jnp.maximum(n, 1).astype(_F32)
    large = max_exact + (jnp.log(nf / max_exact) / math.log(MAX_DISTANCE / max_exact)
                         * (nb - max_exact)).astype(_I32)
    large = jnp.minimum(large, nb - 1)
    bucket = ret + jnp.where(n < max_exact, n, large)
    out = jnp.zeros(shape, _F32)
    for bk in range(N_BUCKETS):
        out = jnp.where(bucket == bk, tbl_ref[bk, h], out)
    visible = (j >> CHUNK_SHIFT) <= (i >> CHUNK_SHIFT)
    bn_ref[0] = jnp.where(visible, out, NEG)


def _bias_tiles(rel_table):
    H = rel_table.shape[1]
    return pl.pallas_call(
        _bias_kernel,
        grid=(H,),
        in_specs=[pl.BlockSpec(memory_space=pltpu.SMEM)],
        out_specs=pl.BlockSpec((1, ATT_TILE, NEAR_W), lambda h: (h, 0, 0)),
        out_shape=jax.ShapeDtypeStruct((H, ATT_TILE, NEAR_W), _F32),
        compiler_params=_params(("arbitrary",)),
        name="bias_tiles",
    )(rel_table)


def _in_specs_common(tm, D):
    return [pl.BlockSpec((1, tm, D), lambda b, i: (b, i, 0)),
            pl.BlockSpec((1, D), lambda b, i: (0, 0)),
            pl.BlockSpec((1, 1, D), lambda b, i: (b, 0, 0)),
            pl.BlockSpec((1, 1, D), lambda b, i: (b, 0, 0))]


def _in_da_kernel(x_ref, g_ref, sc_ref, sh_ref, w_ref, o_ref):
    hn = _rms_mod(x_ref[0], g_ref[...], sc_ref[0], sh_ref[0]).astype(_BF)
    N = w_ref.shape[1]
    tn = 512
    for c in range(N // tn):
        o_ref[0, :, c * tn:(c + 1) * tn] = _dot(hn, w_ref[:, c * tn:(c + 1) * tn]).astype(_BF)


def _in_da(x, g, sc, sh, w):
    B, S, D = x.shape
    N = w.shape[1]
    tm = 512
    return pl.pallas_call(
        _in_da_kernel,
        grid=(B, S // tm),
        in_specs=_in_specs_common(tm, D) + [_full((D, N))],
        out_specs=pl.BlockSpec((1, tm, N), lambda b, i: (b, i, 0)),
        out_shape=jax.ShapeDtypeStruct((B, S, N), _BF),
        compiler_params=_params(("parallel", "parallel")),
        name="in_da",
    )(x, g, sc, sh, w)


def _in_cv_kernel(x_ref, g_ref, sc_ref, sh_ref, w_ref, b_ref, o_ref):
    hn = _rms_mod(x_ref[0], g_ref[...], sc_ref[0], sh_ref[0]).astype(_BF)
    D = o_ref.shape[2]
    tn = 512
    n = D // tn
    lo = lambda c: slice(c * tn, (c + 1) * tn)
    hi = lambda c: slice(D + c * tn, D + (c + 1) * tn)
    dots = lambda c: (_dot(hn, w_ref[:, lo(c)]), _dot(hn, w_ref[:, hi(c)]))
    nxt = dots(0)
    for c in range(n):
        cur = nxt
        if c + 1 < n:
            nxt = dots(c + 1)
        o_ref[0, :, lo(c)] = (cur[0] + b_ref[:, lo(c)]) * jax.nn.sigmoid(cur[1] + b_ref[:, hi(c)])


def _in_cv(x, g, sc, sh, w, b):
    B, S, D = x.shape
    N = w.shape[1]
    tm = 512
    return pl.pallas_call(
        _in_cv_kernel,
        grid=(B, S // tm),
        in_specs=_in_specs_common(tm, D) + [_full((D, N)), _full((1, N))],
        out_specs=pl.BlockSpec((1, tm, D), lambda b, i: (b, i, 0)),
        out_shape=jax.ShapeDtypeStruct((B, S, D), _F32),
        compiler_params=_params(("parallel", "parallel")),
        name="in_cv",
    )(x, g, sc, sh, w, b)


def _in_sa_kernel(x_ref, g_ref, sc_ref, sh_ref, w_ref, kvg_ref,
                  q_ref, ckv_ref, iq_ref, ikk_ref, iw_ref):
    hn = _rms_mod(x_ref[0], g_ref[...], sc_ref[0], sh_ref[0]).astype(_BF)
    H = q_ref.shape[1]
    for c in range(H // 2):
        r = _dot(hn, w_ref[:, c * 256:(c + 1) * 256])
        q_ref[0, 2 * c] = r[:, :HEAD_DIM].astype(_BF)
        q_ref[0, 2 * c + 1] = r[:, HEAD_DIM:].astype(_BF)
    o = H * HEAD_DIM
    ckv = _dot(hn, w_ref[:, o:o + SA_LATENT])
    ms = jnp.mean(ckv * ckv, axis=-1, keepdims=True)
    ckv_ref[0] = (ckv * lax.rsqrt(ms + EPS) * kvg_ref[...]).astype(_BF)
    o += SA_LATENT
    iq_ref[0] = (_dot(hn, w_ref[:, o:o + 512]) * (IDX_DIM ** -0.5)).astype(_BF)
    o += 512
    r = _dot(hn, w_ref[:, o:o + 256])
    ikk_ref[0] = r[:, :128].astype(_BF)
    iw_ref[0] = r[:, 128:] * (N_HEADS ** -0.5)


def _in_sa(x, g, sc, sh, w, kv_g):
    B, S, D = x.shape
    N = w.shape[1]
    H = N_HEADS
    tm = 512
    row = lambda n: pl.BlockSpec((1, tm, n), lambda b, i: (b, i, 0))
    return pl.pallas_call(
        _in_sa_kernel,
        grid=(B, S // tm),
        in_specs=_in_specs_common(tm, D) + [_full((D, N)), _full((1, SA_LATENT))],
        out_specs=[pl.BlockSpec((1, H, tm, HEAD_DIM), lambda b, i: (b, 0, i, 0)),
                   row(SA_LATENT), row(512), row(128), row(128)],
        out_shape=[jax.ShapeDtypeStruct((B, H, S, HEAD_DIM), _BF),
                   jax.ShapeDtypeStruct((B, S, SA_LATENT), _BF),
                   jax.ShapeDtypeStruct((B, S, 512), _BF),
                   jax.ShapeDtypeStruct((B, S, 128), _BF),
                   jax.ShapeDtypeStruct((B, S, 128), _F32)],
        compiler_params=_params(("parallel", "parallel")),
        name="in_sa",
    )(x, g, sc, sh, w, kv_g)


def _in_sg_kernel(x_ref, g_ref, sc_ref, sh_ref, w_ref, b_ref, lg_ref, lb_ref, u_ref, v_ref, a_scr):
    hn = _rms_mod(x_ref[0], g_ref[...], sc_ref[0], sh_ref[0]).astype(_BF)
    W = u_ref.shape[2]
    tn = 512
    n = W // tn
    lo = lambda c: slice(c * tn, (c + 1) * tn)
    hi = lambda c: slice(W + c * tn, W + (c + 1) * tn)
    cols = [hi(c) for c in range(n)] + [lo(c) for c in range(n)]
    nxt = _dot(hn, w_ref[:, cols[0]])
    for t in range(2 * n):
        cur = nxt
        if t + 1 < 2 * n:
            nxt = _dot(hn, w_ref[:, cols[t + 1]])
        act = jax.nn.gelu(cur + b_ref[:, cols[t]], approximate=True)
        if t < n:
            a_scr[:, lo(t)] = act
            if t == n - 1:
                v_ref[0] = _layernorm(a_scr[...], lg_ref[...], lb_ref[...]).astype(_BF)
        else:
            u_ref[0, :, lo(t - n)] = act


def _in_sg(x, g, sc, sh, w, b, ln_g, ln_b):
    B, S, D = x.shape
    N = w.shape[1]
    W = N // 2
    tm = 512
    row = lambda: pl.BlockSpec((1, tm, W), lambda b, i: (b, i, 0))
    return pl.pallas_call(
        _in_sg_kernel,
        grid=(B, S // tm),
        in_specs=_in_specs_common(tm, D) + [_full((D, N)), _full((1, N)), _full((1, W)), _full((1, W))],
        out_specs=[row(), row()],
        out_shape=[jax.ShapeDtypeStruct((B, S, W), _F32), jax.ShapeDtypeStruct((B, S, W), _BF)],
        scratch_shapes=[pltpu.VMEM((tm, W), _F32)],
        compiler_params=_params(("parallel", "parallel")),
        name="in_sg",
    )(x, g, sc, sh, w, b, ln_g, ln_b)


def _attend(qs, k_ref, v_ref, W, fars, near_biases, madd_ref=None):
    nw = min(W, NEAR_W)
    fw = W - nw
    logits = []
    for q, far, near_bias in zip(qs, fars, near_biases):
        s_n = _dot_nt(q, k_ref[0, fw:W, :]) + near_bias(NEAR_W - nw)
        s_f = None
        if fw:
            s_f = _dot_nt(q, k_ref[0, 0:fw, :]) + far
        if madd_ref is not None:
            s_n = s_n + madd_ref[:, fw:W]
            if fw:
                s_f = s_f + madd_ref[:, 0:fw]
        logits.append((s_n, s_f))
    outs = []
    for s_n, s_f in logits:
        m = jnp.max(s_n, axis=1, keepdims=True)
        if fw:
            m = jnp.maximum(m, jnp.max(s_f, axis=1, keepdims=True))
            p_f = jnp.exp(s_f - m)
        p_n = jnp.exp(s_n - m)
        l = jnp.sum(p_n, axis=1, keepdims=True)
        o = _dot(p_n.astype(_BF), v_ref[0, fw:W, :])
        if fw:
            l = l + jnp.sum(p_f, axis=1, keepdims=True)
            o = o + _dot(p_f.astype(_BF), v_ref[0, 0:fw, :])
        outs.append(o * (1.0 / l))
    return outs


def _da_kernel(tbl_ref, q_ref, k_ref, v_ref, bn_ref, lam_ref, g_ref, o_ref, *, lambda_init):
    h = pl.program_id(1)
    a = pl.program_id(2)
    T = q_ref.shape[1]
    S = k_ref.shape[1]
    lane = lax.broadcasted_iota(_I32, (T, HEAD_DIM), 1)
    q = q_ref[0] * (DA_DIM ** -0.5)
    zero = jnp.zeros_like(q)
    q1 = jnp.where(lane < DA_DIM, q, zero)
    q2 = jnp.where(lane >= DA_DIM, q, zero)
    far = tbl_ref[N_BUCKETS // 2 - 1, h]
    lam = lam_ref[...]
    lam_full = (jnp.exp(jnp.sum(lam[0:1] * lam[1:2], axis=1, keepdims=True))
                - jnp.exp(jnp.sum(lam[2:3] * lam[3:4], axis=1, keepdims=True)) + lambda_init)
    near_bias = lambda lo: bn_ref[0, :, lo:]

    for br in range(S // T):
        @pl.when(a == br)
        def _(br=br):
            W = (br + 1) * T
            o1, o2 = _attend([q1, q2], k_ref, v_ref, W, [far, far], [near_bias, near_bias])
            o = o1 - lam_full * o2
            ms = jnp.mean(o * o, axis=-1, keepdims=True)
            o = (o * lax.rsqrt(ms + SUBLN_EPS) * g_ref[...]) * (1.0 - lambda_init)
            o_ref[0] = o.astype(_BF)


def _diff_attn(qkv, rel_table, bn, lam, subln_g, lambda_init):
    B, S, _ = qkv.shape
    H = N_HEADS
    T = ATT_TILE
    kern = functools.partial(_da_kernel, lambda_init=lambda_init)
    return pl.pallas_call(
        kern,
        grid=(B, H, S // T),
        in_specs=[pl.BlockSpec(memory_space=pltpu.SMEM),
                  pl.BlockSpec((1, T, HEAD_DIM), lambda b, h, i: (b, i, h)),
                  pl.BlockSpec((1, S, HEAD_DIM), lambda b, h, i: (b, 0, H + h)),
                  pl.BlockSpec((1, S, HEAD_DIM), lambda b, h, i: (b, 0, 2 * H + h)),
                  pl.BlockSpec((1, T, NEAR_W), lambda b, h, i: (h, 0, 0)),
                  pl.BlockSpec((4, DA_DIM), lambda b, h, i: (0, 0)),
                  pl.BlockSpec((1, HEAD_DIM), lambda b, h, i: (0, 0))],
        out_specs=pl.BlockSpec((1, T, HEAD_DIM), lambda b, h, i: (b, i, h)),
        out_shape=jax.ShapeDtypeStruct((B, S, H * HEAD_DIM), _BF),
        compiler_params=_params(("parallel", "parallel", "arbitrary")),
        name="diff_attn",
    )(rel_table, qkv, qkv, qkv, bn, lam, subln_g)


HALF_BIAS = 1 << 15
_KEY_NEG_INF = np.int32(np.array(-np.inf, np.float32).view(np.int32) ^ np.int32(0x7FFFFFFF))
SEARCH_GROUPS = 4
HEADS_PER_STEP = 2


def _sa_branch(W, tbl_ref, q_ref, ckv_ref, iw_ref, ikk_ref, wuk_ref, wuv_ref, bn_ref,
               iqm_ref, keys_ref, k16_ref, madd_ref, oh_ref, last_ref):
    T = iw_ref.shape[1]
    H = q_ref.shape[1]
    iw = iw_ref[0]

    for cb in range(W // T):
        cols = slice(cb * T, (cb + 1) * T)
        ikt = ikk_ref[0, cols, :]
        score = jnp.zeros((T, T), _F32)
        for hh in range(H):
            score = score + jnp.maximum(_dot_nt(iqm_ref[hh], ikt), 0.0) * iw[:, hh:hh + 1]
        score = jnp.where(score == 0.0, 0.0, score)
        if cb == W // T - 1:
            row = lax.broadcasted_iota(_I32, (T, T), 0)
            col = lax.broadcasted_iota(_I32, (T, T), 1)
            score = jnp.where((col >> CHUNK_SHIFT) <= (row >> CHUNK_SHIFT), score, -jnp.inf)
        bits = lax.bitcast_convert_type(score, _I32)
        keys_ref[:, cols] = bits ^ ((bits >> 31) & np.int32(0x7FFFFFFF))

    def count(pred, rows=slice(None)):
        return jnp.sum(jnp.where(pred(keys_ref[rows, 0:W]), 1.0, 0.0), axis=1, keepdims=True)

    R = T // SEARCH_GROUPS
    groups = [slice(g * R, (g + 1) * R) for g in range(SEARCH_GROUPS)]

    def count16(pred, rows):
        ones = jnp.where(pred(k16_ref[rows, 0:W]), jnp.int16(1), jnp.int16(0))
        acc = ones[:, 0:128]
        for c in range(1, W // 128):
            acc = acc + ones[:, c * 128:(c + 1) * 128]
        return jnp.sum(acc.astype(_F32), axis=1, keepdims=True)

    def half_search(need):
        def body(it, us):
            bit = lax.shift_left(np.int32(1), np.int32(15) - it)
            out = []
            for g in range(SEARCH_GROUPS):
                cand_u = us[g] | bit
                cand = (cand_u - HALF_BIAS).astype(jnp.int16)
                cnt = count16(lambda h: h >= cand, groups[g])
                out.append(jnp.where(cnt >= need[g], cand_u, us[g]))
            return tuple(out)
        return lax.fori_loop(0, 16, body, tuple(jnp.zeros((R, 1), _I32) for _ in range(SEARCH_GROUPS)))

    k16_ref[:, 0:W] = (keys_ref[:, 0:W] >> 16).astype(jnp.int16)
    hi_u = half_search([float(TOPK)] * SEARCH_GROUPS)
    hi_s = [u - HALF_BIAS for u in hi_u]
    above = [count16(lambda h, t=t.astype(jnp.int16): h > t, rows) for t, rows in zip(hi_s, groups)]
    hi_all = jnp.concatenate(hi_s, axis=0)
    k = keys_ref[:, 0:W]
    lo_s = (k & np.int32(0xFFFF)) - HALF_BIAS
    k16_ref[:, 0:W] = jnp.where((k >> 16) == hi_all, lo_s, -HALF_BIAS).astype(jnp.int16)
    lo_u = half_search([float(TOPK) - a for a in above])
    thr = (hi_all << 16) | jnp.concatenate(lo_u, axis=0)

    cnt_gt = count(lambda k: k > thr)
    cnt_ge = count(lambda k: k >= thr)
    need = float(TOPK) - cnt_gt
    last_ref[...] = jnp.full((T, 1), 2 ** 30, _I32)
    tied = jnp.logical_and(cnt_ge - cnt_gt > need, thr > _KEY_NEG_INF)

    @pl.when(jnp.max(jnp.where(tied, 1.0, 0.0)) > 0.0)
    def _():
        col = lax.broadcasted_iota(_I32, (T, W), 1)

        def idx_body(it, v):
            cand = v | lax.shift_left(np.int32(1), np.int32(10) - it)
            below = count(lambda k: jnp.logical_and(k == thr, col < cand))
            return jnp.where(below < need, cand, v)
        last_ref[...] = lax.fori_loop(0, 11, idx_body, jnp.zeros((T, 1), _I32))

    last = last_ref[...]
    k = keys_ref[:, 0:W]
    col = lax.broadcasted_iota(_I32, (T, W), 1)
    sel = jnp.logical_or(k > thr, jnp.logical_and(k == thr, col <= last))
    madd_ref[:, 0:W] = jnp.where(sel, 0.0, NEG)

    def head_body(hp, carry):
        hs = [hp * HEADS_PER_STEP + u for u in range(HEADS_PER_STEP)]
        qlats = [(_dot(q_ref[0, h], wuk_ref[h]) * (HEAD_DIM ** -0.5)).astype(_BF) for h in hs]
        olats = _attend(qlats, ckv_ref, ckv_ref, W,
                        [tbl_ref[N_BUCKETS // 2 - 1, h] for h in hs],
                        [functools.partial(lambda h, lo: bn_ref[h, :, lo:], h) for h in hs], madd_ref)
        for h, olat in zip(hs, olats):
            oh_ref[h] = _dot(olat.astype(_BF), wuv_ref[h])
        return carry

    lax.fori_loop(0, H // HEADS_PER_STEP, head_body, 0)


def _sa_kernel(tbl_ref, q_ref, ckv_ref, iq_ref, ikk_ref, iw_ref, wuk_ref, wuv_ref, bn_ref,
               o_ref, iqm_ref, keys_ref, k16_ref, madd_ref, oh_ref, last_ref):
    a = pl.program_id(1)
    T = iq_ref.shape[1]
    S = ckv_ref.shape[1]
    H = q_ref.shape[1]
    lane = lax.broadcasted_iota(_I32, (T, 128), 1)
    for p in range(H // 2):
        iqp = iq_ref[0, :, p * 128:(p + 1) * 128]
        zero = jnp.zeros_like(iqp)
        iqm_ref[2 * p] = jnp.where(lane < IDX_DIM, iqp, zero)
        iqm_ref[2 * p + 1] = jnp.where(lane >= IDX_DIM, iqp, zero)

    for br in range(S // T):
        @pl.when(a == br)
        def _(br=br):
            _sa_branch((br + 1) * T, tbl_ref, q_ref, ckv_ref, iw_ref, ikk_ref, wuk_ref, wuv_ref, bn_ref,
                       iqm_ref, keys_ref, k16_ref, madd_ref, oh_ref, last_ref)

    for h in range(H):
        o_ref[0, :, h * HEAD_DIM:(h + 1) * HEAD_DIM] = oh_ref[h].astype(_BF)


def _sparse_attn(q, ckv, iq, ikk, iw, wuk, wuv, rel_table, bn):
    B, H, S, _ = q.shape
    T = ATT_TILE
    return pl.pallas_call(
        _sa_kernel,
        grid=(B, S // T),
        in_specs=[pl.BlockSpec(memory_space=pltpu.SMEM),
                  pl.BlockSpec((1, H, T, HEAD_DIM), lambda b, i: (b, 0, i, 0)),
                  pl.BlockSpec((1, S, SA_LATENT), lambda b, i: (b, 0, 0)),
                  pl.BlockSpec((1, T, 512), lambda b, i: (b, i, 0)),
                  pl.BlockSpec((1, S, 128), lambda b, i: (b, 0, 0)),
                  pl.BlockSpec((1, T, 128), lambda b, i: (b, i, 0)),
                  _full((H, HEAD_DIM, SA_LATENT)),
                  _full((H, SA_LATENT, HEAD_DIM)),
                  _full((H, T, NEAR_W))],
        out_specs=pl.BlockSpec((1, T, H * HEAD_DIM), lambda b, i: (b, i, 0)),
        out_shape=jax.ShapeDtypeStruct((B, S, H * HEAD_DIM), _BF),
        scratch_shapes=[pltpu.VMEM((H, T, 128), _BF),
                        pltpu.VMEM((T, S), _I32),
                        pltpu.VMEM((T, S), jnp.int16),
                        pltpu.VMEM((T, S), _F32),
                        pltpu.VMEM((H, T, HEAD_DIM), _F32),
                        pltpu.VMEM((T, 1), _I32)],
        compiler_params=_params(("parallel", "arbitrary")),
        name="sparse_attn",
    )(rel_table, q, ckv, iq, ikk, iw, wuk, wuv, bn)


def _cv_kernel(a_ref, ah_ref, wdw_ref, bdw_ref, lg_ref, lb_ref, w2_ref, b2_ref, x_ref, g1_ref, o_ref,
               ext_ref, y_ref):
    i = pl.program_id(1)
    tm = a_ref.shape[1]
    n_ext = CONV_HALO + tm
    ext_ref[0, CONV_HALO:n_ext] = a_ref[0]
    halo = ah_ref[0]
    ext_ref[0, 0:CONV_HALO] = jnp.where(i > 0, halo, jnp.zeros_like(halo))
    for r in range(1, SUBLANES):
        ext_ref[r, 0:n_ext - SUBLANES] = ext_ref[0, r:r + n_ext - SUBLANES]
    rb = CONV_ROWS
    base = CONV_HALO - (CONV_WIDTH - 1)

    def block(blk, carry):
        row0 = pl.multiple_of(blk * rb, rb)
        acc = jnp.broadcast_to(bdw_ref[...], (rb, bdw_ref.shape[1]))
        for k in range(CONV_WIDTH):
            r = (base + k) % SUBLANES
            al = base + k - r
            acc = acc + wdw_ref[k:k + 1, :] * ext_ref[r, pl.ds(row0 + al, rb), :]
        y_ref[pl.ds(row0, rb)] = _silu(_layernorm(acc, lg_ref[...], lb_ref[...])).astype(_BF)
        return carry

    lax.fori_loop(0, tm // rb, block, 0, unroll=2)
    y = _dot(y_ref[...], w2_ref[...]) + b2_ref[...]
    o_ref[0] = x_ref[0] + g1_ref[0] * y


def _cv_tail(a, w_dw, b_dw, ln_g, ln_b, w2, b2, x, g1):
    B, S, D = a.shape
    tm = 512
    hb = tm // CONV_HALO
    vec = lambda: _full((1, D))
    return pl.pallas_call(
        _cv_kernel,
        grid=(B, S // tm),
        in_specs=[pl.BlockSpec((1, tm, D), lambda b, i: (b, i, 0)),
                  pl.BlockSpec((1, CONV_HALO, D), lambda b, i: (b, jnp.maximum(i * hb - 1, 0), 0)),
                  _full((CONV_WIDTH, D)), vec(), vec(), vec(), _full((D, D)), vec(),
                  pl.BlockSpec((1, tm, D), lambda b, i: (b, i, 0)),
                  pl.BlockSpec((1, 1, D), lambda b, i: (b, 0, 0))],
        out_specs=pl.BlockSpec((1, tm, D), lambda b, i: (b, i, 0)),
        out_shape=jax.ShapeDtypeStruct((B, S, D), _F32),
        scratch_shapes=[pltpu.VMEM((SUBLANES, CONV_HALO + tm, D), _F32), pltpu.VMEM((tm, D), _BF)],
        compiler_params=_params(("parallel", "parallel")),
        name="cv_tail",
    )(a, a, w_dw, b_dw, ln_g, ln_b, w2, b2, x, g1)


def _sg_kernel(u_ref, v_ref, ws_ref, bs_ref, wo_ref, bo_ref, x_ref, g1_ref, o_ref, z_ref):
    tm = u_ref.shape[1]
    G = ws_ref.shape[0]
    C = SG_CHUNK
    gw = u_ref.shape[2] // G
    t = lax.broadcasted_iota(_I32, (C, C), 0)
    s = lax.broadcasted_iota(_I32, (C, C), 1)
    for g in range(G):
        wg = jnp.where(t >= s, ws_ref[g], 0.0).astype(_BF)
        bcol = bs_ref[:, g:g + 1]
        cols = slice(g * gw, (g + 1) * gw)
        for n in range(tm // C):
            rows = slice(n * C, (n + 1) * C)
            sv = _dot(wg, v_ref[0, rows, cols]) + bcol
            z_ref[rows, cols] = (u_ref[0, rows, cols] * sv).astype(_BF)
    y = _dot(z_ref[...], wo_ref[...]) + bo_ref[...]
    o_ref[0] = x_ref[0] + g1_ref[0] * y


def _sg_tail(u, v, w_s, b_s_t, w_out, b_out, x, g1):
    B, S, W = u.shape
    D = x.shape[2]
    G = w_s.shape[0]
    tm = 256
    return pl.pallas_call(
        _sg_kernel,
        grid=(B, S // tm),
        in_specs=[pl.BlockSpec((1, tm, W), lambda b, i: (b, i, 0)),
                  pl.BlockSpec((1, tm, W), lambda b, i: (b, i, 0)),
                  _full((G, SG_CHUNK, SG_CHUNK)), _full((SG_CHUNK, G)),
                  _full((W, D)), _full((1, D)),
                  pl.BlockSpec((1, tm, D), lambda b, i: (b, i, 0)),
                  pl.BlockSpec((1, 1, D), lambda b, i: (b, 0, 0))],
        out_specs=pl.BlockSpec((1, tm, D), lambda b, i: (b, i, 0)),
        out_shape=jax.ShapeDtypeStruct((B, S, D), _F32),
        scratch_shapes=[pltpu.VMEM((tm, W), _BF)],
        compiler_params=_params(("parallel", "parallel")),
        name="sg_tail",
    )(u, v, w_s, b_s_t, w_out, b_out, x, g1)


def _out_kernel(a_ref, w_ref, x_ref, g1_ref, o_ref):
    o_ref[0] = x_ref[0] + g1_ref[0] * _dot(a_ref[0], w_ref[...])


def _out_proj(a, w, x, g1):
    B, S, D = x.shape
    K = a.shape[2]
    tm = 512
    return pl.pallas_call(
        _out_kernel,
        grid=(B, S // tm),
        in_specs=[pl.BlockSpec((1, tm, K), lambda b, i: (b, i, 0)),
                  _full((K, D)),
                  pl.BlockSpec((1, tm, D), lambda b, i: (b, i, 0)),
                  pl.BlockSpec((1, 1, D), lambda b, i: (b, 0, 0))],
        out_specs=pl.BlockSpec((1, tm, D), lambda b, i: (b, i, 0)),
        out_shape=jax.ShapeDtypeStruct((B, S, D), _F32),
        compiler_params=_params(("parallel", "parallel")),
        name="out_proj",
    )(a, w, x, g1)


def _ffn_kernel(x_ref, xh_ref, g_ref, sc_ref, sh_ref, g2_ref, wup_ref, wdw_ref, bdw_ref, wdn_ref,
                fg_ref, o_ref, he_ref, act_ref, acc_ref, *, final):
    i = pl.program_id(1)
    tm = x_ref.shape[1]
    F = wdn_ref.shape[0]
    x = x_ref[0]
    he_ref[FFN_HALO:FFN_HALO + tm] = _rms_mod(x, g_ref[...], sc_ref[0], sh_ref[0]).astype(_BF)
    hh = _rms_mod(xh_ref[0], g_ref[...], sc_ref[0], sh_ref[0])
    he_ref[0:FFN_HALO] = jnp.where(i > 0, hh, jnp.zeros_like(hh)).astype(_BF)
    tf = FFN_CHUNK
    n_chunks = F // tf

    def up(c):
        he = he_ref[...]
        return (_dot(he, wup_ref[:, c * tf:(c + 1) * tf]), _dot(he, wup_ref[:, F + c * tf:F + (c + 1) * tf]))

    def conv(a, col0):
        cols = slice(col0, col0 + tf)
        w = wdw_ref[:, cols]
        return (a[FFN_HALO - 2:FFN_HALO - 2 + tm] * w[0:1]
                + a[FFN_HALO - 1:FFN_HALO - 1 + tm] * w[1:2]
                + a[FFN_HALO:FFN_HALO + tm] * w[2:3] + bdw_ref[:, cols])

    nxt = up(0)
    for c in range(n_chunks):
        cur = nxt
        if c + 1 < n_chunks:
            nxt = up(c + 1)
        act = _silu(conv(cur[0], c * tf)) * conv(cur[1], F + c * tf)
        act_ref[:, c * tf:(c + 1) * tf] = act.astype(_BF)
        if (c + 1) % FFN_DOWN_GROUP == 0 or c + 1 == n_chunks:
            lo = (c // FFN_DOWN_GROUP) * FFN_DOWN_GROUP * tf
            part = _dot(act_ref[:, lo:(c + 1) * tf], wdn_ref[lo:(c + 1) * tf, :])
            if lo == 0:
                acc_ref[...] = part
            else:
                acc_ref[...] += part
    xn = x + g2_ref[0] * acc_ref[...]
    if final:
        ms = jnp.mean(xn * xn, axis=-1, keepdims=True)
        xn = xn * lax.rsqrt(ms + EPS) * fg_ref[...]
    o_ref[0] = xn


def _ffn(x, g, sc, sh, g2, w_up, w_dw, b_dw, w_down, final_g, final):
    B, S, D = x.shape
    F = w_down.shape[0]
    tm = 512
    hb = tm // FFN_HALO
    kern = functools.partial(_ffn_kernel, final=final)
    mod = lambda: pl.BlockSpec((1, 1, D), lambda b, i: (b, 0, 0))
    return pl.pallas_call(
        kern,
        grid=(B, S // tm),
        in_specs=[pl.BlockSpec((1, tm, D), lambda b, i: (b, i, 0)),
                  pl.BlockSpec((1, FFN_HALO, D), lambda b, i: (b, jnp.maximum(i * hb - 1, 0), 0)),
                  _full((1, D)), mod(), mod(), mod(),
                  _full((D, 2 * F)), _full((3, 2 * F)), _full((1, 2 * F)), _full((F, D)),
                  _full((1, D))],
        out_specs=pl.BlockSpec((1, tm, D), lambda b, i: (b, i, 0)),
        out_shape=jax.ShapeDtypeStruct((B, S, D), _F32),
        scratch_shapes=[pltpu.VMEM((FFN_HALO + tm, D), _BF), pltpu.VMEM((tm, F), _BF),
                        pltpu.VMEM((tm, D), _F32)],
        compiler_params=_params(("parallel", "parallel")),
        name="ffn",
    )(x, x, g, sc, sh, g2, w_up, w_dw, b_dw, w_down, final_g)


def kernel(x, c, rel_table, ada_w, ada_b, norm_g, final_g, da_w_in, da_lam, da_subln_g, da_w_out, cv_w_pw1, cv_b_pw1, cv_w_dw, cv_b_dw, cv_ln_g, cv_ln_b, cv_w_pw2, cv_b_pw2, sa_w_in, sa_kv_g, sa_w_uk, sa_w_uv, sa_w_out, sg_w_in, sg_b_in, sg_ln_g, sg_ln_b, sg_w_s, sg_b_s, sg_w_out, sg_b_out, ff_w_up, ff_w_dw, ff_b_dw, ff_w_down):
    B, S, D = x.shape
    depth = ada_w.shape[0]
    n_mixers = 4
    H = N_HEADS
    mods = _ada(c, ada_w, ada_b)
    bn = _bias_tiles(rel_table)
    row = lambda v: v.reshape(1, -1)

    for layer in range(depth):
        kind = layer % n_mixers
        j = layer // n_mixers
        sh1, sc1, g1, sh2, sc2, g2 = [m.reshape(B, 1, D) for m in jnp.split(mods[layer], 6, axis=-1)]
        ng1 = row(norm_g[layer, 0])
        if kind == 0:
            lambda_init = 0.8 - 0.6 * math.exp(-0.3 * layer)
            qkv = _in_da(x, ng1, sc1, sh1, da_w_in[j].astype(_BF))
            o = _diff_attn(qkv, rel_table, bn, da_lam[j], row(da_subln_g[j]), lambda_init)
            x = _out_proj(o, da_w_out[j].astype(_BF), x, g1)
        elif kind == 1:
            a = _in_cv(x, ng1, sc1, sh1, cv_w_pw1[j].astype(_BF), row(cv_b_pw1[j]))
            x = _cv_tail(a, cv_w_dw[j], row(cv_b_dw[j]), row(cv_ln_g[j]), row(cv_ln_b[j]),
                         cv_w_pw2[j].astype(_BF), row(cv_b_pw2[j]), x, g1)
        elif kind == 2:
            w = sa_w_in[j]
            o1 = H * HEAD_DIM
            o2 = o1 + SA_LATENT
            o3 = o2 + H * IDX_DIM
            o4 = o3 + IDX_DIM
            w_ik = w[:, o3:o4]
            w_iw = jnp.pad(w[:, o4:], ((0, 0), (0, 128 - (w.shape[1] - o4))))
            w_cat = jnp.concatenate([w[:, :o3], w_ik, w_ik, w_iw], axis=1).astype(_BF)
            q, ckv, iq, ikk, iw = _in_sa(x, ng1, sc1, sh1, w_cat, row(sa_kv_g[j]))
            wuk = jnp.transpose(sa_w_uk[j], (1, 2, 0)).astype(_BF)
            wuv = jnp.transpose(sa_w_uv[j], (1, 0, 2)).astype(_BF)
            o = _sparse_attn(q, ckv, iq, ikk, iw, wuk, wuv, rel_table, bn)
            x = _out_proj(o, sa_w_out[j].astype(_BF), x, g1)
        else:
            u, v = _in_sg(x, ng1, sc1, sh1, sg_w_in[j].astype(_BF), row(sg_b_in[j]),
                          row(sg_ln_g[j]), row(sg_ln_b[j]))
            x = _sg_tail(u, v, sg_w_s[j], sg_b_s[j].T, sg_w_out[j].astype(_BF), row(sg_b_out[j]), x, g1)
        x = _ffn(x, row(norm_g[layer, 1]), sc2, sh2, g2, ff_w_up[layer].astype(_BF), ff_w_dw[layer],
                 row(ff_b_dw[layer]), ff_w_down[layer].astype(_BF), row(final_g), layer == depth - 1)
    return x
```

```python
import functools
import math

import jax
import jax.numpy as jnp
import numpy as np
from jax import lax
from jax.experimental import pallas as pl
from jax.experimental.pallas import tpu as pltpu

_BF = jnp.bfloat16
_F32 = jnp.float32
_I32 = jnp.int32

EPS = 1e-6
SUBLN_EPS = 1e-5
NEG = -1e30
CHUNK = 64
CHUNK_SHIFT = 6
N_HEADS = 8
HEAD_DIM = 128
DA_DIM = 64
SA_LATENT = 256
IDX_DIM = 64
TOPK = 256
N_BUCKETS = 32
MAX_DISTANCE = 128
SG_CHUNK = 128
SG_GROUPS = 8
CONV_WIDTH = 31
CONV_HALO = 32
CONV_ROWS = 32
SUBLANES = 8
FFN_HALO = 8
FFN_CHUNK = 256
FFN_DOWN_GROUP = 4
ATT_TILE = 512
DA_ROWS = 128
SA_ROWS = 256
NEAR_W = 768
VMEM_LIMIT = 56 * 1024 * 1024


def _dot(a, b):
    return jnp.dot(a, b, preferred_element_type=_F32)


def _dot_nt(a, b):
    return lax.dot_general(a, b, (((1,), (1,)), ((), ())), preferred_element_type=_F32)


def _rms_mod(x, g, sc, sh):
    ms = jnp.mean(x * x, axis=-1, keepdims=True)
    return (x * lax.rsqrt(ms + EPS) * g) * (1.0 + sc) + sh


def _layernorm(x, g, b):
    mu = jnp.mean(x, axis=-1, keepdims=True)
    xc = x - mu
    var = jnp.mean(xc * xc, axis=-1, keepdims=True)
    return xc * lax.rsqrt(var + EPS) * g + b


def _silu(x):
    return x * jax.nn.sigmoid(x)


def _params(sem):
    return pltpu.CompilerParams(dimension_semantics=sem, vmem_limit_bytes=VMEM_LIMIT)


def _full(shape):
    n = len(shape)
    return pl.BlockSpec(shape, lambda *_: (0,) * n, pipeline_mode=pl.Buffered(1))


def _ada_kernel(c_ref, w_ref, b_ref, o_ref):
    ca = _silu(c_ref[...]).astype(_BF)
    o_ref[0] = _dot(ca, w_ref[0].astype(_BF)) + b_ref[0]


def _ada(c, ada_w, ada_b):
    L, D, N = ada_w.shape
    B = c.shape[0]
    tn = N // 4
    return pl.pallas_call(
        _ada_kernel,
        grid=(L, N // tn),
        in_specs=[pl.BlockSpec((B, D), lambda l, j: (0, 0)),
                  pl.BlockSpec((1, D, tn), lambda l, j: (l, 0, j)),
                  pl.BlockSpec((1, 1, tn), lambda l, j: (l, 0, j))],
        out_specs=pl.BlockSpec((1, B, tn), lambda l, j: (l, 0, j)),
        out_shape=jax.ShapeDtypeStruct((L, B, N), _F32),
        compiler_params=_params(("arbitrary", "arbitrary")),
        name="ada",
    )(c, ada_w, ada_b.reshape(L, 1, N))


def _bias_kernel(tbl_ref, bn_ref):
    h = pl.program_id(0)
    shape = bn_ref.shape[1:]
    i = lax.broadcasted_iota(_I32, shape, 0)
    j = lax.broadcasted_iota(_I32, shape, 1) - (NEAR_W - ATT_TILE)
    nb = N_BUCKETS // 2
    max_exact = nb // 2
    rel = j - i
    ret = jnp.where(rel > 0, nb, 0)
    n = jnp.abs(rel)
    nf = jnp.maximum(n, 1).astype(_F32)
    large = max_exact + (jnp.log(nf / max_exact) / math.log(MAX_DISTANCE / max_exact)
                         * (nb - max_exact)).astype(_I32)
    large = jnp.minimum(large, nb - 1)
    bucket = ret + jnp.where(n < max_exact, n, large)
    out = jnp.zeros(shape, _F32)
    for bk in range(N_BUCKETS):
        out = jnp.where(bucket == bk, tbl_ref[bk, h], out)
    visible = (j >> CHUNK_SHIFT) <= (i >> CHUNK_SHIFT)
    bn_ref[0] = jnp.where(visible, out, NEG)


def _bias_tiles(rel_table):
    H = rel_table.shape[1]
    return pl.pallas_call(
        _bias_kernel,
        grid=(H,),
        in_specs=[pl.BlockSpec(memory_space=pltpu.SMEM)],
        out_specs=pl.BlockSpec((1, ATT_TILE, NEAR_W), lambda h: (h, 0, 0)),
        out_shape=jax.ShapeDtypeStruct((H, ATT_TILE, NEAR_W), _F32),
        compiler_params=_params(("arbitrary",)),
        name="bias_tiles",
    )(rel_table)


def _in_specs_common(tm, D):
    return [pl.BlockSpec((1, tm, D), lambda b, i: (b, i, 0)),
            pl.BlockSpec((1, D), lambda b, i: (0, 0)),
            pl.BlockSpec((1, 1, D), lambda b, i: (b, 0, 0)),
            pl.BlockSpec((1, 1, D), lambda b, i: (b, 0, 0))]


def _in_da_kernel(x_ref, g_ref, sc_ref, sh_ref, w_ref, o_ref):
    hn = _rms_mod(x_ref[0], g_ref[...], sc_ref[0], sh_ref[0]).astype(_BF)
    N = w_ref.shape[1]
    tn = 512
    for c in range(N // tn):
        o_ref[0, :, c * tn:(c + 1) * tn] = _dot(hn, w_ref[:, c * tn:(c + 1) * tn]).astype(_BF)


def _in_da(x, g, sc, sh, w):
    B, S, D = x.shape
    N = w.shape[1]
    tm = 512
    return pl.pallas_call(
        _in_da_kernel,
        grid=(B, S // tm),
        in_specs=_in_specs_common(tm, D) + [_full((D, N))],
        out_specs=pl.BlockSpec((1, tm, N), lambda b, i: (b, i, 0)),
        out_shape=jax.ShapeDtypeStruct((B, S, N), _BF),
        compiler_params=_params(("parallel", "parallel")),
        name="in_da",
    )(x, g, sc, sh, w)


def _in_cv_kernel(x_ref, g_ref, sc_ref, sh_ref, w_ref, b_ref, o_ref):
    hn = _rms_mod(x_ref[0], g_ref[...], sc_ref[0], sh_ref[0]).astype(_BF)
    D = o_ref.shape[2]
    tn = 512
    n = D // tn
    lo = lambda c: slice(c * tn, (c + 1) * tn)
    hi = lambda c: slice(D + c * tn, D + (c + 1) * tn)
    dots = lambda c: (_dot(hn, w_ref[:, lo(c)]), _dot(hn, w_ref[:, hi(c)]))
    nxt = dots(0)
    for c in range(n):
        cur = nxt
        if c + 1 < n:
            nxt = dots(c + 1)
        o_ref[0, :, lo(c)] = (cur[0] + b_ref[:, lo(c)]) * jax.nn.sigmoid(cur[1] + b_ref[:, hi(c)])


def _in_cv(x, g, sc, sh, w, b):
    B, S, D = x.shape
    N = w.shape[1]
    tm = 512
    return pl.pallas_call(
        _in_cv_kernel,
        grid=(B, S // tm),
        in_specs=_in_specs_common(tm, D) + [_full((D, N)), _full((1, N))],
        out_specs=pl.BlockSpec((1, tm, D), lambda b, i: (b, i, 0)),
        out_shape=jax.ShapeDtypeStruct((B, S, D), _F32),
        compiler_params=_params(("parallel", "parallel")),
        name="in_cv",
    )(x, g, sc, sh, w, b)


def _in_sa_kernel(x_ref, g_ref, sc_ref, sh_ref, w_ref, kvg_ref,
                  q_ref, ckv_ref, iq_ref, ikk_ref, iw_ref):
    hn = _rms_mod(x_ref[0], g_ref[...], sc_ref[0], sh_ref[0]).astype(_BF)
    H = q_ref.shape[1]
    for c in range(H // 2):
        r = _dot(hn, w_ref[:, c * 256:(c + 1) * 256])
        q_ref[0, 2 * c] = r[:, :HEAD_DIM].astype(_BF)
        q_ref[0, 2 * c + 1] = r[:, HEAD_DIM:].astype(_BF)
    o = H * HEAD_DIM
    ckv = _dot(hn, w_ref[:, o:o + SA_LATENT])
    ms = jnp.mean(ckv * ckv, axis=-1, keepdims=True)
    ckv_ref[0] = (ckv * lax.rsqrt(ms + EPS) * kvg_ref[...]).astype(_BF)
    o += SA_LATENT
    iq_ref[0] = (_dot(hn, w_ref[:, o:o + 512]) * (IDX_DIM ** -0.5)).astype(_BF)
    o += 512
    r = _dot(hn, w_ref[:, o:o + 256])
    ikk_ref[0] = r[:, :128].astype(_BF)
    iw_ref[0] = r[:, 128:] * (N_HEADS ** -0.5)


def _in_sa(x, g, sc, sh, w, kv_g):
    B, S, D = x.shape
    N = w.shape[1]
    H = N_HEADS
    tm = 512
    row = lambda n: pl.BlockSpec((1, tm, n), lambda b, i: (b, i, 0))
    return pl.pallas_call(
        _in_sa_kernel,
        grid=(B, S // tm),
        in_specs=_in_specs_common(tm, D) + [_full((D, N)), _full((1, SA_LATENT))],
        out_specs=[pl.BlockSpec((1, H, tm, HEAD_DIM), lambda b, i: (b, 0, i, 0)),
                   row(SA_LATENT), row(512), row(128), row(128)],
        out_shape=[jax.ShapeDtypeStruct((B, H, S, HEAD_DIM), _BF),
                   jax.ShapeDtypeStruct((B, S, SA_LATENT), _BF),
                   jax.ShapeDtypeStruct((B, S, 512), _BF),
                   jax.ShapeDtypeStruct((B, S, 128), _BF),
                   jax.ShapeDtypeStruct((B, S, 128), _F32)],
        compiler_params=_params(("parallel", "parallel")),
        name="in_sa",
    )(x, g, sc, sh, w, kv_g)


def _in_sg_kernel(x_ref, g_ref, sc_ref, sh_ref, w_ref, b_ref, lg_ref, lb_ref, u_ref, v_ref, a_scr):
    hn = _rms_mod(x_ref[0], g_ref[...], sc_ref[0], sh_ref[0]).astype(_BF)
    W = u_ref.shape[2]
    tn = 512
    n = W // tn
    lo = lambda c: slice(c * tn, (c + 1) * tn)
    hi = lambda c: slice(W + c * tn, W + (c + 1) * tn)
    cols = [hi(c) for c in range(n)] + [lo(c) for c in range(n)]
    nxt = _dot(hn, w_ref[:, cols[0]])
    for t in range(2 * n):
        cur = nxt
        if t + 1 < 2 * n:
            nxt = _dot(hn, w_ref[:, cols[t + 1]])
        act = jax.nn.gelu(cur + b_ref[:, cols[t]], approximate=True)
        if t < n:
            a_scr[:, lo(t)] = act
            if t == n - 1:
                v_ref[0] = _layernorm(a_scr[...], lg_ref[...], lb_ref[...]).astype(_BF)
        else:
            u_ref[0, :, lo(t - n)] = act.astype(u_ref.dtype)


def _in_sg(x, g, sc, sh, w, b, ln_g, ln_b):
    B, S, D = x.shape
    N = w.shape[1]
    W = N // 2
    tm = 512
    row = lambda: pl.BlockSpec((1, tm, W), lambda b, i: (b, i, 0))
    return pl.pallas_call(
        _in_sg_kernel,
        grid=(B, S // tm),
        in_specs=_in_specs_common(tm, D) + [_full((D, N)), _full((1, N)), _full((1, W)), _full((1, W))],
        out_specs=[row(), row()],
        out_shape=[jax.ShapeDtypeStruct((B, S, W), _BF), jax.ShapeDtypeStruct((B, S, W), _BF)],
        scratch_shapes=[pltpu.VMEM((tm, W), _F32)],
        compiler_params=_params(("parallel", "parallel")),
        name="in_sg",
    )(x, g, sc, sh, w, b, ln_g, ln_b)


def _attend(qs, k_ref, v_ref, W, fars, near_biases, block_rows, madd_ref=None):
    nw = min(W, NEAR_W)
    fw = W - nw
    lo = NEAR_W - nw
    n_rows = qs[0].shape[0]
    blocks = [slice(r, r + block_rows) for r in range(0, n_rows, block_rows)]
    logits = []
    for q, far, near_bias in zip(qs, fars, near_biases):
        for rows in blocks:
            we = W - (n_rows - rows.stop)
            s_n = _dot_nt(q[rows], k_ref[0, fw:we, :]) + near_bias(rows, lo, lo + we - fw)
            s_f = None
            if fw:
                s_f = _dot_nt(q[rows], k_ref[0, 0:fw, :]) + far
            if madd_ref is not None:
                s_n = s_n + madd_ref[rows, fw:we]
                if fw:
                    s_f = s_f + madd_ref[rows, 0:fw]
            logits.append((s_n, s_f, we))
    outs = []
    for s_n, s_f, we in logits:
        m = jnp.max(s_n, axis=1, keepdims=True)
        if fw:
            m = jnp.maximum(m, jnp.max(s_f, axis=1, keepdims=True))
            p_f = jnp.exp(s_f - m)
        p_n = jnp.exp(s_n - m)
        l = jnp.sum(p_n, axis=1, keepdims=True)
        o = _dot(p_n.astype(_BF), v_ref[0, fw:we, :])
        if fw:
            l = l + jnp.sum(p_f, axis=1, keepdims=True)
            o = o + _dot(p_f.astype(_BF), v_ref[0, 0:fw, :])
        outs.append(o * (1.0 / l))
    nb = len(blocks)
    return [jnp.concatenate(outs[i * nb:(i + 1) * nb], axis=0) for i in range(len(qs))]


def _da_kernel(tbl_ref, q_ref, k_ref, v_ref, bn_ref, lam_ref, g_ref, o_ref, *, lambda_init):
    h = pl.program_id(1)
    a = pl.program_id(2)
    T = q_ref.shape[1]
    S = k_ref.shape[1]
    lane = lax.broadcasted_iota(_I32, (T, HEAD_DIM), 1)
    q = q_ref[0] * (DA_DIM ** -0.5)
    zero = jnp.zeros_like(q)
    q1 = jnp.where(lane < DA_DIM, q, zero)
    q2 = jnp.where(lane >= DA_DIM, q, zero)
    far = tbl_ref[N_BUCKETS // 2 - 1, h]
    lam = lam_ref[...]
    lam_full = (jnp.exp(jnp.sum(lam[0:1] * lam[1:2], axis=1, keepdims=True))
                - jnp.exp(jnp.sum(lam[2:3] * lam[3:4], axis=1, keepdims=True)) + lambda_init)
    near_bias = lambda rows, lo, hi: bn_ref[0, rows, lo:hi]

    for br in range(S // T):
        @pl.when(a == br)
        def _(br=br):
            W = (br + 1) * T
            o1, o2 = _attend([q1, q2], k_ref, v_ref, W, [far, far], [near_bias, near_bias], DA_ROWS)
            o = o1 - lam_full * o2
            ms = jnp.mean(o * o, axis=-1, keepdims=True)
            o = (o * lax.rsqrt(ms + SUBLN_EPS) * g_ref[...]) * (1.0 - lambda_init)
            o_ref[0] = o.astype(_BF)


def _diff_attn(qkv, rel_table, bn, lam, subln_g, lambda_init):
    B, S, _ = qkv.shape
    H = N_HEADS
    T = ATT_TILE
    kern = functools.partial(_da_kernel, lambda_init=lambda_init)
    return pl.pallas_call(
        kern,
        grid=(B, H, S // T),
        in_specs=[pl.BlockSpec(memory_space=pltpu.SMEM),
                  pl.BlockSpec((1, T, HEAD_DIM), lambda b, h, i: (b, i, h)),
                  pl.BlockSpec((1, S, HEAD_DIM), lambda b, h, i: (b, 0, H + h)),
                  pl.BlockSpec((1, S, HEAD_DIM), lambda b, h, i: (b, 0, 2 * H + h)),
                  pl.BlockSpec((1, T, NEAR_W), lambda b, h, i: (h, 0, 0)),
                  pl.BlockSpec((4, DA_DIM), lambda b, h, i: (0, 0)),
                  pl.BlockSpec((1, HEAD_DIM), lambda b, h, i: (0, 0))],
        out_specs=pl.BlockSpec((1, T, HEAD_DIM), lambda b, h, i: (b, i, h)),
        out_shape=jax.ShapeDtypeStruct((B, S, H * HEAD_DIM), _BF),
        compiler_params=_params(("parallel", "parallel", "arbitrary")),
        name="diff_attn",
    )(rel_table, qkv, qkv, qkv, bn, lam, subln_g)


_INT_MIN = np.int32(-2 ** 31)
_KEY_NEG_INF = np.int32(np.array(-np.inf, np.float32).view(np.int32) ^ np.int32(0x7FFFFFFF))
SEARCH_GROUPS = 4
HEADS_PER_STEP = 2


def _sa_branch(W, tbl_ref, q_ref, ckv_ref, iw_ref, ikk_ref, wuk_ref, wuv_ref, bn_ref,
               iqm_ref, keys_ref, madd_ref, oh_ref, last_ref):
    T = iw_ref.shape[1]
    H = q_ref.shape[1]
    iw = iw_ref[0]

    for cb in range(W // T):
        cols = slice(cb * T, (cb + 1) * T)
        ikt = ikk_ref[0, cols, :]
        score = jnp.zeros((T, T), _F32)
        for hh in range(H):
            score = score + jnp.maximum(_dot_nt(iqm_ref[hh], ikt), 0.0) * iw[:, hh:hh + 1]
        score = jnp.where(score == 0.0, 0.0, score)
        if cb == W // T - 1:
            row = lax.broadcasted_iota(_I32, (T, T), 0)
            col = lax.broadcasted_iota(_I32, (T, T), 1)
            score = jnp.where((col >> CHUNK_SHIFT) <= (row >> CHUNK_SHIFT), score, -jnp.inf)
        bits = lax.bitcast_convert_type(score, _I32)
        keys_ref[:, cols] = bits ^ ((bits >> 31) & np.int32(0x7FFFFFFF))

    def count(pred, rows=slice(None)):
        return jnp.sum(jnp.where(pred(keys_ref[rows, 0:W]), 1.0, 0.0), axis=1, keepdims=True)

    R = T // SEARCH_GROUPS

    def bit_body(it, thrs):
        bit = lax.shift_left(np.int32(1), np.int32(31) - it)
        out = []
        for g in range(SEARCH_GROUPS):
            cand_u = thrs[g] | bit
            cand = cand_u ^ _INT_MIN
            cnt = count(lambda k: k >= cand, slice(g * R, (g + 1) * R))
            out.append(jnp.where(cnt >= float(TOPK), cand_u, thrs[g]))
        return tuple(out)

    thrs = lax.fori_loop(0, 32, bit_body, tuple(jnp.zeros((R, 1), _I32) for _ in range(SEARCH_GROUPS)))
    thr = jnp.concatenate(thrs, axis=0) ^ _INT_MIN

    cnt_gt = count(lambda k: k > thr)
    cnt_ge = count(lambda k: k >= thr)
    need = float(TOPK) - cnt_gt
    last_ref[...] = jnp.full((T, 1), 2 ** 30, _I32)
    tied = jnp.logical_and(cnt_ge - cnt_gt > need, thr > _KEY_NEG_INF)

    @pl.when(jnp.max(jnp.where(tied, 1.0, 0.0)) > 0.0)
    def _():
        col = lax.broadcasted_iota(_I32, (T, W), 1)

        def idx_body(it, v):
            cand = v | lax.shift_left(np.int32(1), np.int32(10) - it)
            below = count(lambda k: jnp.logical_and(k == thr, col < cand))
            return jnp.where(below < need, cand, v)
        last_ref[...] = lax.fori_loop(0, 11, idx_body, jnp.zeros((T, 1), _I32))

    last = last_ref[...]
    k = keys_ref[:, 0:W]
    col = lax.broadcasted_iota(_I32, (T, W), 1)
    sel = jnp.logical_or(k > thr, jnp.logical_and(k == thr, col <= last))
    madd_ref[:, 0:W] = jnp.where(sel, 0.0, NEG)

    def head_body(hp, carry):
        hs = [hp * HEADS_PER_STEP + u for u in range(HEADS_PER_STEP)]
        qlats = [(_dot(q_ref[0, h], wuk_ref[h]) * (HEAD_DIM ** -0.5)).astype(_BF) for h in hs]
        olats = _attend(qlats, ckv_ref, ckv_ref, W,
                        [tbl_ref[N_BUCKETS // 2 - 1, h] for h in hs],
                        [functools.partial(lambda h, rows, lo, hi: bn_ref[h, rows, lo:hi], h) for h in hs],
                        SA_ROWS, madd_ref)
        for h, olat in zip(hs, olats):
            oh_ref[h] = _dot(olat.astype(_BF), wuv_ref[h])
        return carry

    lax.fori_loop(0, H // HEADS_PER_STEP, head_body, 0)


def _sa_kernel(tbl_ref, q_ref, ckv_ref, iq_ref, ikk_ref, iw_ref, wuk_ref, wuv_ref, bn_ref,
               o_ref, iqm_ref, keys_ref, madd_ref, oh_ref, last_ref):
    a = pl.program_id(1)
    T = iq_ref.shape[1]
    S = ckv_ref.shape[1]
    H = q_ref.shape[1]
    lane = lax.broadcasted_iota(_I32, (T, 128), 1)
    for p in range(H // 2):
        iqp = iq_ref[0, :, p * 128:(p + 1) * 128]
        zero = jnp.zeros_like(iqp)
        iqm_ref[2 * p] = jnp.where(lane < IDX_DIM, iqp, zero)
        iqm_ref[2 * p + 1] = jnp.where(lane >= IDX_DIM, iqp, zero)

    for br in range(S // T):
        @pl.when(a == br)
        def _(br=br):
            _sa_branch((br + 1) * T, tbl_ref, q_ref, ckv_ref, iw_ref, ikk_ref, wuk_ref, wuv_ref, bn_ref,
                       iqm_ref, keys_ref, madd_ref, oh_ref, last_ref)

    for h in range(H):
        o_ref[0, :, h * HEAD_DIM:(h + 1) * HEAD_DIM] = oh_ref[h].astype(_BF)


def _sparse_attn(q, ckv, iq, ikk, iw, wuk, wuv, rel_table, bn):
    B, H, S, _ = q.shape
    T = ATT_TILE
    return pl.pallas_call(
        _sa_kernel,
        grid=(B, S // T),
        in_specs=[pl.BlockSpec(memory_space=pltpu.SMEM),
                  pl.BlockSpec((1, H, T, HEAD_DIM), lambda b, i: (b, 0, i, 0)),
                  pl.BlockSpec((1, S, SA_LATENT), lambda b, i: (b, 0, 0)),
                  pl.BlockSpec((1, T, 512), lambda b, i: (b, i, 0)),
                  pl.BlockSpec((1, S, 128), lambda b, i: (b, 0, 0)),
                  pl.BlockSpec((1, T, 128), lambda b, i: (b, i, 0)),
                  _full((H, HEAD_DIM, SA_LATENT)),
                  _full((H, SA_LATENT, HEAD_DIM)),
                  _full((H, T, NEAR_W))],
        out_specs=pl.BlockSpec((1, T, H * HEAD_DIM), lambda b, i: (b, i, 0)),
        out_shape=jax.ShapeDtypeStruct((B, S, H * HEAD_DIM), _BF),
        scratch_shapes=[pltpu.VMEM((H, T, 128), _BF),
                        pltpu.VMEM((T, S), _I32),
                        pltpu.VMEM((T, S), _F32),
                        pltpu.VMEM((H, T, HEAD_DIM), _F32),
                        pltpu.VMEM((T, 1), _I32)],
        compiler_params=_params(("parallel", "arbitrary")),
        name="sparse_attn",
    )(rel_table, q, ckv, iq, ikk, iw, wuk, wuv, bn)


def _cv_kernel(a_ref, ah_ref, wdw_ref, bdw_ref, lg_ref, lb_ref, w2_ref, b2_ref, x_ref, g1_ref, o_ref,
               ext_ref, y_ref):
    i = pl.program_id(1)
    tm = a_ref.shape[1]
    n_ext = CONV_HALO + tm
    ext_ref[0, CONV_HALO:n_ext] = a_ref[0]
    halo = ah_ref[0]
    ext_ref[0, 0:CONV_HALO] = jnp.where(i > 0, halo, jnp.zeros_like(halo))
    for r in range(1, SUBLANES):
        ext_ref[r, 0:n_ext - SUBLANES] = ext_ref[0, r:r + n_ext - SUBLANES]
    rb = CONV_ROWS
    base = CONV_HALO - (CONV_WIDTH - 1)

    def block(blk, carry):
        row0 = pl.multiple_of(blk * rb, rb)
        acc = jnp.broadcast_to(bdw_ref[...], (rb, bdw_ref.shape[1]))
        for k in range(CONV_WIDTH):
            r = (base + k) % SUBLANES
            al = base + k - r
            acc = acc + wdw_ref[k:k + 1, :] * ext_ref[r, pl.ds(row0 + al, rb), :]
        y_ref[pl.ds(row0, rb)] = _silu(_layernorm(acc, lg_ref[...], lb_ref[...])).astype(_BF)
        return carry

    lax.fori_loop(0, tm // rb, block, 0, unroll=2)
    y = _dot(y_ref[...], w2_ref[...]) + b2_ref[...]
    o_ref[0] = x_ref[0] + g1_ref[0] * y


def _cv_tail(a, w_dw, b_dw, ln_g, ln_b, w2, b2, x, g1):
    B, S, D = a.shape
    tm = 512
    hb = tm // CONV_HALO
    vec = lambda: _full((1, D))
    return pl.pallas_call(
        _cv_kernel,
        grid=(B, S // tm),
        in_specs=[pl.BlockSpec((1, tm, D), lambda b, i: (b, i, 0)),
                  pl.BlockSpec((1, CONV_HALO, D), lambda b, i: (b, jnp.maximum(i * hb - 1, 0), 0)),
                  _full((CONV_WIDTH, D)), vec(), vec(), vec(), _full((D, D)), vec(),
                  pl.BlockSpec((1, tm, D), lambda b, i: (b, i, 0)),
                  pl.BlockSpec((1, 1, D), lambda b, i: (b, 0, 0))],
        out_specs=pl.BlockSpec((1, tm, D), lambda b, i: (b, i, 0)),
        out_shape=jax.ShapeDtypeStruct((B, S, D), _F32),
        scratch_shapes=[pltpu.VMEM((SUBLANES, CONV_HALO + tm, D), _F32), pltpu.VMEM((tm, D), _BF)],
        compiler_params=_params(("parallel", "parallel")),
        name="cv_tail",
    )(a, a, w_dw, b_dw, ln_g, ln_b, w2, b2, x, g1)


def _sg_kernel(u_ref, v_ref, ws_ref, bs_ref, wo_ref, bo_ref, x_ref, g1_ref, o_ref, z_ref):
    tm = u_ref.shape[1]
    G = ws_ref.shape[0]
    C = SG_CHUNK
    gw = u_ref.shape[2] // G
    t = lax.broadcasted_iota(_I32, (C, C), 0)
    s = lax.broadcasted_iota(_I32, (C, C), 1)
    for g in range(G):
        wg = jnp.where(t >= s, ws_ref[g], 0.0).astype(_BF)
        bcol = bs_ref[:, g:g + 1]
        cols = slice(g * gw, (g + 1) * gw)
        for n in range(tm // C):
            rows = slice(n * C, (n + 1) * C)
            sv = _dot(wg, v_ref[0, rows, cols]) + bcol
            z_ref[rows, cols] = (u_ref[0, rows, cols] * sv).astype(_BF)
    y = _dot(z_ref[...], wo_ref[...]) + bo_ref[...]
    o_ref[0] = x_ref[0] + g1_ref[0] * y


def _sg_tail(u, v, w_s, b_s_t, w_out, b_out, x, g1):
    B, S, W = u.shape
    D = x.shape[2]
    G = w_s.shape[0]
    tm = 256
    return pl.pallas_call(
        _sg_kernel,
        grid=(B, S // tm),
        in_specs=[pl.BlockSpec((1, tm, W), lambda b, i: (b, i, 0)),
                  pl.BlockSpec((1, tm, W), lambda b, i: (b, i, 0)),
                  _full((G, SG_CHUNK, SG_CHUNK)), _full((SG_CHUNK, G)),
                  _full((W, D)), _full((1, D)),
                  pl.BlockSpec((1, tm, D), lambda b, i: (b, i, 0)),
                  pl.BlockSpec((1, 1, D), lambda b, i: (b, 0, 0))],
        out_specs=pl.BlockSpec((1, tm, D), lambda b, i: (b, i, 0)),
        out_shape=jax.ShapeDtypeStruct((B, S, D), _F32),
        scratch_shapes=[pltpu.VMEM((tm, W), _BF)],
        compiler_params=_params(("parallel", "parallel")),
        name="sg_tail",
    )(u, v, w_s, b_s_t, w_out, b_out, x, g1)


def _out_kernel(a_ref, w_ref, x_ref, g1_ref, o_ref):
    o_ref[0] = x_ref[0] + g1_ref[0] * _dot(a_ref[0], w_ref[...])


def _out_proj(a, w, x, g1):
    B, S, D = x.shape
    K = a.shape[2]
    tm = 512
    return pl.pallas_call(
        _out_kernel,
        grid=(B, S // tm),
        in_specs=[pl.BlockSpec((1, tm, K), lambda b, i: (b, i, 0)),
                  _full((K, D)),
                  pl.BlockSpec((1, tm, D), lambda b, i: (b, i, 0)),
                  pl.BlockSpec((1, 1, D), lambda b, i: (b, 0, 0))],
        out_specs=pl.BlockSpec((1, tm, D), lambda b, i: (b, i, 0)),
        out_shape=jax.ShapeDtypeStruct((B, S, D), _F32),
        compiler_params=_params(("parallel", "parallel")),
        name="out_proj",
    )(a, w, x, g1)


def _ffn_kernel(x_ref, xh_ref, g_ref, sc_ref, sh_ref, g2_ref, wup_ref, wdw_ref, bdw_ref, wdn_ref,
                fg_ref, o_ref, he_ref, act_ref, acc_ref, *, final):
    i = pl.program_id(1)
    tm = x_ref.shape[1]
    F = wdn_ref.shape[0]
    x = x_ref[0]
    he_ref[FFN_HALO:FFN_HALO + tm] = _rms_mod(x, g_ref[...], sc_ref[0], sh_ref[0]).astype(_BF)
    hh = _rms_mod(xh_ref[0], g_ref[...], sc_ref[0], sh_ref[0])
    he_ref[0:FFN_HALO] = jnp.where(i > 0, hh, jnp.zeros_like(hh)).astype(_BF)
    tf = FFN_CHUNK
    n_chunks = F // tf

    def up(c):
        he = he_ref[...]
        return (_dot(he, wup_ref[:, c * tf:(c + 1) * tf]), _dot(he, wup_ref[:, F + c * tf:F + (c + 1) * tf]))

    def conv(a, col0):
        cols = slice(col0, col0 + tf)
        w = wdw_ref[:, cols]
        return (a[FFN_HALO - 2:FFN_HALO - 2 + tm] * w[0:1]
                + a[FFN_HALO - 1:FFN_HALO - 1 + tm] * w[1:2]
                + a[FFN_HALO:FFN_HALO + tm] * w[2:3] + bdw_ref[:, cols])

    nxt = up(0)
    for c in range(n_chunks):
        cur = nxt
        if c + 1 < n_chunks:
            nxt = up(c + 1)
        act = _silu(conv(cur[0], c * tf)) * conv(cur[1], F + c * tf)
        act_ref[:, c * tf:(c + 1) * tf] = act.astype(_BF)
        if (c + 1) % FFN_DOWN_GROUP == 0 or c + 1 == n_chunks:
            lo = (c // FFN_DOWN_GROUP) * FFN_DOWN_GROUP * tf
            part = _dot(act_ref[:, lo:(c + 1) * tf], wdn_ref[lo:(c + 1) * tf, :])
            if lo == 0:
                acc_ref[...] = part
            else:
                acc_ref[...] += part
    xn = x + g2_ref[0] * acc_ref[...]
    if final:
        ms = jnp.mean(xn * xn, axis=-1, keepdims=True)
        xn = xn * lax.rsqrt(ms + EPS) * fg_ref[...]
    o_ref[0] = xn


def _ffn(x, g, sc, sh, g2, w_up, w_dw, b_dw, w_down, final_g, final):
    B, S, D = x.shape
    F = w_down.shape[0]
    tm = 512
    hb = tm // FFN_HALO
    kern = functools.partial(_ffn_kernel, final=final)
    mod = lambda: pl.BlockSpec((1, 1, D), lambda b, i: (b, 0, 0))
    return pl.pallas_call(
        kern,
        grid=(B, S // tm),
        in_specs=[pl.BlockSpec((1, tm, D), lambda b, i: (b, i, 0)),
                  pl.BlockSpec((1, FFN_HALO, D), lambda b, i: (b, jnp.maximum(i * hb - 1, 0), 0)),
                  _full((1, D)), mod(), mod(), mod(),
                  _full((D, 2 * F)), _full((3, 2 * F)), _full((1, 2 * F)), _full((F, D)),
                  _full((1, D))],
        out_specs=pl.BlockSpec((1, tm, D), lambda b, i: (b, i, 0)),
        out_shape=jax.ShapeDtypeStruct((B, S, D), _F32),
        scratch_shapes=[pltpu.VMEM((FFN_HALO + tm, D), _BF), pltpu.VMEM((tm, F), _BF),
                        pltpu.VMEM((tm, D), _F32)],
        compiler_params=_params(("parallel", "parallel")),
        name="ffn",
    )(x, x, g, sc, sh, g2, w_up, w_dw, b_dw, w_down, final_g)


def kernel(x, c, rel_table, ada_w, ada_b, norm_g, final_g, da_w_in, da_lam, da_subln_g, da_w_out, cv_w_pw1, cv_b_pw1, cv_w_dw, cv_b_dw, cv_ln_g, cv_ln_b, cv_w_pw2, cv_b_pw2, sa_w_in, sa_kv_g, sa_w_uk, sa_w_uv, sa_w_out, sg_w_in, sg_b_in, sg_ln_g, sg_ln_b, sg_w_s, sg_b_s, sg_w_out, sg_b_out, ff_w_up, ff_w_dw, ff_b_dw, ff_w_down):
    B, S, D = x.shape
    depth = ada_w.shape[0]
    n_mixers = 4
    H = N_HEADS
    mods = _ada(c, ada_w, ada_b)
    bn = _bias_tiles(rel_table)
    row = lambda v: v.reshape(1, -1)

    for layer in range(depth):
        kind = layer % n_mixers
        j = layer // n_mixers
        sh1, sc1, g1, sh2, sc2, g2 = [m.reshape(B, 1, D) for m in jnp.split(mods[layer], 6, axis=-1)]
        ng1 = row(norm_g[layer, 0])
        if kind == 0:
            lambda_init = 0.8 - 0.6 * math.exp(-0.3 * layer)
            qkv = _in_da(x, ng1, sc1, sh1, da_w_in[j].astype(_BF))
            o = _diff_attn(qkv, rel_table, bn, da_lam[j], row(da_subln_g[j]), lambda_init)
            x = _out_proj(o, da_w_out[j].astype(_BF), x, g1)
        elif kind == 1:
            a = _in_cv(x, ng1, sc1, sh1, cv_w_pw1[j].astype(_BF), row(cv_b_pw1[j]))
            x = _cv_tail(a, cv_w_dw[j], row(cv_b_dw[j]), row(cv_ln_g[j]), row(cv_ln_b[j]),
                         cv_w_pw2[j].astype(_BF), row(cv_b_pw2[j]), x, g1)
        elif kind == 2:
            w = sa_w_in[j]
            o1 = H * HEAD_DIM
            o2 = o1 + SA_LATENT
            o3 = o2 + H * IDX_DIM
            o4 = o3 + IDX_DIM
            w_ik = w[:, o3:o4]
            w_iw = jnp.pad(w[:, o4:], ((0, 0), (0, 128 - (w.shape[1] - o4))))
            w_cat = jnp.concatenate([w[:, :o3], w_ik, w_ik, w_iw], axis=1).astype(_BF)
            q, ckv, iq, ikk, iw = _in_sa(x, ng1, sc1, sh1, w_cat, row(sa_kv_g[j]))
            wuk = jnp.transpose(sa_w_uk[j], (1, 2, 0)).astype(_BF)
            wuv = jnp.transpose(sa_w_uv[j], (1, 0, 2)).astype(_BF)
            o = _sparse_attn(q, ckv, iq, ikk, iw, wuk, wuv, rel_table, bn)
            x = _out_proj(o, sa_w_out[j].astype(_BF), x, g1)
        else:
            u, v = _in_sg(x, ng1, sc1, sh1, sg_w_in[j].astype(_BF), row(sg_b_in[j]),
                          row(sg_ln_g[j]), row(sg_ln_b[j]))
            x = _sg_tail(u, v, sg_w_s[j], sg_b_s[j].T, sg_w_out[j].astype(_BF), row(sg_b_out[j]), x, g1)
        x = _ffn(x, row(norm_g[layer, 1]), sc2, sh2, g2, ff_w_up[layer].astype(_BF), ff_w_dw[layer],
                 row(ff_b_dw[layer]), ff_w_down[layer].astype(_BF), row(final_g), layer == depth - 1)
    return x
```

```python
import functools
import math

import jax
import jax.numpy as jnp
import numpy as np
from jax import lax
from jax.experimental import pallas as pl
from jax.experimental.pallas import tpu as pltpu

_BF = jnp.bfloat16
_F32 = jnp.float32
_I32 = jnp.int32

EPS = 1e-6
SUBLN_EPS = 1e-5
NEG = -1e30
CHUNK = 64
CHUNK_SHIFT = 6
N_HEADS = 8
HEAD_DIM = 128
DA_DIM = 64
SA_LATENT = 256
IDX_DIM = 64
TOPK = 256
N_BUCKETS = 32
MAX_DISTANCE = 128
SG_CHUNK = 128
SG_GROUPS = 8
CONV_WIDTH = 31
CONV_HALO = 32
CONV_ROWS = 32
SUBLANES = 8
FFN_HALO = 8
FFN_CHUNK = 256
FFN_DOWN_GROUP = 4
ATT_TILE = 512
DA_ROWS = 128
SA_ROWS = 256
NEAR_W = 768
VMEM_LIMIT = 56 * 1024 * 1024


def _dot(a, b):
    return jnp.dot(a, b, preferred_element_type=_F32)


def _dot_nt(a, b):
    return lax.dot_general(a, b, (((1,), (1,)), ((), ())), preferred_element_type=_F32)


def _rms_mod(x, g, sc, sh):
    ms = jnp.mean(x * x, axis=-1, keepdims=True)
    return (x * lax.rsqrt(ms + EPS) * g) * (1.0 + sc) + sh


def _layernorm(x, g, b):
    mu = jnp.mean(x, axis=-1, keepdims=True)
    xc = x - mu
    var = jnp.mean(xc * xc, axis=-1, keepdims=True)
    return xc * lax.rsqrt(var + EPS) * g + b


def _silu(x):
    return x * jax.nn.sigmoid(x)


def _params(sem):
    return pltpu.CompilerParams(dimension_semantics=sem, vmem_limit_bytes=VMEM_LIMIT)


def _full(shape):
    n = len(shape)
    return pl.BlockSpec(shape, lambda *_: (0,) * n, pipeline_mode=pl.Buffered(1))


def _ada_kernel(c_ref, w_ref, b_ref, o_ref):
    ca = _silu(c_ref[...]).astype(_BF)
    o_ref[0] = _dot(ca, w_ref[0].astype(_BF)) + b_ref[0]


def _ada(c, ada_w, ada_b):
    L, D, N = ada_w.shape
    B = c.shape[0]
    tn = N // 4
    return pl.pallas_call(
        _ada_kernel,
        grid=(L, N // tn),
        in_specs=[pl.BlockSpec((B, D), lambda l, j: (0, 0)),
                  pl.BlockSpec((1, D, tn), lambda l, j: (l, 0, j)),
                  pl.BlockSpec((1, 1, tn), lambda l, j: (l, 0, j))],
        out_specs=pl.BlockSpec((1, B, tn), lambda l, j: (l, 0, j)),
        out_shape=jax.ShapeDtypeStruct((L, B, N), _F32),
        compiler_params=_params(("arbitrary", "arbitrary")),
        name="ada",
    )(c, ada_w, ada_b.reshape(L, 1, N))


def _bias_kernel(tbl_ref, bn_ref):
    h = pl.program_id(0)
    shape = bn_ref.shape[1:]
    i = lax.broadcasted_iota(_I32, shape, 0)
    j = lax.broadcasted_iota(_I32, shape, 1) - (NEAR_W - ATT_TILE)
    nb = N_BUCKETS // 2
    max_exact = nb // 2
    rel = j - i
    ret = jnp.where(rel > 0, nb, 0)
    n = jnp.abs(rel)
    nf = jnp.maximum(n, 1).astype(_F32)
    large = max_exact + (jnp.log(nf / max_exact) / math.log(MAX_DISTANCE / max_exact)
                         * (nb - max_exact)).astype(_I32)
    large = jnp.minimum(large, nb - 1)
    bucket = ret + jnp.where(n < max_exact, n, large)
    out = jnp.zeros(shape, _F32)
    for bk in range(N_BUCKETS):
        out = jnp.where(bucket == bk, tbl_ref[bk, h], out)
    visible = (j >> CHUNK_SHIFT) <= (i >> CHUNK_SHIFT)
    bn_ref[0] = jnp.where(visible, out, NEG)


def _bias_tiles(rel_table):
    H = rel_table.shape[1]
    return pl.pallas_call(
        _bias_kernel,
        grid=(H,),
        in_specs=[pl.BlockSpec(memory_space=pltpu.SMEM)],
        out_specs=pl.BlockSpec((1, ATT_TILE, NEAR_W), lambda h: (h, 0, 0)),
        out_shape=jax.ShapeDtypeStruct((H, ATT_TILE, NEAR_W), _F32),
        compiler_params=_params(("arbitrary",)),
        name="bias_tiles",
    )(rel_table)


def _in_specs_common(tm, D):
    return [pl.BlockSpec((1, tm, D), lambda b, i: (b, i, 0)),
            pl.BlockSpec((1, D), lambda b, i: (0, 0)),
            pl.BlockSpec((1, 1, D), lambda b, i: (b, 0, 0)),
            pl.BlockSpec((1, 1, D), lambda b, i: (b, 0, 0))]


def _in_da_kernel(x_ref, g_ref, sc_ref, sh_ref, w_ref, o_ref):
    hn = _rms_mod(x_ref[0], g_ref[...], sc_ref[0], sh_ref[0]).astype(_BF)
    N = w_ref.shape[1]
    tn = 512
    for c in range(N // tn):
        o_ref[0, :, c * tn:(c + 1) * tn] = _dot(hn, w_ref[:, c * tn:(c + 1) * tn]).astype(_BF)


def _in_da(x, g, sc, sh, w):
    B, S, D = x.shape
    N = w.shape[1]
    tm = 512
    return pl.pallas_call(
        _in_da_kernel,
        grid=(B, S // tm),
        in_specs=_in_specs_common(tm, D) + [_full((D, N))],
        out_specs=pl.BlockSpec((1, tm, N), lambda b, i: (b, i, 0)),
        out_shape=jax.ShapeDtypeStruct((B, S, N), _BF),
        compiler_params=_params(("parallel", "parallel")),
        name="in_da",
    )(x, g, sc, sh, w)


def _in_sa_kernel(x_ref, g_ref, sc_ref, sh_ref, w_ref, kvg_ref,
                  q_ref, ckv_ref, iq_ref, ikk_ref, iw_ref):
    hn = _rms_mod(x_ref[0], g_ref[...], sc_ref[0], sh_ref[0]).astype(_BF)
    H = q_ref.shape[1]
    for c in range(H // 2):
        r = _dot(hn, w_ref[:, c * 256:(c + 1) * 256])
        q_ref[0, 2 * c] = r[:, :HEAD_DIM].astype(_BF)
        q_ref[0, 2 * c + 1] = r[:, HEAD_DIM:].astype(_BF)
    o = H * HEAD_DIM
    ckv = _dot(hn, w_ref[:, o:o + SA_LATENT])
    ms = jnp.mean(ckv * ckv, axis=-1, keepdims=True)
    ckv_ref[0] = (ckv * lax.rsqrt(ms + EPS) * kvg_ref[...]).astype(_BF)
    o += SA_LATENT
    iq_ref[0] = (_dot(hn, w_ref[:, o:o + 512]) * (IDX_DIM ** -0.5)).astype(_BF)
    o += 512
    r = _dot(hn, w_ref[:, o:o + 256])
    ikk_ref[0] = r[:, :128].astype(_BF)
    iw_ref[0] = r[:, 128:] * (N_HEADS ** -0.5)


def _in_sa(x, g, sc, sh, w, kv_g):
    B, S, D = x.shape
    N = w.shape[1]
    H = N_HEADS
    tm = 512
    row = lambda n: pl.BlockSpec((1, tm, n), lambda b, i: (b, i, 0))
    return pl.pallas_call(
        _in_sa_kernel,
        grid=(B, S // tm),
        in_specs=_in_specs_common(tm, D) + [_full((D, N)), _full((1, SA_LATENT))],
        out_specs=[pl.BlockSpec((1, H, tm, HEAD_DIM), lambda b, i: (b, 0, i, 0)),
                   row(SA_LATENT), row(512), row(128), row(128)],
        out_shape=[jax.ShapeDtypeStruct((B, H, S, HEAD_DIM), _BF),
                   jax.ShapeDtypeStruct((B, S, SA_LATENT), _BF),
                   jax.ShapeDtypeStruct((B, S, 512), _BF),
                   jax.ShapeDtypeStruct((B, S, 128), _BF),
                   jax.ShapeDtypeStruct((B, S, 128), _F32)],
        compiler_params=_params(("parallel", "parallel")),
        name="in_sa",
    )(x, g, sc, sh, w, kv_g)


def _sg_kernel(x_ref, g_ref, sc_ref, sh_ref, w_ref, b_ref, lg_ref, lb_ref, ws_ref, bs_ref, wo_ref, bo_ref,
               g1_ref, o_ref, a_scr, v_scr, z_scr):
    x = x_ref[0]
    hn = _rms_mod(x, g_ref[...], sc_ref[0], sh_ref[0]).astype(_BF)
    tm = x.shape[0]
    W = a_scr.shape[1]
    G = ws_ref.shape[0]
    C = SG_CHUNK
    gw = W // G
    tn = 512
    n = W // tn
    lo = lambda c: slice(c * tn, (c + 1) * tn)
    hi = lambda c: slice(W + c * tn, W + (c + 1) * tn)
    ti = lax.broadcasted_iota(_I32, (C, C), 0)
    si = lax.broadcasted_iota(_I32, (C, C), 1)
    cols = [hi(c) for c in range(n)] + [lo(c) for c in range(n)]
    nxt = _dot(hn, w_ref[:, cols[0]])
    for t in range(2 * n):
        cur = nxt
        if t + 1 < 2 * n:
            nxt = _dot(hn, w_ref[:, cols[t + 1]])
        act = jax.nn.gelu(cur + b_ref[:, cols[t]], approximate=True)
        if t < n:
            a_scr[:, lo(t)] = act
            if t == n - 1:
                v_scr[...] = _layernorm(a_scr[...], lg_ref[...], lb_ref[...]).astype(_BF)
            continue
        for gl in range(tn // gw):
            g = (t - n) * (tn // gw) + gl
            wg = jnp.where(ti >= si, ws_ref[g], 0.0).astype(_BF)
            bcol = bs_ref[:, g:g + 1]
            gcols = slice(g * gw, (g + 1) * gw)
            for r in range(tm // C):
                rows = slice(r * C, (r + 1) * C)
                sv = _dot(wg, v_scr[rows, gcols]) + bcol
                z_scr[rows, gcols] = (act[rows, gl * gw:(gl + 1) * gw] * sv).astype(_BF)
    y = _dot(z_scr[...], wo_ref[...]) + bo_ref[...]
    o_ref[0] = x + g1_ref[0] * y


def _sg_mixer(x, g, sc, sh, w, b, ln_g, ln_b, w_s, b_s_t, w_out, b_out, g1):
    B, S, D = x.shape
    N = w.shape[1]
    W = N // 2
    G = w_s.shape[0]
    tm = 512
    return pl.pallas_call(
        _sg_kernel,
        grid=(B, S // tm),
        in_specs=_in_specs_common(tm, D) + [_full((D, N)), _full((1, N)), _full((1, W)), _full((1, W)),
                                            _full((G, SG_CHUNK, SG_CHUNK)), _full((SG_CHUNK, G)),
                                            _full((W, D)), _full((1, D)),
                                            pl.BlockSpec((1, 1, D), lambda b, i: (b, 0, 0))],
        out_specs=pl.BlockSpec((1, tm, D), lambda b, i: (b, i, 0)),
        out_shape=jax.ShapeDtypeStruct((B, S, D), _F32),
        scratch_shapes=[pltpu.VMEM((tm, W), _F32), pltpu.VMEM((tm, W), _BF), pltpu.VMEM((tm, W), _BF)],
        compiler_params=_params(("parallel", "parallel")),
        name="sg_mixer",
    )(x, g, sc, sh, w, b, ln_g, ln_b, w_s, b_s_t, w_out, b_out, g1)


def _attend(qs, k_ref, v_ref, W, fars, near_biases, block_rows, madd_ref=None):
    nw = min(W, NEAR_W)
    fw = W - nw
    lo = NEAR_W - nw
    n_rows = qs[0].shape[0]
    blocks = [slice(r, r + block_rows) for r in range(0, n_rows, block_rows)]
    logits = []
    for q, far, near_bias in zip(qs, fars, near_biases):
        for rows in blocks:
            we = W - (n_rows - rows.stop)
            s_n = _dot_nt(q[rows], k_ref[0, fw:we, :]) + near_bias(rows, lo, lo + we - fw)
            s_f = None
            if fw:
                s_f = _dot_nt(q[rows], k_ref[0, 0:fw, :]) + far
            if madd_ref is not None:
                s_n = s_n + madd_ref[rows, fw:we]
                if fw:
                    s_f = s_f + madd_ref[rows, 0:fw]
            logits.append((s_n, s_f, we))
    outs = []
    for s_n, s_f, we in logits:
        m = jnp.max(s_n, axis=1, keepdims=True)
        if fw:
            m = jnp.maximum(m, jnp.max(s_f, axis=1, keepdims=True))
            p_f = jnp.exp(s_f - m)
        p_n = jnp.exp(s_n - m)
        l = jnp.sum(p_n, axis=1, keepdims=True)
        o = _dot(p_n.astype(_BF), v_ref[0, fw:we, :])
        if fw:
            l = l + jnp.sum(p_f, axis=1, keepdims=True)
            o = o + _dot(p_f.astype(_BF), v_ref[0, 0:fw, :])
        outs.append(o * (1.0 / l))
    nb = len(blocks)
    return [jnp.concatenate(outs[i * nb:(i + 1) * nb], axis=0) for i in range(len(qs))]


def _da_kernel(tbl_ref, q_ref, k_ref, v_ref, bn_ref, lam_ref, g_ref, o_ref, *, lambda_init):
    h = pl.program_id(1)
    a = pl.program_id(2)
    T = q_ref.shape[1]
    S = k_ref.shape[1]
    lane = lax.broadcasted_iota(_I32, (T, HEAD_DIM), 1)
    q = q_ref[0] * (DA_DIM ** -0.5)
    zero = jnp.zeros_like(q)
    q1 = jnp.where(lane < DA_DIM, q, zero)
    q2 = jnp.where(lane >= DA_DIM, q, zero)
    far = tbl_ref[N_BUCKETS // 2 - 1, h]
    lam = lam_ref[...]
    lam_full = (jnp.exp(jnp.sum(lam[0:1] * lam[1:2], axis=1, keepdims=True))
                - jnp.exp(jnp.sum(lam[2:3] * lam[3:4], axis=1, keepdims=True)) + lambda_init)
    near_bias = lambda rows, lo, hi: bn_ref[0, rows, lo:hi]

    for br in range(S // T):
        @pl.when(a == br)
        def _(br=br):
            W = (br + 1) * T
            o1, o2 = _attend([q1, q2], k_ref, v_ref, W, [far, far], [near_bias, near_bias], DA_ROWS)
            o = o1 - lam_full * o2
            ms = jnp.mean(o * o, axis=-1, keepdims=True)
            o = (o * lax.rsqrt(ms + SUBLN_EPS) * g_ref[...]) * (1.0 - lambda_init)
            o_ref[0] = o.astype(_BF)


def _diff_attn(qkv, rel_table, bn, lam, subln_g, lambda_init):
    B, S, _ = qkv.shape
    H = N_HEADS
    T = ATT_TILE
    kern = functools.partial(_da_kernel, lambda_init=lambda_init)
    return pl.pallas_call(
        kern,
        grid=(B, H, S // T),
        in_specs=[pl.BlockSpec(memory_space=pltpu.SMEM),
                  pl.BlockSpec((1, T, HEAD_DIM), lambda b, h, i: (b, i, h)),
                  pl.BlockSpec((1, S, HEAD_DIM), lambda b, h, i: (b, 0, H + h)),
                  pl.BlockSpec((1, S, HEAD_DIM), lambda b, h, i: (b, 0, 2 * H + h)),
                  pl.BlockSpec((1, T, NEAR_W), lambda b, h, i: (h, 0, 0)),
                  pl.BlockSpec((4, DA_DIM), lambda b, h, i: (0, 0)),
                  pl.BlockSpec((1, HEAD_DIM), lambda b, h, i: (0, 0))],
        out_specs=pl.BlockSpec((1, T, HEAD_DIM), lambda b, h, i: (b, i, h)),
        out_shape=jax.ShapeDtypeStruct((B, S, H * HEAD_DIM), _BF),
        compiler_params=_params(("parallel", "parallel", "arbitrary")),
        name="diff_attn",
    )(rel_table, qkv, qkv, qkv, bn, lam, subln_g)


_INT_MIN = np.int32(-2 ** 31)
_KEY_NEG_INF = np.int32(np.array(-np.inf, np.float32).view(np.int32) ^ np.int32(0x7FFFFFFF))
SEARCH_GROUPS = 4
HEADS_PER_STEP = 2


def _sa_branch(W, tbl_ref, q_ref, ckv_ref, iw_ref, ikk_ref, wuk_ref, wuv_ref, bn_ref,
               iqm_ref, keys_ref, madd_ref, oh_ref, last_ref):
    T = iw_ref.shape[1]
    H = q_ref.shape[1]
    iw = iw_ref[0]

    for cb in range(W // T):
        cols = slice(cb * T, (cb + 1) * T)
        ikt = ikk_ref[0, cols, :]
        score = jnp.zeros((T, T), _F32)
        for hh in range(H):
            score = score + jnp.maximum(_dot_nt(iqm_ref[hh], ikt), 0.0) * iw[:, hh:hh + 1]
        score = jnp.where(score == 0.0, 0.0, score)
        if cb == W // T - 1:
            row = lax.broadcasted_iota(_I32, (T, T), 0)
            col = lax.broadcasted_iota(_I32, (T, T), 1)
            score = jnp.where((col >> CHUNK_SHIFT) <= (row >> CHUNK_SHIFT), score, -jnp.inf)
        bits = lax.bitcast_convert_type(score, _I32)
        keys_ref[:, cols] = bits ^ ((bits >> 31) & np.int32(0x7FFFFFFF))

    def count(pred, rows=slice(None)):
        return jnp.sum(jnp.where(pred(keys_ref[rows, 0:W]), 1.0, 0.0), axis=1, keepdims=True)

    R = T // SEARCH_GROUPS

    def bit_body(it, thrs):
        bit = lax.shift_left(np.int32(1), np.int32(31) - it)
        out = []
        for g in range(SEARCH_GROUPS):
            cand_u = thrs[g] | bit
            cand = cand_u ^ _INT_MIN
            cnt = count(lambda k: k >= cand, slice(g * R, (g + 1) * R))
            out.append(jnp.where(cnt >= float(TOPK), cand_u, thrs[g]))
        return tuple(out)

    thrs = lax.fori_loop(0, 32, bit_body, tuple(jnp.zeros((R, 1), _I32) for _ in range(SEARCH_GROUPS)))
    thr = jnp.concatenate(thrs, axis=0) ^ _INT_MIN

    cnt_gt = count(lambda k: k > thr)
    cnt_ge = count(lambda k: k >= thr)
    need = float(TOPK) - cnt_gt
    last_ref[...] = jnp.full((T, 1), 2 ** 30, _I32)
    tied = jnp.logical_and(cnt_ge - cnt_gt > need, thr > _KEY_NEG_INF)

    @pl.when(jnp.max(jnp.where(tied, 1.0, 0.0)) > 0.0)
    def _():
        col = lax.broadcasted_iota(_I32, (T, W), 1)

        def idx_body(it, v):
            cand = v | lax.shift_left(np.int32(1), np.int32(10) - it)
            below = count(lambda k: jnp.logical_and(k == thr, col < cand))
            return jnp.where(below < need, cand, v)
        last_ref[...] = lax.fori_loop(0, 11, idx_body, jnp.zeros((T, 1), _I32))

    last = last_ref[...]
    k = keys_ref[:, 0:W]
    col = lax.broadcasted_iota(_I32, (T, W), 1)
    sel = jnp.logical_or(k > thr, jnp.logical_and(k == thr, col <= last))
    madd_ref[:, 0:W] = jnp.where(sel, 0.0, NEG)

    def head_body(hp, carry):
        hs = [hp * HEADS_PER_STEP + u for u in range(HEADS_PER_STEP)]
        qlats = [(_dot(q_ref[0, h], wuk_ref[h]) * (HEAD_DIM ** -0.5)).astype(_BF) for h in hs]
        olats = _attend(qlats, ckv_ref, ckv_ref, W,
                        [tbl_ref[N_BUCKETS // 2 - 1, h] for h in hs],
                        [functools.partial(lambda h, rows, lo, hi: bn_ref[h, rows, lo:hi], h) for h in hs],
                        SA_ROWS, madd_ref)
        for h, olat in zip(hs, olats):
            oh_ref[h] = _dot(olat.astype(_BF), wuv_ref[h])
        return carry

    lax.fori_loop(0, H // HEADS_PER_STEP, head_body, 0)


def _sa_kernel(tbl_ref, q_ref, ckv_ref, iq_ref, ikk_ref, iw_ref, wuk_ref, wuv_ref, bn_ref,
               o_ref, iqm_ref, keys_ref, madd_ref, oh_ref, last_ref):
    a = pl.program_id(1)
    T = iq_ref.shape[1]
    S = ckv_ref.shape[1]
    H = q_ref.shape[1]
    lane = lax.broadcasted_iota(_I32, (T, 128), 1)
    for p in range(H // 2):
        iqp = iq_ref[0, :, p * 128:(p + 1) * 128]
        zero = jnp.zeros_like(iqp)
        iqm_ref[2 * p] = jnp.where(lane < IDX_DIM, iqp, zero)
        iqm_ref[2 * p + 1] = jnp.where(lane >= IDX_DIM, iqp, zero)

    for br in range(S // T):
        @pl.when(a == br)
        def _(br=br):
            _sa_branch((br + 1) * T, tbl_ref, q_ref, ckv_ref, iw_ref, ikk_ref, wuk_ref, wuv_ref, bn_ref,
                       iqm_ref, keys_ref, madd_ref, oh_ref, last_ref)

    for h in range(H):
        o_ref[0, :, h * HEAD_DIM:(h + 1) * HEAD_DIM] = oh_ref[h].astype(_BF)


def _sparse_attn(q, ckv, iq, ikk, iw, wuk, wuv, rel_table, bn):
    B, H, S, _ = q.shape
    T = ATT_TILE
    return pl.pallas_call(
        _sa_kernel,
        grid=(B, S // T),
        in_specs=[pl.BlockSpec(memory_space=pltpu.SMEM),
                  pl.BlockSpec((1, H, T, HEAD_DIM), lambda b, i: (b, 0, i, 0)),
                  pl.BlockSpec((1, S, SA_LATENT), lambda b, i: (b, 0, 0)),
                  pl.BlockSpec((1, T, 512), lambda b, i: (b, i, 0)),
                  pl.BlockSpec((1, S, 128), lambda b, i: (b, 0, 0)),
                  pl.BlockSpec((1, T, 128), lambda b, i: (b, i, 0)),
                  _full((H, HEAD_DIM, SA_LATENT)),
                  _full((H, SA_LATENT, HEAD_DIM)),
                  _full((H, T, NEAR_W))],
        out_specs=pl.BlockSpec((1, T, H * HEAD_DIM), lambda b, i: (b, i, 0)),
        out_shape=jax.ShapeDtypeStruct((B, S, H * HEAD_DIM), _BF),
        scratch_shapes=[pltpu.VMEM((H, T, 128), _BF),
                        pltpu.VMEM((T, S), _I32),
                        pltpu.VMEM((T, S), _F32),
                        pltpu.VMEM((H, T, HEAD_DIM), _F32),
                        pltpu.VMEM((T, 1), _I32)],
        compiler_params=_params(("parallel", "arbitrary")),
        name="sparse_attn",
    )(rel_table, q, ckv, iq, ikk, iw, wuk, wuv, bn)


def _cv_kernel(x_ref, g_ref, sc_ref, sh_ref, w1_ref, b1_ref, wdw_ref, bdw_ref, lg_ref, lb_ref, w2_ref, b2_ref,
               g1_ref, o_ref, ext_ref, y_ref):
    i = pl.program_id(1)
    tm = x_ref.shape[1]
    D = x_ref.shape[2]
    n_ext = CONV_HALO + tm

    @pl.when(i == 0)
    def _():
        ext_ref[0, 0:CONV_HALO] = jnp.zeros((CONV_HALO, D), _F32)

    @pl.when(i > 0)
    def _():
        ext_ref[0, 0:CONV_HALO] = ext_ref[0, tm:n_ext]

    hn = _rms_mod(x_ref[0], g_ref[...], sc_ref[0], sh_ref[0]).astype(_BF)
    tn = 512
    n = D // tn
    lo = lambda c: slice(c * tn, (c + 1) * tn)
    hi = lambda c: slice(D + c * tn, D + (c + 1) * tn)
    dots = lambda c: (_dot(hn, w1_ref[:, lo(c)]), _dot(hn, w1_ref[:, hi(c)]))
    nxt = dots(0)
    for c in range(n):
        cur = nxt
        if c + 1 < n:
            nxt = dots(c + 1)
        ext_ref[0, CONV_HALO:n_ext, lo(c)] = ((cur[0] + b1_ref[:, lo(c)])
                                              * jax.nn.sigmoid(cur[1] + b1_ref[:, hi(c)]))
    for r in range(1, SUBLANES):
        ext_ref[r, 0:n_ext - SUBLANES] = ext_ref[0, r:r + n_ext - SUBLANES]
    rb = CONV_ROWS
    base = CONV_HALO - (CONV_WIDTH - 1)

    def block(blk, carry):
        row0 = pl.multiple_of(blk * rb, rb)
        acc = jnp.broadcast_to(bdw_ref[...], (rb, bdw_ref.shape[1]))
        for k in range(CONV_WIDTH):
            r = (base + k) % SUBLANES
            al = base + k - r
            w = jnp.concatenate([wdw_ref[k]] * (rb // SUBLANES), axis=0)
            acc = acc + w * ext_ref[r, pl.ds(row0 + al, rb), :]
        y_ref[pl.ds(row0, rb)] = _silu(_layernorm(acc, lg_ref[...], lb_ref[...])).astype(_BF)
        return carry

    lax.fori_loop(0, tm // rb, block, 0, unroll=2)
    y = _dot(y_ref[...], w2_ref[...]) + b2_ref[...]
    o_ref[0] = x_ref[0] + g1_ref[0] * y


def _cv_mixer(x, g, sc, sh, w1, b1, w_dw, b_dw, ln_g, ln_b, w2, b2, g1):
    B, S, D = x.shape
    tm = 512
    vec = lambda: _full((1, D))
    return pl.pallas_call(
        _cv_kernel,
        grid=(B, S // tm),
        in_specs=_in_specs_common(tm, D) + [_full((D, 2 * D)), _full((1, 2 * D)),
                                            _full((CONV_WIDTH, SUBLANES, D)), vec(), vec(), vec(),
                                            _full((D, D)), vec(),
                                            pl.BlockSpec((1, 1, D), lambda b, i: (b, 0, 0))],
        out_specs=pl.BlockSpec((1, tm, D), lambda b, i: (b, i, 0)),
        out_shape=jax.ShapeDtypeStruct((B, S, D), _F32),
        scratch_shapes=[pltpu.VMEM((SUBLANES, CONV_HALO + tm, D), _F32), pltpu.VMEM((tm, D), _BF)],
        compiler_params=_params(("parallel", "arbitrary")),
        name="cv_mixer",
    )(x, g, sc, sh, w1, b1, w_dw, b_dw, ln_g, ln_b, w2, b2, g1)


def _out_kernel(a_ref, w_ref, x_ref, g1_ref, o_ref):
    o_ref[0] = x_ref[0] + g1_ref[0] * _dot(a_ref[0], w_ref[...])


def _out_proj(a, w, x, g1):
    B, S, D = x.shape
    K = a.shape[2]
    tm = 512
    return pl.pallas_call(
        _out_kernel,
        grid=(B, S // tm),
        in_specs=[pl.BlockSpec((1, tm, K), lambda b, i: (b, i, 0)),
                  _full((K, D)),
                  pl.BlockSpec((1, tm, D), lambda b, i: (b, i, 0)),
                  pl.BlockSpec((1, 1, D), lambda b, i: (b, 0, 0))],
        out_specs=pl.BlockSpec((1, tm, D), lambda b, i: (b, i, 0)),
        out_shape=jax.ShapeDtypeStruct((B, S, D), _F32),
        compiler_params=_params(("parallel", "parallel")),
        name="out_proj",
    )(a, w, x, g1)


def _ffn_kernel(x_ref, xh_ref, g_ref, sc_ref, sh_ref, g2_ref, wup_ref, wdw_ref, bdw_ref, wdn_ref,
                fg_ref, o_ref, he_ref, act_ref, acc_ref, *, final):
    i = pl.program_id(1)
    tm = x_ref.shape[1]
    F = wdn_ref.shape[0]
    x = x_ref[0]
    he_ref[FFN_HALO:FFN_HALO + tm] = _rms_mod(x, g_ref[...], sc_ref[0], sh_ref[0]).astype(_BF)
    hh = _rms_mod(xh_ref[0], g_ref[...], sc_ref[0], sh_ref[0])
    he_ref[0:FFN_HALO] = jnp.where(i > 0, hh, jnp.zeros_like(hh)).astype(_BF)
    tf = FFN_CHUNK
    n_chunks = F // tf

    def up(c):
        he = he_ref[...]
        return (_dot(he, wup_ref[:, c * tf:(c + 1) * tf]), _dot(he, wup_ref[:, F + c * tf:F + (c + 1) * tf]))

    def conv(a, col0):
        cols = slice(col0, col0 + tf)
        w = wdw_ref[:, cols]
        return (a[FFN_HALO - 2:FFN_HALO - 2 + tm] * w[0:1]
                + a[FFN_HALO - 1:FFN_HALO - 1 + tm] * w[1:2]
                + a[FFN_HALO:FFN_HALO + tm] * w[2:3] + bdw_ref[:, cols])

    nxt = up(0)
    for c in range(n_chunks):
        cur = nxt
        if c + 1 < n_chunks:
            nxt = up(c + 1)
        act = _silu(conv(cur[0], c * tf)) * conv(cur[1], F + c * tf)
        act_ref[:, c * tf:(c + 1) * tf] = act.astype(_BF)
        if (c + 1) % FFN_DOWN_GROUP == 0 or c + 1 == n_chunks:
            lo = (c // FFN_DOWN_GROUP) * FFN_DOWN_GROUP * tf
            part = _dot(act_ref[:, lo:(c + 1) * tf], wdn_ref[lo:(c + 1) * tf, :])
            if lo == 0:
                acc_ref[...] = part
            else:
                acc_ref[...] += part
    xn = x + g2_ref[0] * acc_ref[...]
    if final:
        ms = jnp.mean(xn * xn, axis=-1, keepdims=True)
        xn = xn * lax.rsqrt(ms + EPS) * fg_ref[...]
    o_ref[0] = xn


def _ffn(x, g, sc, sh, g2, w_up, w_dw, b_dw, w_down, final_g, final):
    B, S, D = x.shape
    F = w_down.shape[0]
    tm = 512
    hb = tm // FFN_HALO
    kern = functools.partial(_ffn_kernel, final=final)
    mod = lambda: pl.BlockSpec((1, 1, D), lambda b, i: (b, 0, 0))
    return pl.pallas_call(
        kern,
        grid=(B, S // tm),
        in_specs=[pl.BlockSpec((1, tm, D), lambda b, i: (b, i, 0)),
                  pl.BlockSpec((1, FFN_HALO, D), lambda b, i: (b, jnp.maximum(i * hb - 1, 0), 0)),
                  _full((1, D)), mod(), mod(), mod(),
                  _full((D, 2 * F)), _full((3, 2 * F)), _full((1, 2 * F)), _full((F, D)),
                  _full((1, D))],
        out_specs=pl.BlockSpec((1, tm, D), lambda b, i: (b, i, 0)),
        out_shape=jax.ShapeDtypeStruct((B, S, D), _F32),
        scratch_shapes=[pltpu.VMEM((FFN_HALO + tm, D), _BF), pltpu.VMEM((tm, F), _BF),
                        pltpu.VMEM((tm, D), _F32)],
        compiler_params=_params(("parallel", "parallel")),
        name="ffn",
    )(x, x, g, sc, sh, g2, w_up, w_dw, b_dw, w_down, final_g)


def kernel(x, c, rel_table, ada_w, ada_b, norm_g, final_g, da_w_in, da_lam, da_subln_g, da_w_out, cv_w_pw1, cv_b_pw1, cv_w_dw, cv_b_dw, cv_ln_g, cv_ln_b, cv_w_pw2, cv_b_pw2, sa_w_in, sa_kv_g, sa_w_uk, sa_w_uv, sa_w_out, sg_w_in, sg_b_in, sg_ln_g, sg_ln_b, sg_w_s, sg_b_s, sg_w_out, sg_b_out, ff_w_up, ff_w_dw, ff_b_dw, ff_w_down):
    B, S, D = x.shape
    depth = ada_w.shape[0]
    n_mixers = 4
    H = N_HEADS
    mods = _ada(c, ada_w, ada_b)
    bn = _bias_tiles(rel_table)
    row = lambda v: v.reshape(1, -1)

    for layer in range(depth):
        kind = layer % n_mixers
        j = layer // n_mixers
        sh1, sc1, g1, sh2, sc2, g2 = [m.reshape(B, 1, D) for m in jnp.split(mods[layer], 6, axis=-1)]
        ng1 = row(norm_g[layer, 0])
        if kind == 0:
            lambda_init = 0.8 - 0.6 * math.exp(-0.3 * layer)
            qkv = _in_da(x, ng1, sc1, sh1, da_w_in[j].astype(_BF))
            o = _diff_attn(qkv, rel_table, bn, da_lam[j], row(da_subln_g[j]), lambda_init)
            x = _out_proj(o, da_w_out[j].astype(_BF), x, g1)
        elif kind == 1:
            w_dw = jnp.broadcast_to(cv_w_dw[j][:, None, :], (CONV_WIDTH, SUBLANES, D))
            x = _cv_mixer(x, ng1, sc1, sh1, cv_w_pw1[j].astype(_BF), row(cv_b_pw1[j]), w_dw, row(cv_b_dw[j]),
                          row(cv_ln_g[j]), row(cv_ln_b[j]), cv_w_pw2[j].astype(_BF), row(cv_b_pw2[j]), g1)
        elif kind == 2:
            w = sa_w_in[j]
            o1 = H * HEAD_DIM
            o2 = o1 + SA_LATENT
            o3 = o2 + H * IDX_DIM
            o4 = o3 + IDX_DIM
            w_ik = w[:, o3:o4]
            w_iw = jnp.pad(w[:, o4:], ((0, 0), (0, 128 - (w.shape[1] - o4))))
            w_cat = jnp.concatenate([w[:, :o3], w_ik, w_ik, w_iw], axis=1).astype(_BF)
            q, ckv, iq, ikk, iw = _in_sa(x, ng1, sc1, sh1, w_cat, row(sa_kv_g[j]))
            wuk = jnp.transpose(sa_w_uk[j], (1, 2, 0)).astype(_BF)
            wuv = jnp.transpose(sa_w_uv[j], (1, 0, 2)).astype(_BF)
            o = _sparse_attn(q, ckv, iq, ikk, iw, wuk, wuv, rel_table, bn)
            x = _out_proj(o, sa_w_out[j].astype(_BF), x, g1)
        else:
            x = _sg_mixer(x, ng1, sc1, sh1, sg_w_in[j].astype(_BF), row(sg_b_in[j]), row(sg_ln_g[j]),
                          row(sg_ln_b[j]), sg_w_s[j], sg_b_s[j].T, sg_w_out[j].astype(_BF),
                          row(sg_b_out[j]), g1)
        x = _ffn(x, row(norm_g[layer, 1]), sc2, sh2, g2, ff_w_up[layer].astype(_BF), ff_w_dw[layer],
                 row(ff_b_dw[layer]), ff_w_down[layer].astype(_BF), row(final_g), layer == depth - 1)
    return x
```

```python
import functools
import math

import jax
import jax.numpy as jnp
import numpy as np
from jax import lax
from jax.experimental import pallas as pl
from jax.experimental.pallas import tpu as pltpu

_BF = jnp.bfloat16
_F32 = jnp.float32
_I32 = jnp.int32

EPS = 1e-6
SUBLN_EPS = 1e-5
NEG = -1e30
CHUNK = 64
CHUNK_SHIFT = 6
N_HEADS = 8
HEAD_DIM = 128
DA_DIM = 64
SA_LATENT = 256
IDX_DIM = 64
TOPK = 256
N_BUCKETS = 32
MAX_DISTANCE = 128
SG_CHUNK = 128
SG_GROUPS = 8
CONV_WIDTH = 31
CONV_HALO = 32
CONV_ROWS = 32
SUBLANES = 8
FFN_HALO = 8
PROJ_ROWS = 1024
SG_ROWS = 1024
FFN_ROWS = 1024
FFN_CHUNK = 256
FFN_DOWN_GROUP = 11
ATT_TILE = 512
DA_ROWS = 128
SA_ROWS = 256
NEAR_W = 768
VMEM_LIMIT = 56 * 1024 * 1024


def _dot(a, b):
    return jnp.dot(a, b, preferred_element_type=_F32)


def _dot_nt(a, b):
    return lax.dot_general(a, b, (((1,), (1,)), ((), ())), preferred_element_type=_F32)


def _rms_mod(x, g, sc, sh):
    ms = jnp.mean(x * x, axis=-1, keepdims=True)
    return (x * lax.rsqrt(ms + EPS) * g) * (1.0 + sc) + sh


def _layernorm(x, g, b):
    mu = jnp.mean(x, axis=-1, keepdims=True)
    xc = x - mu
    var = jnp.mean(xc * xc, axis=-1, keepdims=True)
    return xc * lax.rsqrt(var + EPS) * g + b


def _silu(x):
    return x * jax.nn.sigmoid(x)


def _params(sem):
    return pltpu.CompilerParams(dimension_semantics=sem, vmem_limit_bytes=VMEM_LIMIT)


def _full(shape):
    n = len(shape)
    return pl.BlockSpec(shape, lambda *_: (0,) * n, pipeline_mode=pl.Buffered(1))


def _ada_kernel(c_ref, w_ref, b_ref, o_ref):
    ca = _silu(c_ref[...]).astype(_BF)
    o_ref[0] = _dot(ca, w_ref[0].astype(_BF)) + b_ref[0]


def _ada(c, ada_w, ada_b):
    L, D, N = ada_w.shape
    B = c.shape[0]
    tn = N // 4
    return pl.pallas_call(
        _ada_kernel,
        grid=(L, N // tn),
        in_specs=[pl.BlockSpec((B, D), lambda l, j: (0, 0)),
                  pl.BlockSpec((1, D, tn), lambda l, j: (l, 0, j)),
                  pl.BlockSpec((1, 1, tn), lambda l, j: (l, 0, j))],
        out_specs=pl.BlockSpec((1, B, tn), lambda l, j: (l, 0, j)),
        out_shape=jax.ShapeDtypeStruct((L, B, N), _F32),
        compiler_params=_params(("arbitrary", "arbitrary")),
        name="ada",
    )(c, ada_w, ada_b.reshape(L, 1, N))


def _bias_kernel(tbl_ref, bn_ref):
    h = pl.program_id(0)
    shape = bn_ref.shape[1:]
    i = lax.broadcasted_iota(_I32, shape, 0)
    j = lax.broadcasted_iota(_I32, shape, 1) - (NEAR_W - ATT_TILE)
    nb = N_BUCKETS // 2
    max_exact = nb // 2
    rel = j - i
    ret = jnp.where(rel > 0, nb, 0)
    n = jnp.abs(rel)
    nf = jnp.maximum(n, 1).astype(_F32)
    large = max_exact + (jnp.log(nf / max_exact) / math.log(MAX_DISTANCE / max_exact)
                         * (nb - max_exact)).astype(_I32)
    large = jnp.minimum(large, nb - 1)
    bucket = ret + jnp.where(n < max_exact, n, large)
    out = jnp.zeros(shape, _F32)
    for bk in range(N_BUCKETS):
        out = jnp.where(bucket == bk, tbl_ref[bk, h], out)
    visible = (j >> CHUNK_SHIFT) <= (i >> CHUNK_SHIFT)
    bn_ref[0] = jnp.where(visible, out, NEG)


def _bias_tiles(rel_table):
    H = rel_table.shape[1]
    return pl.pallas_call(
        _bias_kernel,
        grid=(H,),
        in_specs=[pl.BlockSpec(memory_space=pltpu.SMEM)],
        out_specs=pl.BlockSpec((1, ATT_TILE, NEAR_W), lambda h: (h, 0, 0)),
        out_shape=jax.ShapeDtypeStruct((H, ATT_TILE, NEAR_W), _F32),
        compiler_params=_params(("arbitrary",)),
        name="bias_tiles",
    )(rel_table)


def _in_specs_common(tm, D):
    return [pl.BlockSpec((1, tm, D), lambda b, i: (b, i, 0)),
            pl.BlockSpec((1, D), lambda b, i: (0, 0)),
            pl.BlockSpec((1, 1, D), lambda b, i: (b, 0, 0)),
            pl.BlockSpec((1, 1, D), lambda b, i: (b, 0, 0))]


def _in_da_kernel(x_ref, g_ref, sc_ref, sh_ref, w_ref, o_ref):
    hn = _rms_mod(x_ref[0], g_ref[...], sc_ref[0], sh_ref[0]).astype(_BF)
    N = w_ref.shape[1]
    tn = 512
    for c in range(N // tn):
        o_ref[0, :, c * tn:(c + 1) * tn] = _dot(hn, w_ref[:, c * tn:(c + 1) * tn]).astype(_BF)


def _in_da(x, g, sc, sh, w):
    B, S, D = x.shape
    N = w.shape[1]
    tm = PROJ_ROWS
    return pl.pallas_call(
        _in_da_kernel,
        grid=(B, S // tm),
        in_specs=_in_specs_common(tm, D) + [_full((D, N))],
        out_specs=pl.BlockSpec((1, tm, N), lambda b, i: (b, i, 0)),
        out_shape=jax.ShapeDtypeStruct((B, S, N), _BF),
        compiler_params=_params(("parallel", "parallel")),
        name="in_da",
    )(x, g, sc, sh, w)


def _in_sa_kernel(x_ref, g_ref, sc_ref, sh_ref, w_ref, kvg_ref,
                  q_ref, ckv_ref, iq_ref, ikk_ref, iw_ref):
    hn = _rms_mod(x_ref[0], g_ref[...], sc_ref[0], sh_ref[0]).astype(_BF)
    H = q_ref.shape[1]
    for c in range(H // 2):
        r = _dot(hn, w_ref[:, c * 256:(c + 1) * 256])
        q_ref[0, 2 * c] = r[:, :HEAD_DIM].astype(_BF)
        q_ref[0, 2 * c + 1] = r[:, HEAD_DIM:].astype(_BF)
    o = H * HEAD_DIM
    ckv = _dot(hn, w_ref[:, o:o + SA_LATENT])
    ms = jnp.mean(ckv * ckv, axis=-1, keepdims=True)
    ckv_ref[0] = (ckv * lax.rsqrt(ms + EPS) * kvg_ref[...]).astype(_BF)
    o += SA_LATENT
    iq_ref[0] = (_dot(hn, w_ref[:, o:o + 512]) * (IDX_DIM ** -0.5)).astype(_BF)
    o += 512
    r = _dot(hn, w_ref[:, o:o + 256])
    ikk_ref[0] = r[:, :128].astype(_BF)
    iw_ref[0] = r[:, 128:] * (N_HEADS ** -0.5)


def _in_sa(x, g, sc, sh, w, kv_g):
    B, S, D = x.shape
    N = w.shape[1]
    H = N_HEADS
    tm = PROJ_ROWS
    row = lambda n: pl.BlockSpec((1, tm, n), lambda b, i: (b, i, 0))
    return pl.pallas_call(
        _in_sa_kernel,
        grid=(B, S // tm),
        in_specs=_in_specs_common(tm, D) + [_full((D, N)), _full((1, SA_LATENT))],
        out_specs=[pl.BlockSpec((1, H, tm, HEAD_DIM), lambda b, i: (b, 0, i, 0)),
                   row(SA_LATENT), row(512), row(128), row(128)],
        out_shape=[jax.ShapeDtypeStruct((B, H, S, HEAD_DIM), _BF),
                   jax.ShapeDtypeStruct((B, S, SA_LATENT), _BF),
                   jax.ShapeDtypeStruct((B, S, 512), _BF),
                   jax.ShapeDtypeStruct((B, S, 128), _BF),
                   jax.ShapeDtypeStruct((B, S, 128), _F32)],
        compiler_params=_params(("parallel", "parallel")),
        name="in_sa",
    )(x, g, sc, sh, w, kv_g)


def _sg_kernel(x_ref, g_ref, sc_ref, sh_ref, w_ref, b_ref, lg_ref, lb_ref, ws_ref, bs_ref, wo_ref, bo_ref,
               g1_ref, o_ref, a_scr, v_scr, z_scr):
    x = x_ref[0]
    hn = _rms_mod(x, g_ref[...], sc_ref[0], sh_ref[0]).astype(_BF)
    tm = x.shape[0]
    W = a_scr.shape[1]
    G = ws_ref.shape[0]
    C = SG_CHUNK
    gw = W // G
    tn = 512
    n = W // tn
    lo = lambda c: slice(c * tn, (c + 1) * tn)
    hi = lambda c: slice(W + c * tn, W + (c + 1) * tn)
    ti = lax.broadcasted_iota(_I32, (C, C), 0)
    si = lax.broadcasted_iota(_I32, (C, C), 1)
    cols = [hi(c) for c in range(n)] + [lo(c) for c in range(n)]
    nxt = _dot(hn, w_ref[:, cols[0]])
    for t in range(2 * n):
        cur = nxt
        if t + 1 < 2 * n:
            nxt = _dot(hn, w_ref[:, cols[t + 1]])
        act = jax.nn.gelu(cur + b_ref[:, cols[t]], approximate=True)
        if t < n:
            a_scr[:, lo(t)] = act
            if t == n - 1:
                v_scr[...] = _layernorm(a_scr[...], lg_ref[...], lb_ref[...]).astype(_BF)
            continue
        for gl in range(tn // gw):
            g = (t - n) * (tn // gw) + gl
            wg = jnp.where(ti >= si, ws_ref[g], 0.0).astype(_BF)
            bcol = bs_ref[:, g:g + 1]
            gcols = slice(g * gw, (g + 1) * gw)
            for r in range(tm // C):
                rows = slice(r * C, (r + 1) * C)
                sv = _dot(wg, v_scr[rows, gcols]) + bcol
                z_scr[rows, gcols] = (act[rows, gl * gw:(gl + 1) * gw] * sv).astype(_BF)
    y = _dot(z_scr[...], wo_ref[...]) + bo_ref[...]
    o_ref[0] = x + g1_ref[0] * y


def _sg_mixer(x, g, sc, sh, w, b, ln_g, ln_b, w_s, b_s_t, w_out, b_out, g1):
    B, S, D = x.shape
    N = w.shape[1]
    W = N // 2
    G = w_s.shape[0]
    tm = SG_ROWS
    return pl.pallas_call(
        _sg_kernel,
        grid=(B, S // tm),
        in_specs=_in_specs_common(tm, D) + [_full((D, N)), _full((1, N)), _full((1, W)), _full((1, W)),
                                            _full((G, SG_CHUNK, SG_CHUNK)), _full((SG_CHUNK, G)),
                                            _full((W, D)), _full((1, D)),
                                            pl.BlockSpec((1, 1, D), lambda b, i: (b, 0, 0))],
        out_specs=pl.BlockSpec((1, tm, D), lambda b, i: (b, i, 0)),
        out_shape=jax.ShapeDtypeStruct((B, S, D), _F32),
        scratch_shapes=[pltpu.VMEM((tm, W), _F32), pltpu.VMEM((tm, W), _BF), pltpu.VMEM((tm, W), _BF)],
        compiler_params=_params(("parallel", "parallel")),
        name="sg_mixer",
    )(x, g, sc, sh, w, b, ln_g, ln_b, w_s, b_s_t, w_out, b_out, g1)


def _attend(qs, k_ref, v_ref, W, fars, near_biases, block_rows, madd_ref=None):
    nw = min(W, NEAR_W)
    fw = W - nw
    lo = NEAR_W - nw
    n_rows = qs[0].shape[0]
    blocks = [slice(r, r + block_rows) for r in range(0, n_rows, block_rows)]
    logits = []
    for q, far, near_bias in zip(qs, fars, near_biases):
        for rows in blocks:
            we = W - (n_rows - rows.stop)
            s_n = _dot_nt(q[rows], k_ref[0, fw:we, :]) + near_bias(rows, lo, lo + we - fw)
            s_f = None
            if fw:
                s_f = _dot_nt(q[rows], k_ref[0, 0:fw, :]) + far
            if madd_ref is not None:
                s_n = s_n + madd_ref[rows, fw:we]
                if fw:
                    s_f = s_f + madd_ref[rows, 0:fw]
            logits.append((s_n, s_f, we))
    outs = []
    for s_n, s_f, we in logits:
        m = jnp.max(s_n, axis=1, keepdims=True)
        if fw:
            m = jnp.maximum(m, jnp.max(s_f, axis=1, keepdims=True))
            p_f = jnp.exp(s_f - m)
        p_n = jnp.exp(s_n - m)
        l = jnp.sum(p_n, axis=1, keepdims=True)
        o = _dot(p_n.astype(_BF), v_ref[0, fw:we, :])
        if fw:
            l = l + jnp.sum(p_f, axis=1, keepdims=True)
            o = o + _dot(p_f.astype(_BF), v_ref[0, 0:fw, :])
        outs.append(o * (1.0 / l))
    nb = len(blocks)
    return [jnp.concatenate(outs[i * nb:(i + 1) * nb], axis=0) for i in range(len(qs))]


def _da_kernel(tbl_ref, q_ref, k_ref, v_ref, bn_ref, lam_ref, g_ref, o_ref, *, lambda_init):
    h = pl.program_id(1)
    a = pl.program_id(2)
    T = q_ref.shape[1]
    S = k_ref.shape[1]
    lane = lax.broadcasted_iota(_I32, (T, HEAD_DIM), 1)
    q = q_ref[0] * (DA_DIM ** -0.5)
    zero = jnp.zeros_like(q)
    q1 = jnp.where(lane < DA_DIM, q, zero)
    q2 = jnp.where(lane >= DA_DIM, q, zero)
    far = tbl_ref[N_BUCKETS // 2 - 1, h]
    lam = lam_ref[...]
    lam_full = (jnp.exp(jnp.sum(lam[0:1] * lam[1:2], axis=1, keepdims=True))
                - jnp.exp(jnp.sum(lam[2:3] * lam[3:4], axis=1, keepdims=True)) + lambda_init)
    near_bias = lambda rows, lo, hi: bn_ref[0, rows, lo:hi]

    for br in range(S // T):
        @pl.when(a == br)
        def _(br=br):
            W = (br + 1) * T
            o1, o2 = _attend([q1, q2], k_ref, v_ref, W, [far, far], [near_bias, near_bias], DA_ROWS)
            o = o1 - lam_full * o2
            ms = jnp.mean(o * o, axis=-1, keepdims=True)
            o = (o * lax.rsqrt(ms + SUBLN_EPS) * g_ref[...]) * (1.0 - lambda_init)
            o_ref[0] = o.astype(_BF)


def _diff_attn(qkv, rel_table, bn, lam, subln_g, lambda_init):
    B, S, _ = qkv.shape
    H = N_HEADS
    T = ATT_TILE
    kern = functools.partial(_da_kernel, lambda_init=lambda_init)
    return pl.pallas_call(
        kern,
        grid=(B, H, S // T),
        in_specs=[pl.BlockSpec(memory_space=pltpu.SMEM),
                  pl.BlockSpec((1, T, HEAD_DIM), lambda b, h, i: (b, i, h)),
                  pl.BlockSpec((1, S, HEAD_DIM), lambda b, h, i: (b, 0, H + h)),
                  pl.BlockSpec((1, S, HEAD_DIM), lambda b, h, i: (b, 0, 2 * H + h)),
                  pl.BlockSpec((1, T, NEAR_W), lambda b, h, i: (h, 0, 0)),
                  pl.BlockSpec((4, DA_DIM), lambda b, h, i: (0, 0)),
                  pl.BlockSpec((1, HEAD_DIM), lambda b, h, i: (0, 0))],
        out_specs=pl.BlockSpec((1, T, HEAD_DIM), lambda b, h, i: (b, i, h)),
        out_shape=jax.ShapeDtypeStruct((B, S, H * HEAD_DIM), _BF),
        compiler_params=_params(("parallel", "parallel", "arbitrary")),
        name="diff_attn",
    )(rel_table, qkv, qkv, qkv, bn, lam, subln_g)


_INT_MIN = np.int32(-2 ** 31)
_KEY_NEG_INF = np.int32(np.array(-np.inf, np.float32).view(np.int32) ^ np.int32(0x7FFFFFFF))
SEARCH_GROUPS = 4
HEADS_PER_STEP = 2


def _sa_branch(W, tbl_ref, q_ref, ckv_ref, iw_ref, ikk_ref, wuk_ref, wuv_ref, bn_ref,
               iqm_ref, keys_ref, madd_ref, oh_ref, last_ref):
    T = iw_ref.shape[1]
    H = q_ref.shape[1]
    iw = iw_ref[0]

    for cb in range(W // T):
        cols = slice(cb * T, (cb + 1) * T)
        ikt = ikk_ref[0, cols, :]
        score = jnp.zeros((T, T), _F32)
        for hh in range(H):
            score = score + jnp.maximum(_dot_nt(iqm_ref[hh], ikt), 0.0) * iw[:, hh:hh + 1]
        score = jnp.where(score == 0.0, 0.0, score)
        if cb == W // T - 1:
            row = lax.broadcasted_iota(_I32, (T, T), 0)
            col = lax.broadcasted_iota(_I32, (T, T), 1)
            score = jnp.where((col >> CHUNK_SHIFT) <= (row >> CHUNK_SHIFT), score, -jnp.inf)
        bits = lax.bitcast_convert_type(score, _I32)
        keys_ref[:, cols] = bits ^ ((bits >> 31) & np.int32(0x7FFFFFFF))

    def count(pred, rows=slice(None)):
        return jnp.sum(jnp.where(pred(keys_ref[rows, 0:W]), 1.0, 0.0), axis=1, keepdims=True)

    R = T // SEARCH_GROUPS

    def bit_body(it, thrs):
        bit = lax.shift_left(np.int32(1), np.int32(31) - it)
        out = []
        for g in range(SEARCH_GROUPS):
            cand_u = thrs[g] | bit
            cand = cand_u ^ _INT_MIN
            cnt = count(lambda k: k >= cand, slice(g * R, (g + 1) * R))
            out.append(jnp.where(cnt >= float(TOPK), cand_u, thrs[g]))
        return tuple(out)

    thrs = lax.fori_loop(0, 32, bit_body, tuple(jnp.zeros((R, 1), _I32) for _ in range(SEARCH_GROUPS)))
    thr = jnp.concatenate(thrs, axis=0) ^ _INT_MIN

    cnt_gt = count(lambda k: k > thr)
    cnt_ge = count(lambda k: k >= thr)
    need = float(TOPK) - cnt_gt
    last_ref[...] = jnp.full((T, 1), 2 ** 30, _I32)
    tied = jnp.logical_and(cnt_ge - cnt_gt > need, thr > _KEY_NEG_INF)

    @pl.when(jnp.max(jnp.where(tied, 1.0, 0.0)) > 0.0)
    def _():
        col = lax.broadcasted_iota(_I32, (T, W), 1)

        def idx_body(it, v):
            cand = v | lax.shift_left(np.int32(1), np.int32(10) - it)
            below = count(lambda k: jnp.logical_and(k == thr, col < cand))
            return jnp.where(below < need, cand, v)
        last_ref[...] = lax.fori_loop(0, 11, idx_body, jnp.zeros((T, 1), _I32))

    last = last_ref[...]
    k = keys_ref[:, 0:W]
    col = lax.broadcasted_iota(_I32, (T, W), 1)
    sel = jnp.logical_or(k > thr, jnp.logical_and(k == thr, col <= last))
    madd_ref[:, 0:W] = jnp.where(sel, 0.0, NEG)

    def head_body(hp, carry):
        hs = [hp * HEADS_PER_STEP + u for u in range(HEADS_PER_STEP)]
        qlats = [(_dot(q_ref[0, h], wuk_ref[h]) * (HEAD_DIM ** -0.5)).astype(_BF) for h in hs]
        olats = _attend(qlats, ckv_ref, ckv_ref, W,
                        [tbl_ref[N_BUCKETS // 2 - 1, h] for h in hs],
                        [functools.partial(lambda h, rows, lo, hi: bn_ref[h, rows, lo:hi], h) for h in hs],
                        SA_ROWS, madd_ref)
        for h, olat in zip(hs, olats):
            oh_ref[h] = _dot(olat.astype(_BF), wuv_ref[h])
        return carry

    lax.fori_loop(0, H // HEADS_PER_STEP, head_body, 0)


def _sa_kernel(tbl_ref, q_ref, ckv_ref, iq_ref, ikk_ref, iw_ref, wuk_ref, wuv_ref, bn_ref,
               o_ref, iqm_ref, keys_ref, madd_ref, oh_ref, last_ref):
    a = pl.program_id(1)
    T = iq_ref.shape[1]
    S = ckv_ref.shape[1]
    H = q_ref.shape[1]
    lane = lax.broadcasted_iota(_I32, (T, 128), 1)
    for p in range(H // 2):
        iqp = iq_ref[0, :, p * 128:(p + 1) * 128]
        zero = jnp.zeros_like(iqp)
        iqm_ref[2 * p] = jnp.where(lane < IDX_DIM, iqp, zero)
        iqm_ref[2 * p + 1] = jnp.where(lane >= IDX_DIM, iqp, zero)

    for br in range(S // T):
        @pl.when(a == br)
        def _(br=br):
            _sa_branch((br + 1) * T, tbl_ref, q_ref, ckv_ref, iw_ref, ikk_ref, wuk_ref, wuv_ref, bn_ref,
                       iqm_ref, keys_ref, madd_ref, oh_ref, last_ref)

    for h in range(H):
        o_ref[0, :, h * HEAD_DIM:(h + 1) * HEAD_DIM] = oh_ref[h].astype(_BF)


def _sparse_attn(q, ckv, iq, ikk, iw, wuk, wuv, rel_table, bn):
    B, H, S, _ = q.shape
    T = ATT_TILE
    return pl.pallas_call(
        _sa_kernel,
        grid=(B, S // T),
        in_specs=[pl.BlockSpec(memory_space=pltpu.SMEM),
                  pl.BlockSpec((1, H, T, HEAD_DIM), lambda b, i: (b, 0, i, 0)),
                  pl.BlockSpec((1, S, SA_LATENT), lambda b, i: (b, 0, 0)),
                  pl.BlockSpec((1, T, 512), lambda b, i: (b, i, 0)),
                  pl.BlockSpec((1, S, 128), lambda b, i: (b, 0, 0)),
                  pl.BlockSpec((1, T, 128), lambda b, i: (b, i, 0)),
                  _full((H, HEAD_DIM, SA_LATENT)),
                  _full((H, SA_LATENT, HEAD_DIM)),
                  _full((H, T, NEAR_W))],
        out_specs=pl.BlockSpec((1, T, H * HEAD_DIM), lambda b, i: (b, i, 0)),
        out_shape=jax.ShapeDtypeStruct((B, S, H * HEAD_DIM), _BF),
        scratch_shapes=[pltpu.VMEM((H, T, 128), _BF),
                        pltpu.VMEM((T, S), _I32),
                        pltpu.VMEM((T, S), _F32),
                        pltpu.VMEM((H, T, HEAD_DIM), _F32),
                        pltpu.VMEM((T, 1), _I32)],
        compiler_params=_params(("parallel", "arbitrary")),
        name="sparse_attn",
    )(rel_table, q, ckv, iq, ikk, iw, wuk, wuv, bn)


def _cv_kernel(x_ref, g_ref, sc_ref, sh_ref, w1_ref, b1_ref, wdw_ref, bdw_ref, lg_ref, lb_ref, w2_ref, b2_ref,
               g1_ref, o_ref, ext_ref, y_ref):
    i = pl.program_id(1)
    tm = x_ref.shape[1]
    D = x_ref.shape[2]
    n_ext = CONV_HALO + tm

    @pl.when(i == 0)
    def _():
        ext_ref[0, 0:CONV_HALO] = jnp.zeros((CONV_HALO, D), _F32)

    @pl.when(i > 0)
    def _():
        ext_ref[0, 0:CONV_HALO] = ext_ref[0, tm:n_ext]

    hn = _rms_mod(x_ref[0], g_ref[...], sc_ref[0], sh_ref[0]).astype(_BF)
    tn = 512
    n = D // tn
    lo = lambda c: slice(c * tn, (c + 1) * tn)
    hi = lambda c: slice(D + c * tn, D + (c + 1) * tn)
    dots = lambda c: (_dot(hn, w1_ref[:, lo(c)]), _dot(hn, w1_ref[:, hi(c)]))
    nxt = dots(0)
    for c in range(n):
        cur = nxt
        if c + 1 < n:
            nxt = dots(c + 1)
        ext_ref[0, CONV_HALO:n_ext, lo(c)] = ((cur[0] + b1_ref[:, lo(c)])
                                              * jax.nn.sigmoid(cur[1] + b1_ref[:, hi(c)]))
    for r in range(1, SUBLANES):
        ext_ref[r, 0:n_ext - SUBLANES] = ext_ref[0, r:r + n_ext - SUBLANES]
    rb = CONV_ROWS
    base = CONV_HALO - (CONV_WIDTH - 1)

    def block(blk, carry):
        row0 = pl.multiple_of(blk * rb, rb)
        acc = jnp.broadcast_to(bdw_ref[...], (rb, bdw_ref.shape[1]))
        for k in range(CONV_WIDTH):
            r = (base + k) % SUBLANES
            al = base + k - r
            w = jnp.concatenate([wdw_ref[k]] * (rb // SUBLANES), axis=0)
            acc = acc + w * ext_ref[r, pl.ds(row0 + al, rb), :]
        y_ref[pl.ds(row0, rb)] = _silu(_layernorm(acc, lg_ref[...], lb_ref[...])).astype(_BF)
        return carry

    lax.fori_loop(0, tm // rb, block, 0, unroll=2)
    y = _dot(y_ref[...], w2_ref[...]) + b2_ref[...]
    o_ref[0] = x_ref[0] + g1_ref[0] * y


def _cv_mixer(x, g, sc, sh, w1, b1, w_dw, b_dw, ln_g, ln_b, w2, b2, g1):
    B, S, D = x.shape
    tm = 512
    vec = lambda: _full((1, D))
    return pl.pallas_call(
        _cv_kernel,
        grid=(B, S // tm),
        in_specs=_in_specs_common(tm, D) + [_full((D, 2 * D)), _full((1, 2 * D)),
                                            _full((CONV_WIDTH, SUBLANES, D)), vec(), vec(), vec(),
                                            _full((D, D)), vec(),
                                            pl.BlockSpec((1, 1, D), lambda b, i: (b, 0, 0))],
        out_specs=pl.BlockSpec((1, tm, D), lambda b, i: (b, i, 0)),
        out_shape=jax.ShapeDtypeStruct((B, S, D), _F32),
        scratch_shapes=[pltpu.VMEM((SUBLANES, CONV_HALO + tm, D), _F32), pltpu.VMEM((tm, D), _BF)],
        compiler_params=_params(("parallel", "arbitrary")),
        name="cv_mixer",
    )(x, g, sc, sh, w1, b1, w_dw, b_dw, ln_g, ln_b, w2, b2, g1)


def _out_kernel(a_ref, w_ref, x_ref, g1_ref, o_ref):
    o_ref[0] = x_ref[0] + g1_ref[0] * _dot(a_ref[0], w_ref[...])


def _out_proj(a, w, x, g1):
    B, S, D = x.shape
    K = a.shape[2]
    tm = PROJ_ROWS
    return pl.pallas_call(
        _out_kernel,
        grid=(B, S // tm),
        in_specs=[pl.BlockSpec((1, tm, K), lambda b, i: (b, i, 0)),
                  _full((K, D)),
                  pl.BlockSpec((1, tm, D), lambda b, i: (b, i, 0)),
                  pl.BlockSpec((1, 1, D), lambda b, i: (b, 0, 0))],
        out_specs=pl.BlockSpec((1, tm, D), lambda b, i: (b, i, 0)),
        out_shape=jax.ShapeDtypeStruct((B, S, D), _F32),
        compiler_params=_params(("parallel", "parallel")),
        name="out_proj",
    )(a, w, x, g1)


def _ffn_kernel(x_ref, xh_ref, g_ref, sc_ref, sh_ref, g2_ref, wup_ref, wdw_ref, bdw_ref, wdn_ref,
                fg_ref, o_ref, he_ref, act_ref, acc_ref, *, final):
    i = pl.program_id(1)
    tm = x_ref.shape[1]
    F = wdn_ref.shape[0]
    x = x_ref[0]
    he_ref[FFN_HALO:FFN_HALO + tm] = _rms_mod(x, g_ref[...], sc_ref[0], sh_ref[0]).astype(_BF)
    hh = _rms_mod(xh_ref[0], g_ref[...], sc_ref[0], sh_ref[0])
    he_ref[0:FFN_HALO] = jnp.where(i > 0, hh, jnp.zeros_like(hh)).astype(_BF)
    tf = FFN_CHUNK
    n_chunks = F // tf

    def up(c):
        he = he_ref[...]
        return (_dot(he, wup_ref[:, c * tf:(c + 1) * tf]), _dot(he, wup_ref[:, F + c * tf:F + (c + 1) * tf]))

    def conv(a, col0):
        cols = slice(col0, col0 + tf)
        w = wdw_ref[:, cols]
        return (a[FFN_HALO - 2:FFN_HALO - 2 + tm] * w[0:1]
                + a[FFN_HALO - 1:FFN_HALO - 1 + tm] * w[1:2]
                + a[FFN_HALO:FFN_HALO + tm] * w[2:3] + bdw_ref[:, cols])

    nxt = up(0)
    for c in range(n_chunks):
        cur = nxt
        if c + 1 < n_chunks:
            nxt = up(c + 1)
        act = _silu(conv(cur[0], c * tf)) * conv(cur[1], F + c * tf)
        act_ref[:, c * tf:(c + 1) * tf] = act.astype(_BF)
        if (c + 1) % FFN_DOWN_GROUP == 0 or c + 1 == n_chunks:
            lo = (c // FFN_DOWN_GROUP) * FFN_DOWN_GROUP * tf
            part = _dot(act_ref[:, lo:(c + 1) * tf], wdn_ref[lo:(c + 1) * tf, :])
            if lo == 0:
                acc_ref[...] = part
            else:
                acc_ref[...] += part
    xn = x + g2_ref[0] * acc_ref[...]
    if final:
        ms = jnp.mean(xn * xn, axis=-1, keepdims=True)
        xn = xn * lax.rsqrt(ms + EPS) * fg_ref[...]
    o_ref[0] = xn


def _ffn(x, g, sc, sh, g2, w_up, w_dw, b_dw, w_down, final_g, final):
    B, S, D = x.shape
    F = w_down.shape[0]
    tm = FFN_ROWS
    hb = tm // FFN_HALO
    kern = functools.partial(_ffn_kernel, final=final)
    mod = lambda: pl.BlockSpec((1, 1, D), lambda b, i: (b, 0, 0))
    return pl.pallas_call(
        kern,
        grid=(B, S // tm),
        in_specs=[pl.BlockSpec((1, tm, D), lambda b, i: (b, i, 0)),
                  pl.BlockSpec((1, FFN_HALO, D), lambda b, i: (b, jnp.maximum(i * hb - 1, 0), 0)),
                  _full((1, D)), mod(), mod(), mod(),
                  _full((D, 2 * F)), _full((3, 2 * F)), _full((1, 2 * F)), _full((F, D)),
                  _full((1, D))],
        out_specs=pl.BlockSpec((1, tm, D), lambda b, i: (b, i, 0)),
        out_shape=jax.ShapeDtypeStruct((B, S, D), _F32),
        scratch_shapes=[pltpu.VMEM((FFN_HALO + tm, D), _BF), pltpu.VMEM((tm, F), _BF),
                        pltpu.VMEM((tm, D), _F32)],
        compiler_params=_params(("parallel", "parallel")),
        name="ffn",
    )(x, x, g, sc, sh, g2, w_up, w_dw, b_dw, w_down, final_g)


def kernel(x, c, rel_table, ada_w, ada_b, norm_g, final_g, da_w_in, da_lam, da_subln_g, da_w_out, cv_w_pw1, cv_b_pw1, cv_w_dw, cv_b_dw, cv_ln_g, cv_ln_b, cv_w_pw2, cv_b_pw2, sa_w_in, sa_kv_g, sa_w_uk, sa_w_uv, sa_w_out, sg_w_in, sg_b_in, sg_ln_g, sg_ln_b, sg_w_s, sg_b_s, sg_w_out, sg_b_out, ff_w_up, ff_w_dw, ff_b_dw, ff_w_down):
    B, S, D = x.shape
    depth = ada_w.shape[0]
    n_mixers = 4
    H = N_HEADS
    mods = _ada(c, ada_w, ada_b)
    bn = _bias_tiles(rel_table)
    row = lambda v: v.reshape(1, -1)

    for layer in range(depth):
        kind = layer % n_mixers
        j = layer // n_mixers
        sh1, sc1, g1, sh2, sc2, g2 = [m.reshape(B, 1, D) for m in jnp.split(mods[layer], 6, axis=-1)]
        ng1 = row(norm_g[layer, 0])
        if kind == 0:
            lambda_init = 0.8 - 0.6 * math.exp(-0.3 * layer)
            qkv = _in_da(x, ng1, sc1, sh1, da_w_in[j].astype(_BF))
            o = _diff_attn(qkv, rel_table, bn, da_lam[j], row(da_subln_g[j]), lambda_init)
            x = _out_proj(o, da_w_out[j].astype(_BF), x, g1)
        elif kind == 1:
            w_dw = jnp.broadcast_to(cv_w_dw[j][:, None, :], (CONV_WIDTH, SUBLANES, D))
            x = _cv_mixer(x, ng1, sc1, sh1, cv_w_pw1[j].astype(_BF), row(cv_b_pw1[j]), w_dw, row(cv_b_dw[j]),
                          row(cv_ln_g[j]), row(cv_ln_b[j]), cv_w_pw2[j].astype(_BF), row(cv_b_pw2[j]), g1)
        elif kind == 2:
            w = sa_w_in[j]
            o1 = H * HEAD_DIM
            o2 = o1 + SA_LATENT
            o3 = o2 + H * IDX_DIM
            o4 = o3 + IDX_DIM
            w_ik = w[:, o3:o4]
            w_iw = jnp.pad(w[:, o4:], ((0, 0), (0, 128 - (w.shape[1] - o4))))
            w_cat = jnp.concatenate([w[:, :o3], w_ik, w_ik, w_iw], axis=1).astype(_BF)
            q, ckv, iq, ikk, iw = _in_sa(x, ng1, sc1, sh1, w_cat, row(sa_kv_g[j]))
            wuk = jnp.transpose(sa_w_uk[j], (1, 2, 0)).astype(_BF)
            wuv = jnp.transpose(sa_w_uv[j], (1, 0, 2)).astype(_BF)
            o = _sparse_attn(q, ckv, iq, ikk, iw, wuk, wuv, rel_table, bn)
            x = _out_proj(o, sa_w_out[j].astype(_BF), x, g1)
        else:
            x = _sg_mixer(x, ng1, sc1, sh1, sg_w_in[j].astype(_BF), row(sg_b_in[j]), row(sg_ln_g[j]),
                          row(sg_ln_b[j]), sg_w_s[j], sg_b_s[j].T, sg_w_out[j].astype(_BF),
                          row(sg_b_out[j]), g1)
        x = _ffn(x, row(norm_g[layer, 1]), sc2, sh2, g2, ff_w_up[layer].astype(_BF), ff_w_dw[layer],
                 row(ff_b_dw[layer]), ff_w_down[layer].astype(_BF), row(final_g), layer == depth - 1)
    return x
```

```python
import functools
import math

import jax
import jax.numpy as jnp
import numpy as np
from jax import lax
from jax.experimental import pallas as pl
from jax.experimental.pallas import tpu as pltpu

_BF = jnp.bfloat16
_F32 = jnp.float32
_I32 = jnp.int32

EPS = 1e-6
SUBLN_EPS = 1e-5
NEG = -1e30
CHUNK = 64
CHUNK_SHIFT = 6
N_HEADS = 8
HEAD_DIM = 128
DA_DIM = 64
SA_LATENT = 256
IDX_DIM = 64
TOPK = 256
N_BUCKETS = 32
MAX_DISTANCE = 128
SG_CHUNK = 128
SG_GROUPS = 8
CONV_WIDTH = 31
CONV_HALO = 32
CONV_ROWS = 8
CONV_UNROLL = 64
SUBLANES = 8
FFN_HALO = 8
PROJ_ROWS = 1024
SG_ROWS = 1024
FFN_ROWS = 1024
FFN_CHUNK = 256
FFN_DOWN_GROUP = 11
ATT_TILE = 512
DA_ROWS = 128
SA_ROWS = 256
NEAR_W = 768
VMEM_LIMIT = 56 * 1024 * 1024


def _dot(a, b):
    return jnp.dot(a, b, preferred_element_type=_F32)


def _dot_nt(a, b):
    return lax.dot_general(a, b, (((1,), (1,)), ((), ())), preferred_element_type=_F32)


def _rms_mod(x, g, sc, sh):
    ms = jnp.mean(x * x, axis=-1, keepdims=True)
    return (x * lax.rsqrt(ms + EPS) * g) * (1.0 + sc) + sh


def _layernorm(x, g, b):
    mu = jnp.mean(x, axis=-1, keepdims=True)
    xc = x - mu
    var = jnp.mean(xc * xc, axis=-1, keepdims=True)
    return xc * lax.rsqrt(var + EPS) * g + b


def _silu(x):
    return x * jax.nn.sigmoid(x)


def _params(sem):
    return pltpu.CompilerParams(dimension_semantics=sem, vmem_limit_bytes=VMEM_LIMIT)


def _full(shape):
    n = len(shape)
    return pl.BlockSpec(shape, lambda *_: (0,) * n, pipeline_mode=pl.Buffered(1))


def _ada_kernel(c_ref, w_ref, b_ref, o_ref):
    ca = _silu(c_ref[...]).astype(_BF)
    o_ref[0] = _dot(ca, w_ref[0].astype(_BF)) + b_ref[0]


def _ada(c, ada_w, ada_b):
    L, D, N = ada_w.shape
    B = c.shape[0]
    tn = N // 4
    return pl.pallas_call(
        _ada_kernel,
        grid=(L, N // tn),
        in_specs=[pl.BlockSpec((B, D), lambda l, j: (0, 0)),
                  pl.BlockSpec((1, D, tn), lambda l, j: (l, 0, j)),
                  pl.BlockSpec((1, 1, tn), lambda l, j: (l, 0, j))],
        out_specs=pl.BlockSpec((1, B, tn), lambda l, j: (l, 0, j)),
        out_shape=jax.ShapeDtypeStruct((L, B, N), _F32),
        compiler_params=_params(("arbitrary", "arbitrary")),
        name="ada",
    )(c, ada_w, ada_b.reshape(L, 1, N))


def _bias_kernel(tbl_ref, bn_ref):
    h = pl.program_id(0)
    shape = bn_ref.shape[1:]
    i = lax.broadcasted_iota(_I32, shape, 0)
    j = lax.broadcasted_iota(_I32, shape, 1) - (NEAR_W - ATT_TILE)
    nb = N_BUCKETS // 2
    max_exact = nb // 2
    rel = j - i
    ret = jnp.where(rel > 0, nb, 0)
    n = jnp.abs(rel)
    nf = jnp.maximum(n, 1).astype(_F32)
    large = max_exact + (jnp.log(nf / max_exact) / math.log(MAX_DISTANCE / max_exact)
                         * (nb - max_exact)).astype(_I32)
    large = jnp.minimum(large, nb - 1)
    bucket = ret + jnp.where(n < max_exact, n, large)
    out = jnp.zeros(shape, _F32)
    for bk in range(N_BUCKETS):
        out = jnp.where(bucket == bk, tbl_ref[bk, h], out)
    visible = (j >> CHUNK_SHIFT) <= (i >> CHUNK_SHIFT)
    bn_ref[0] = jnp.where(visible, out, NEG)


def _bias_tiles(rel_table):
    H = rel_table.shape[1]
    return pl.pallas_call(
        _bias_kernel,
        grid=(H,),
        in_specs=[pl.BlockSpec(memory_space=pltpu.SMEM)],
        out_specs=pl.BlockSpec((1, ATT_TILE, NEAR_W), lambda h: (h, 0, 0)),
        out_shape=jax.ShapeDtypeStruct((H, ATT_TILE, NEAR_W), _F32),
        compiler_params=_params(("arbitrary",)),
        name="bias_tiles",
    )(rel_table)


def _in_specs_common(tm, D):
    return [pl.BlockSpec((1, tm, D), lambda b, i: (b, i, 0)),
            pl.BlockSpec((1, D), lambda b, i: (0, 0)),
            pl.BlockSpec((1, 1, D), lambda b, i: (b, 0, 0)),
            pl.BlockSpec((1, 1, D), lambda b, i: (b, 0, 0))]


def _in_da_kernel(x_ref, g_ref, sc_ref, sh_ref, w_ref, o_ref):
    hn = _rms_mod(x_ref[0], g_ref[...], sc_ref[0], sh_ref[0]).astype(_BF)
    N = w_ref.shape[1]
    tn = 512
    for c in range(N // tn):
        o_ref[0, :, c * tn:(c + 1) * tn] = _dot(hn, w_ref[:, c * tn:(c + 1) * tn]).astype(_BF)


def _in_da(x, g, sc, sh, w):
    B, S, D = x.shape
    N = w.shape[1]
    tm = PROJ_ROWS
    return pl.pallas_call(
        _in_da_kernel,
        grid=(B, S // tm),
        in_specs=_in_specs_common(tm, D) + [_full((D, N))],
        out_specs=pl.BlockSpec((1, tm, N), lambda b, i: (b, i, 0)),
        out_shape=jax.ShapeDtypeStruct((B, S, N), _BF),
        compiler_params=_params(("parallel", "parallel")),
        name="in_da",
    )(x, g, sc, sh, w)


def _in_sa_kernel(x_ref, g_ref, sc_ref, sh_ref, w_ref, kvg_ref,
                  q_ref, ckv_ref, iq_ref, ikk_ref, iw_ref):
    hn = _rms_mod(x_ref[0], g_ref[...], sc_ref[0], sh_ref[0]).astype(_BF)
    H = q_ref.shape[1]
    for c in range(H // 2):
        r = _dot(hn, w_ref[:, c * 256:(c + 1) * 256])
        q_ref[0, 2 * c] = r[:, :HEAD_DIM].astype(_BF)
        q_ref[0, 2 * c + 1] = r[:, HEAD_DIM:].astype(_BF)
    o = H * HEAD_DIM
    ckv = _dot(hn, w_ref[:, o:o + SA_LATENT])
    ms = jnp.mean(ckv * ckv, axis=-1, keepdims=True)
    ckv_ref[0] = (ckv * lax.rsqrt(ms + EPS) * kvg_ref[...]).astype(_BF)
    o += SA_LATENT
    iq_ref[0] = (_dot(hn, w_ref[:, o:o + 512]) * (IDX_DIM ** -0.5)).astype(_BF)
    o += 512
    r = _dot(hn, w_ref[:, o:o + 256])
    ikk_ref[0] = r[:, :128].astype(_BF)
    iw_ref[0] = r[:, 128:] * (N_HEADS ** -0.5)


def _in_sa(x, g, sc, sh, w, kv_g):
    B, S, D = x.shape
    N = w.shape[1]
    H = N_HEADS
    tm = PROJ_ROWS
    row = lambda n: pl.BlockSpec((1, tm, n), lambda b, i: (b, i, 0))
    return pl.pallas_call(
        _in_sa_kernel,
        grid=(B, S // tm),
        in_specs=_in_specs_common(tm, D) + [_full((D, N)), _full((1, SA_LATENT))],
        out_specs=[pl.BlockSpec((1, H, tm, HEAD_DIM), lambda b, i: (b, 0, i, 0)),
                   row(SA_LATENT), row(512), row(128), row(128)],
        out_shape=[jax.ShapeDtypeStruct((B, H, S, HEAD_DIM), _BF),
                   jax.ShapeDtypeStruct((B, S, SA_LATENT), _BF),
                   jax.ShapeDtypeStruct((B, S, 512), _BF),
                   jax.ShapeDtypeStruct((B, S, 128), _BF),
                   jax.ShapeDtypeStruct((B, S, 128), _F32)],
        compiler_params=_params(("parallel", "parallel")),
        name="in_sa",
    )(x, g, sc, sh, w, kv_g)


def _sg_kernel(x_ref, g_ref, sc_ref, sh_ref, w_ref, b_ref, lg_ref, lb_ref, ws_ref, bs_ref, wo_ref, bo_ref,
               g1_ref, o_ref, a_scr, v_scr, z_scr):
    x = x_ref[0]
    hn = _rms_mod(x, g_ref[...], sc_ref[0], sh_ref[0]).astype(_BF)
    tm = x.shape[0]
    W = a_scr.shape[1]
    G = ws_ref.shape[0]
    C = SG_CHUNK
    gw = W // G
    tn = 512
    n = W // tn
    lo = lambda c: slice(c * tn, (c + 1) * tn)
    hi = lambda c: slice(W + c * tn, W + (c + 1) * tn)
    ti = lax.broadcasted_iota(_I32, (C, C), 0)
    si = lax.broadcasted_iota(_I32, (C, C), 1)
    cols = [hi(c) for c in range(n)] + [lo(c) for c in range(n)]
    nxt = _dot(hn, w_ref[:, cols[0]])
    for t in range(2 * n):
        cur = nxt
        if t + 1 < 2 * n:
            nxt = _dot(hn, w_ref[:, cols[t + 1]])
        act = jax.nn.gelu(cur + b_ref[:, cols[t]], approximate=True)
        if t < n:
            a_scr[:, lo(t)] = act
            if t == n - 1:
                v_scr[...] = _layernorm(a_scr[...], lg_ref[...], lb_ref[...]).astype(_BF)
            continue
        for gl in range(tn // gw):
            g = (t - n) * (tn // gw) + gl
            wg = jnp.where(ti >= si, ws_ref[g], 0.0).astype(_BF)
            bcol = bs_ref[:, g:g + 1]
            gcols = slice(g * gw, (g + 1) * gw)
            for r in range(tm // C):
                rows = slice(r * C, (r + 1) * C)
                sv = _dot(wg, v_scr[rows, gcols]) + bcol
                z_scr[rows, gcols] = (act[rows, gl * gw:(gl + 1) * gw] * sv).astype(_BF)
    y = _dot(z_scr[...], wo_ref[...]) + bo_ref[...]
    o_ref[0] = x + g1_ref[0] * y


def _sg_mixer(x, g, sc, sh, w, b, ln_g, ln_b, w_s, b_s_t, w_out, b_out, g1):
    B, S, D = x.shape
    N = w.shape[1]
    W = N // 2
    G = w_s.shape[0]
    tm = SG_ROWS
    return pl.pallas_call(
        _sg_kernel,
        grid=(B, S // tm),
        in_specs=_in_specs_common(tm, D) + [_full((D, N)), _full((1, N)), _full((1, W)), _full((1, W)),
                                            _full((G, SG_CHUNK, SG_CHUNK)), _full((SG_CHUNK, G)),
                                            _full((W, D)), _full((1, D)),
                                            pl.BlockSpec((1, 1, D), lambda b, i: (b, 0, 0))],
        out_specs=pl.BlockSpec((1, tm, D), lambda b, i: (b, i, 0)),
        out_shape=jax.ShapeDtypeStruct((B, S, D), _F32),
        scratch_shapes=[pltpu.VMEM((tm, W), _F32), pltpu.VMEM((tm, W), _BF), pltpu.VMEM((tm, W), _BF)],
        compiler_params=_params(("parallel", "parallel")),
        name="sg_mixer",
    )(x, g, sc, sh, w, b, ln_g, ln_b, w_s, b_s_t, w_out, b_out, g1)


def _attend(qs, k_ref, v_ref, W, fars, near_biases, block_rows, madd_ref=None):
    nw = min(W, NEAR_W)
    fw = W - nw
    lo = NEAR_W - nw
    n_rows = qs[0].shape[0]
    blocks = [slice(r, r + block_rows) for r in range(0, n_rows, block_rows)]
    logits = []
    for q, far, near_bias in zip(qs, fars, near_biases):
        for rows in blocks:
            we = W - (n_rows - rows.stop)
            s_n = _dot_nt(q[rows], k_ref[0, fw:we, :]) + near_bias(rows, lo, lo + we - fw)
            s_f = None
            if fw:
                s_f = _dot_nt(q[rows], k_ref[0, 0:fw, :]) + far
            if madd_ref is not None:
                s_n = s_n + madd_ref[rows, fw:we]
                if fw:
                    s_f = s_f + madd_ref[rows, 0:fw]
            logits.append((s_n, s_f, we))
    outs = []
    for s_n, s_f, we in logits:
        m = jnp.max(s_n, axis=1, keepdims=True)
        if fw:
            m = jnp.maximum(m, jnp.max(s_f, axis=1, keepdims=True))
            p_f = jnp.exp(s_f - m)
        p_n = jnp.exp(s_n - m)
        l = jnp.sum(p_n, axis=1, keepdims=True)
        o = _dot(p_n.astype(_BF), v_ref[0, fw:we, :])
        if fw:
            l = l + jnp.sum(p_f, axis=1, keepdims=True)
            o = o + _dot(p_f.astype(_BF), v_ref[0, 0:fw, :])
        outs.append(o * (1.0 / l))
    nb = len(blocks)
    return [jnp.concatenate(outs[i * nb:(i + 1) * nb], axis=0) for i in range(len(qs))]


def _da_kernel(tbl_ref, q_ref, k_ref, v_ref, bn_ref, lam_ref, g_ref, o_ref, *, lambda_init):
    h = pl.program_id(1)
    a = pl.program_id(2)
    T = q_ref.shape[1]
    S = k_ref.shape[1]
    lane = lax.broadcasted_iota(_I32, (T, HEAD_DIM), 1)
    q = q_ref[0] * (DA_DIM ** -0.5)
    zero = jnp.zeros_like(q)
    q1 = jnp.where(lane < DA_DIM, q, zero)
    q2 = jnp.where(lane >= DA_DIM, q, zero)
    far = tbl_ref[N_BUCKETS // 2 - 1, h]
    lam = lam_ref[...]
    lam_full = (jnp.exp(jnp.sum(lam[0:1] * lam[1:2], axis=1, keepdims=True))
                - jnp.exp(jnp.sum(lam[2:3] * lam[3:4], axis=1, keepdims=True)) + lambda_init)
    near_bias = lambda rows, lo, hi: bn_ref[0, rows, lo:hi]

    for br in range(S // T):
        @pl.when(a == br)
        def _(br=br):
            W = (br + 1) * T
            o1, o2 = _attend([q1, q2], k_ref, v_ref, W, [far, far], [near_bias, near_bias], DA_ROWS)
            o = o1 - lam_full * o2
            ms = jnp.mean(o * o, axis=-1, keepdims=True)
            o = (o * lax.rsqrt(ms + SUBLN_EPS) * g_ref[...]) * (1.0 - lambda_init)
            o_ref[0] = o.astype(_BF)


def _diff_attn(qkv, rel_table, bn, lam, subln_g, lambda_init):
    B, S, _ = qkv.shape
    H = N_HEADS
    T = ATT_TILE
    kern = functools.partial(_da_kernel, lambda_init=lambda_init)
    return pl.pallas_call(
        kern,
        grid=(B, H, S // T),
        in_specs=[pl.BlockSpec(memory_space=pltpu.SMEM),
                  pl.BlockSpec((1, T, HEAD_DIM), lambda b, h, i: (b, i, h)),
                  pl.BlockSpec((1, S, HEAD_DIM), lambda b, h, i: (b, 0, H + h)),
                  pl.BlockSpec((1, S, HEAD_DIM), lambda b, h, i: (b, 0, 2 * H + h)),
                  pl.BlockSpec((1, T, NEAR_W), lambda b, h, i: (h, 0, 0)),
                  pl.BlockSpec((4, DA_DIM), lambda b, h, i: (0, 0)),
                  pl.BlockSpec((1, HEAD_DIM), lambda b, h, i: (0, 0))],
        out_specs=pl.BlockSpec((1, T, HEAD_DIM), lambda b, h, i: (b, i, h)),
        out_shape=jax.ShapeDtypeStruct((B, S, H * HEAD_DIM), _BF),
        compiler_params=_params(("parallel", "parallel", "arbitrary")),
        name="diff_attn",
    )(rel_table, qkv, qkv, qkv, bn, lam, subln_g)


_INT_MIN = np.int32(-2 ** 31)
_KEY_NEG_INF = np.int32(np.array(-np.inf, np.float32).view(np.int32) ^ np.int32(0x7FFFFFFF))
SEARCH_GROUPS = 4
HEADS_PER_STEP = 2


def _sa_branch(W, tbl_ref, q_ref, ckv_ref, iw_ref, ikk_ref, wuk_ref, wuv_ref, bn_ref,
               iqm_ref, keys_ref, madd_ref, oh_ref, last_ref):
    T = iw_ref.shape[1]
    H = q_ref.shape[1]
    iw = iw_ref[0]

    for cb in range(W // T):
        cols = slice(cb * T, (cb + 1) * T)
        ikt = ikk_ref[0, cols, :]
        score = jnp.zeros((T, T), _F32)
        for hh in range(H):
            score = score + jnp.maximum(_dot_nt(iqm_ref[hh], ikt), 0.0) * iw[:, hh:hh + 1]
        score = jnp.where(score == 0.0, 0.0, score)
        if cb == W // T - 1:
            row = lax.broadcasted_iota(_I32, (T, T), 0)
            col = lax.broadcasted_iota(_I32, (T, T), 1)
            score = jnp.where((col >> CHUNK_SHIFT) <= (row >> CHUNK_SHIFT), score, -jnp.inf)
        bits = lax.bitcast_convert_type(score, _I32)
        keys_ref[:, cols] = bits ^ ((bits >> 31) & np.int32(0x7FFFFFFF))

    def count(pred, rows=slice(None)):
        return jnp.sum(jnp.where(pred(keys_ref[rows, 0:W]), 1.0, 0.0), axis=1, keepdims=True)

    R = T // SEARCH_GROUPS

    def bit_body(it, thrs):
        bit = lax.shift_left(np.int32(1), np.int32(31) - it)
        out = []
        for g in range(SEARCH_GROUPS):
            cand_u = thrs[g] | bit
            cand = cand_u ^ _INT_MIN
            cnt = count(lambda k: k >= cand, slice(g * R, (g + 1) * R))
            out.append(jnp.where(cnt >= float(TOPK), cand_u, thrs[g]))
        return tuple(out)

    thrs = lax.fori_loop(0, 32, bit_body, tuple(jnp.zeros((R, 1), _I32) for _ in range(SEARCH_GROUPS)))
    thr = jnp.concatenate(thrs, axis=0) ^ _INT_MIN

    cnt_gt = count(lambda k: k > thr)
    cnt_ge = count(lambda k: k >= thr)
    need = float(TOPK) - cnt_gt
    last_ref[...] = jnp.full((T, 1), 2 ** 30, _I32)
    tied = jnp.logical_and(cnt_ge - cnt_gt > need, thr > _KEY_NEG_INF)

    @pl.when(jnp.max(jnp.where(tied, 1.0, 0.0)) > 0.0)
    def _():
        col = lax.broadcasted_iota(_I32, (T, W), 1)

        def idx_body(it, v):
            cand = v | lax.shift_left(np.int32(1), np.int32(10) - it)
            below = count(lambda k: jnp.logical_and(k == thr, col < cand))
            return jnp.where(below < need, cand, v)
        last_ref[...] = lax.fori_loop(0, 11, idx_body, jnp.zeros((T, 1), _I32))

    last = last_ref[...]
    k = keys_ref[:, 0:W]
    col = lax.broadcasted_iota(_I32, (T, W), 1)
    sel = jnp.logical_or(k > thr, jnp.logical_and(k == thr, col <= last))
    madd_ref[:, 0:W] = jnp.where(sel, 0.0, NEG)

    def head_body(hp, carry):
        hs = [hp * HEADS_PER_STEP + u for u in range(HEADS_PER_STEP)]
        qlats = [(_dot(q_ref[0, h], wuk_ref[h]) * (HEAD_DIM ** -0.5)).astype(_BF) for h in hs]
        olats = _attend(qlats, ckv_ref, ckv_ref, W,
                        [tbl_ref[N_BUCKETS // 2 - 1, h] for h in hs],
                        [functools.partial(lambda h, rows, lo, hi: bn_ref[h, rows, lo:hi], h) for h in hs],
                        SA_ROWS, madd_ref)
        for h, olat in zip(hs, olats):
            oh_ref[h] = _dot(olat.astype(_BF), wuv_ref[h])
        return carry

    lax.fori_loop(0, H // HEADS_PER_STEP, head_body, 0)


def _sa_kernel(tbl_ref, q_ref, ckv_ref, iq_ref, ikk_ref, iw_ref, wuk_ref, wuv_ref, bn_ref, wo_ref, x_ref, g1_ref,
               o_ref, iqm_ref, keys_ref, madd_ref, oh_ref, last_ref, oc_ref):
    a = pl.program_id(1)
    T = iq_ref.shape[1]
    S = ckv_ref.shape[1]
    H = q_ref.shape[1]
    lane = lax.broadcasted_iota(_I32, (T, 128), 1)
    for p in range(H // 2):
        iqp = iq_ref[0, :, p * 128:(p + 1) * 128]
        zero = jnp.zeros_like(iqp)
        iqm_ref[2 * p] = jnp.where(lane < IDX_DIM, iqp, zero)
        iqm_ref[2 * p + 1] = jnp.where(lane >= IDX_DIM, iqp, zero)

    for br in range(S // T):
        @pl.when(a == br)
        def _(br=br):
            _sa_branch((br + 1) * T, tbl_ref, q_ref, ckv_ref, iw_ref, ikk_ref, wuk_ref, wuv_ref, bn_ref,
                       iqm_ref, keys_ref, madd_ref, oh_ref, last_ref)

    for h in range(H):
        oc_ref[:, h * HEAD_DIM:(h + 1) * HEAD_DIM] = oh_ref[h].astype(_BF)
    o_ref[0] = x_ref[0] + g1_ref[0] * _dot(oc_ref[...], wo_ref[...])


def _sparse_attn(q, ckv, iq, ikk, iw, wuk, wuv, rel_table, bn, w_out, x, g1):
    B, H, S, _ = q.shape
    T = ATT_TILE
    D = x.shape[2]
    return pl.pallas_call(
        _sa_kernel,
        grid=(B, S // T),
        in_specs=[pl.BlockSpec(memory_space=pltpu.SMEM),
                  pl.BlockSpec((1, H, T, HEAD_DIM), lambda b, i: (b, 0, i, 0)),
                  pl.BlockSpec((1, S, SA_LATENT), lambda b, i: (b, 0, 0)),
                  pl.BlockSpec((1, T, 512), lambda b, i: (b, i, 0)),
                  pl.BlockSpec((1, S, 128), lambda b, i: (b, 0, 0)),
                  pl.BlockSpec((1, T, 128), lambda b, i: (b, i, 0)),
                  _full((H, HEAD_DIM, SA_LATENT)),
                  _full((H, SA_LATENT, HEAD_DIM)),
                  _full((H, T, NEAR_W)),
                  _full((H * HEAD_DIM, D)),
                  pl.BlockSpec((1, T, D), lambda b, i: (b, i, 0)),
                  pl.BlockSpec((1, 1, D), lambda b, i: (b, 0, 0))],
        out_specs=pl.BlockSpec((1, T, D), lambda b, i: (b, i, 0)),
        out_shape=jax.ShapeDtypeStruct((B, S, D), _F32),
        scratch_shapes=[pltpu.VMEM((H, T, 128), _BF),
                        pltpu.VMEM((T, S), _I32),
                        pltpu.VMEM((T, S), _F32),
                        pltpu.VMEM((H, T, HEAD_DIM), _F32),
                        pltpu.VMEM((T, 1), _I32),
                        pltpu.VMEM((T, H * HEAD_DIM), _BF)],
        compiler_params=_params(("parallel", "arbitrary")),
        name="sparse_attn",
    )(rel_table, q, ckv, iq, ikk, iw, wuk, wuv, bn, w_out, x, g1)


def _cv_kernel(x_ref, g_ref, sc_ref, sh_ref, w1_ref, b1_ref, wdw_ref, bdw_ref, lg_ref, lb_ref, w2_ref, b2_ref,
               g1_ref, o_ref, ext_ref, y_ref):
    i = pl.program_id(1)
    tm = x_ref.shape[1]
    D = x_ref.shape[2]
    n_ext = CONV_HALO + tm

    @pl.when(i == 0)
    def _():
        ext_ref[0, 0:CONV_HALO] = jnp.zeros((CONV_HALO, D), _F32)

    @pl.when(i > 0)
    def _():
        ext_ref[0, 0:CONV_HALO] = ext_ref[0, tm:n_ext]

    hn = _rms_mod(x_ref[0], g_ref[...], sc_ref[0], sh_ref[0]).astype(_BF)
    tn = 512
    n = D // tn
    lo = lambda c: slice(c * tn, (c + 1) * tn)
    hi = lambda c: slice(D + c * tn, D + (c + 1) * tn)
    dots = lambda c: (_dot(hn, w1_ref[:, lo(c)]), _dot(hn, w1_ref[:, hi(c)]))
    nxt = dots(0)
    for c in range(n):
        cur = nxt
        if c + 1 < n:
            nxt = dots(c + 1)
        ext_ref[0, CONV_HALO:n_ext, lo(c)] = ((cur[0] + b1_ref[:, lo(c)])
                                              * jax.nn.sigmoid(cur[1] + b1_ref[:, hi(c)]))
    for r in range(1, SUBLANES):
        ext_ref[r, 0:n_ext - SUBLANES] = ext_ref[0, r:r + n_ext - SUBLANES]
    rb = CONV_ROWS
    base = CONV_HALO - (CONV_WIDTH - 1)

    def block(blk, carry):
        row0 = pl.multiple_of(blk * rb, rb)
        ns = rb // SUBLANES
        accs = [jnp.broadcast_to(bdw_ref[...], (SUBLANES, bdw_ref.shape[1]))] * ns
        for k in range(CONV_WIDTH):
            r = (base + k) % SUBLANES
            al = base + k - r
            w = wdw_ref[k]
            xk = ext_ref[r, pl.ds(row0 + al, rb), :]
            accs = [a + w * xk[s * SUBLANES:(s + 1) * SUBLANES] for s, a in enumerate(accs)]
        acc = jnp.concatenate(accs, axis=0)
        y_ref[pl.ds(row0, rb)] = _silu(_layernorm(acc, lg_ref[...], lb_ref[...])).astype(_BF)
        return carry

    lax.fori_loop(0, tm // rb, block, 0, unroll=CONV_UNROLL)
    y = _dot(y_ref[...], w2_ref[...]) + b2_ref[...]
    o_ref[0] = x_ref[0] + g1_ref[0] * y


def _cv_mixer(x, g, sc, sh, w1, b1, w_dw, b_dw, ln_g, ln_b, w2, b2, g1):
    B, S, D = x.shape
    tm = 512
    vec = lambda: _full((1, D))
    return pl.pallas_call(
        _cv_kernel,
        grid=(B, S // tm),
        in_specs=_in_specs_common(tm, D) + [_full((D, 2 * D)), _full((1, 2 * D)),
                                            _full((CONV_WIDTH, SUBLANES, D)), vec(), vec(), vec(),
                                            _full((D, D)), vec(),
                                            pl.BlockSpec((1, 1, D), lambda b, i: (b, 0, 0))],
        out_specs=pl.BlockSpec((1, tm, D), lambda b, i: (b, i, 0)),
        out_shape=jax.ShapeDtypeStruct((B, S, D), _F32),
        scratch_shapes=[pltpu.VMEM((SUBLANES, CONV_HALO + tm, D), _F32), pltpu.VMEM((tm, D), _BF)],
        compiler_params=_params(("parallel", "arbitrary")),
        name="cv_mixer",
    )(x, g, sc, sh, w1, b1, w_dw, b_dw, ln_g, ln_b, w2, b2, g1)


def _out_kernel(a_ref, w_ref, x_ref, g1_ref, o_ref):
    o_ref[0] = x_ref[0] + g1_ref[0] * _dot(a_ref[0], w_ref[...])


def _out_proj(a, w, x, g1):
    B, S, D = x.shape
    K = a.shape[2]
    tm = PROJ_ROWS
    return pl.pallas_call(
        _out_kernel,
        grid=(B, S // tm),
        in_specs=[pl.BlockSpec((1, tm, K), lambda b, i: (b, i, 0)),
                  _full((K, D)),
                  pl.BlockSpec((1, tm, D), lambda b, i: (b, i, 0)),
                  pl.BlockSpec((1, 1, D), lambda b, i: (b, 0, 0))],
        out_specs=pl.BlockSpec((1, tm, D), lambda b, i: (b, i, 0)),
        out_shape=jax.ShapeDtypeStruct((B, S, D), _F32),
        compiler_params=_params(("parallel", "parallel")),
        name="out_proj",
    )(a, w, x, g1)


def _ffn_kernel(x_ref, xh_ref, g_ref, sc_ref, sh_ref, g2_ref, wup_ref, wdw_ref, bdw_ref, wdn_ref,
                fg_ref, o_ref, he_ref, act_ref, acc_ref, *, final):
    i = pl.program_id(1)
    tm = x_ref.shape[1]
    F = wdn_ref.shape[0]
    x = x_ref[0]
    he_ref[FFN_HALO:FFN_HALO + tm] = _rms_mod(x, g_ref[...], sc_ref[0], sh_ref[0]).astype(_BF)
    hh = _rms_mod(xh_ref[0], g_ref[...], sc_ref[0], sh_ref[0])
    he_ref[0:FFN_HALO] = jnp.where(i > 0, hh, jnp.zeros_like(hh)).astype(_BF)
    tf = FFN_CHUNK
    n_chunks = F // tf

    def up(c):
        he = he_ref[...]
        return (_dot(he, wup_ref[:, c * tf:(c + 1) * tf]), _dot(he, wup_ref[:, F + c * tf:F + (c + 1) * tf]))

    def conv(a, col0):
        cols = slice(col0, col0 + tf)
        w = wdw_ref[:, cols]
        return (a[FFN_HALO - 2:FFN_HALO - 2 + tm] * w[0:1]
                + a[FFN_HALO - 1:FFN_HALO - 1 + tm] * w[1:2]
                + a[FFN_HALO:FFN_HALO + tm] * w[2:3] + bdw_ref[:, cols])

    nxt = up(0)
    for c in range(n_chunks):
        cur = nxt
        if c + 1 < n_chunks:
            nxt = up(c + 1)
        act = _silu(conv(cur[0], c * tf)) * conv(cur[1], F + c * tf)
        act_ref[:, c * tf:(c + 1) * tf] = act.astype(_BF)
        if (c + 1) % FFN_DOWN_GROUP == 0 or c + 1 == n_chunks:
            lo = (c // FFN_DOWN_GROUP) * FFN_DOWN_GROUP * tf
            part = _dot(act_ref[:, lo:(c + 1) * tf], wdn_ref[lo:(c + 1) * tf, :])
            if lo == 0:
                acc_ref[...] = part
            else:
                acc_ref[...] += part
    xn = x + g2_ref[0] * acc_ref[...]
    if final:
        ms = jnp.mean(xn * xn, axis=-1, keepdims=True)
        xn = xn * lax.rsqrt(ms + EPS) * fg_ref[...]
    o_ref[0] = xn


def _ffn(x, g, sc, sh, g2, w_up, w_dw, b_dw, w_down, final_g, final):
    B, S, D = x.shape
    F = w_down.shape[0]
    tm = FFN_ROWS
    hb = tm // FFN_HALO
    kern = functools.partial(_ffn_kernel, final=final)
    mod = lambda: pl.BlockSpec((1, 1, D), lambda b, i: (b, 0, 0))
    return pl.pallas_call(
        kern,
        grid=(B, S // tm),
        in_specs=[pl.BlockSpec((1, tm, D), lambda b, i: (b, i, 0)),
                  pl.BlockSpec((1, FFN_HALO, D), lambda b, i: (b, jnp.maximum(i * hb - 1, 0), 0)),
                  _full((1, D)), mod(), mod(), mod(),
                  _full((D, 2 * F)), _full((3, 2 * F)), _full((1, 2 * F)), _full((F, D)),
                  _full((1, D))],
        out_specs=pl.BlockSpec((1, tm, D), lambda b, i: (b, i, 0)),
        out_shape=jax.ShapeDtypeStruct((B, S, D), _F32),
        scratch_shapes=[pltpu.VMEM((FFN_HALO + tm, D), _BF), pltpu.VMEM((tm, F), _BF),
                        pltpu.VMEM((tm, D), _F32)],
        compiler_params=_params(("parallel", "parallel")),
        name="ffn",
    )(x, x, g, sc, sh, g2, w_up, w_dw, b_dw, w_down, final_g)


def kernel(x, c, rel_table, ada_w, ada_b, norm_g, final_g, da_w_in, da_lam, da_subln_g, da_w_out, cv_w_pw1, cv_b_pw1, cv_w_dw, cv_b_dw, cv_ln_g, cv_ln_b, cv_w_pw2, cv_b_pw2, sa_w_in, sa_kv_g, sa_w_uk, sa_w_uv, sa_w_out, sg_w_in, sg_b_in, sg_ln_g, sg_ln_b, sg_w_s, sg_b_s, sg_w_out, sg_b_out, ff_w_up, ff_w_dw, ff_b_dw, ff_w_down):
    B, S, D = x.shape
    depth = ada_w.shape[0]
    n_mixers = 4
    H = N_HEADS
    mods = _ada(c, ada_w, ada_b)
    bn = _bias_tiles(rel_table)
    row = lambda v: v.reshape(1, -1)

    for layer in range(depth):
        kind = layer % n_mixers
        j = layer // n_mixers
        sh1, sc1, g1, sh2, sc2, g2 = [m.reshape(B, 1, D) for m in jnp.split(mods[layer], 6, axis=-1)]
        ng1 = row(norm_g[layer, 0])
        if kind == 0:
            lambda_init = 0.8 - 0.6 * math.exp(-0.3 * layer)
            qkv = _in_da(x, ng1, sc1, sh1, da_w_in[j].astype(_BF))
            o = _diff_attn(qkv, rel_table, bn, da_lam[j], row(da_subln_g[j]), lambda_init)
            x = _out_proj(o, da_w_out[j].astype(_BF), x, g1)
        elif kind == 1:
            w_dw = jnp.broadcast_to(cv_w_dw[j][:, None, :], (CONV_WIDTH, SUBLANES, D))
            x = _cv_mixer(x, ng1, sc1, sh1, cv_w_pw1[j].astype(_BF), row(cv_b_pw1[j]), w_dw, row(cv_b_dw[j]),
                          row(cv_ln_g[j]), row(cv_ln_b[j]), cv_w_pw2[j].astype(_BF), row(cv_b_pw2[j]), g1)
        elif kind == 2:
            w = sa_w_in[j]
            o1 = H * HEAD_DIM
            o2 = o1 + SA_LATENT
            o3 = o2 + H * IDX_DIM
            o4 = o3 + IDX_DIM
            w_ik = w[:, o3:o4]
            w_iw = jnp.pad(w[:, o4:], ((0, 0), (0, 128 - (w.shape[1] - o4))))
            w_cat = jnp.concatenate([w[:, :o3], w_ik, w_ik, w_iw], axis=1).astype(_BF)
            q, ckv, iq, ikk, iw = _in_sa(x, ng1, sc1, sh1, w_cat, row(sa_kv_g[j]))
            wuk = jnp.transpose(sa_w_uk[j], (1, 2, 0)).astype(_BF)
            wuv = jnp.transpose(sa_w_uv[j], (1, 0, 2)).astype(_BF)
            x = _sparse_attn(q, ckv, iq, ikk, iw, wuk, wuv, rel_table, bn, sa_w_out[j].astype(_BF), x, g1)
        else:
            x = _sg_mixer(x, ng1, sc1, sh1, sg_w_in[j].astype(_BF), row(sg_b_in[j]), row(sg_ln_g[j]),
                          row(sg_ln_b[j]), sg_w_s[j], sg_b_s[j].T, sg_w_out[j].astype(_BF),
                          row(sg_b_out[j]), g1)
        x = _ffn(x, row(norm_g[layer, 1]), sc2, sh2, g2, ff_w_up[layer].astype(_BF), ff_w_dw[layer],
                 row(ff_b_dw[layer]), ff_w_down[layer].astype(_BF), row(final_g), layer == depth - 1)
    return x
```

```python
import functools
import math

import jax
import jax.numpy as jnp
import numpy as np
from jax import lax
from jax.experimental import pallas as pl
from jax.experimental.pallas import tpu as pltpu

_BF = jnp.bfloat16
_F32 = jnp.float32
_I32 = jnp.int32

EPS = 1e-6
SUBLN_EPS = 1e-5
NEG = -1e30
CHUNK = 64
CHUNK_SHIFT = 6
N_HEADS = 8
HEAD_DIM = 128
DA_DIM = 64
SA_LATENT = 256
IDX_DIM = 64
TOPK = 256
N_BUCKETS = 32
MAX_DISTANCE = 128
SG_CHUNK = 128
SG_GROUPS = 8
CONV_WIDTH = 31
CONV_HALO = 32
CONV_ROWS = 8
SUBLANES = 8
FFN_HALO = 8
PROJ_ROWS = 1024
SG_ROWS = 1024
FFN_ROWS = 1024
FFN_CHUNK = 256
FFN_DOWN_GROUP = 11
ATT_TILE = 512
DA_ROWS = 128
SA_ROWS = 256
NEAR_W = 768
VMEM_LIMIT = 56 * 1024 * 1024


def _dot(a, b):
    return jnp.dot(a, b, preferred_element_type=_F32)


def _dot_nt(a, b):
    return lax.dot_general(a, b, (((1,), (1,)), ((), ())), preferred_element_type=_F32)


def _rms_mod(x, g, sc, sh):
    ms = jnp.mean(x * x, axis=-1, keepdims=True)
    return (x * lax.rsqrt(ms + EPS) * g) * (1.0 + sc) + sh


def _layernorm(x, g, b):
    mu = jnp.mean(x, axis=-1, keepdims=True)
    xc = x - mu
    var = jnp.mean(xc * xc, axis=-1, keepdims=True)
    return xc * lax.rsqrt(var + EPS) * g + b


def _silu(x):
    return x * jax.nn.sigmoid(x)


def _params(sem):
    return pltpu.CompilerParams(dimension_semantics=sem, vmem_limit_bytes=VMEM_LIMIT)


def _full(shape):
    n = len(shape)
    return pl.BlockSpec(shape, lambda *_: (0,) * n, pipeline_mode=pl.Buffered(1))


def _ada_kernel(c_ref, w_ref, b_ref, o_ref):
    ca = _silu(c_ref[...]).astype(_BF)
    o_ref[0] = _dot(ca, w_ref[0].astype(_BF)) + b_ref[0]


def _ada(c, ada_w, ada_b):
    L, D, N = ada_w.shape
    B = c.shape[0]
    tn = N // 4
    return pl.pallas_call(
        _ada_kernel,
        grid=(L, N // tn),
        in_specs=[pl.BlockSpec((B, D), lambda l, j: (0, 0)),
                  pl.BlockSpec((1, D, tn), lambda l, j: (l, 0, j)),
                  pl.BlockSpec((1, 1, tn), lambda l, j: (l, 0, j))],
        out_specs=pl.BlockSpec((1, B, tn), lambda l, j: (l, 0, j)),
        out_shape=jax.ShapeDtypeStruct((L, B, N), _F32),
        compiler_params=_params(("arbitrary", "arbitrary")),
        name="ada",
    )(c, ada_w, ada_b.reshape(L, 1, N))


def _bias_kernel(tbl_ref, bn_ref):
    h = pl.program_id(0)
    shape = bn_ref.shape[1:]
    i = lax.broadcasted_iota(_I32, shape, 0)
    j = lax.broadcasted_iota(_I32, shape, 1) - (NEAR_W - ATT_TILE)
    nb = N_BUCKETS // 2
    max_exact = nb // 2
    rel = j - i
    ret = jnp.where(rel > 0, nb, 0)
    n = jnp.abs(rel)
    nf = jnp.maximum(n, 1).astype(_F32)
    large = max_exact + (jnp.log(nf / max_exact) / math.log(MAX_DISTANCE / max_exact)
                         * (nb - max_exact)).astype(_I32)
    large = jnp.minimum(large, nb - 1)
    bucket = ret + jnp.where(n < max_exact, n, large)
    out = jnp.zeros(shape, _F32)
    for bk in range(N_BUCKETS):
        out = jnp.where(bucket == bk, tbl_ref[bk, h], out)
    visible = (j >> CHUNK_SHIFT) <= (i >> CHUNK_SHIFT)
    bn_ref[0] = jnp.where(visible, out, NEG)


def _bias_tiles(rel_table):
    H = rel_table.shape[1]
    return pl.pallas_call(
        _bias_kernel,
        grid=(H,),
        in_specs=[pl.BlockSpec(memory_space=pltpu.SMEM)],
        out_specs=pl.BlockSpec((1, ATT_TILE, NEAR_W), lambda h: (h, 0, 0)),
        out_shape=jax.ShapeDtypeStruct((H, ATT_TILE, NEAR_W), _F32),
        compiler_params=_params(("arbitrary",)),
        name="bias_tiles",
    )(rel_table)


def _in_specs_common(tm, D):
    return [pl.BlockSpec((1, tm, D), lambda b, i: (b, i, 0)),
            pl.BlockSpec((1, D), lambda b, i: (0, 0)),
            pl.BlockSpec((1, 1, D), lambda b, i: (b, 0, 0)),
            pl.BlockSpec((1, 1, D), lambda b, i: (b, 0, 0))]


def _in_da_kernel(x_ref, g_ref, sc_ref, sh_ref, w_ref, o_ref):
    hn = _rms_mod(x_ref[0], g_ref[...], sc_ref[0], sh_ref[0]).astype(_BF)
    N = w_ref.shape[1]
    H = o_ref.shape[2]
    tn = 512
    per = tn // HEAD_DIM
    for c in range(N // tn):
        r = _dot(hn, w_ref[:, c * tn:(c + 1) * tn])
        for k in range(per):
            idx = c * per + k
            o_ref[0, idx // H, idx % H] = r[:, k * HEAD_DIM:(k + 1) * HEAD_DIM].astype(_BF)


def _in_da(x, g, sc, sh, w):
    B, S, D = x.shape
    N = w.shape[1]
    H = N_HEADS
    tm = PROJ_ROWS
    return pl.pallas_call(
        _in_da_kernel,
        grid=(B, S // tm),
        in_specs=_in_specs_common(tm, D) + [_full((D, N))],
        out_specs=pl.BlockSpec((1, 3, H, tm, HEAD_DIM), lambda b, i: (b, 0, 0, i, 0)),
        out_shape=jax.ShapeDtypeStruct((B, 3, H, S, HEAD_DIM), _BF),
        compiler_params=_params(("parallel", "parallel")),
        name="in_da",
    )(x, g, sc, sh, w)


def _in_sa_kernel(x_ref, g_ref, sc_ref, sh_ref, w_ref, kvg_ref,
                  q_ref, ckv_ref, iq_ref, ikk_ref, iw_ref):
    hn = _rms_mod(x_ref[0], g_ref[...], sc_ref[0], sh_ref[0]).astype(_BF)
    H = q_ref.shape[1]
    for c in range(H // 2):
        r = _dot(hn, w_ref[:, c * 256:(c + 1) * 256])
        q_ref[0, 2 * c] = r[:, :HEAD_DIM].astype(_BF)
        q_ref[0, 2 * c + 1] = r[:, HEAD_DIM:].astype(_BF)
    o = H * HEAD_DIM
    ckv = _dot(hn, w_ref[:, o:o + SA_LATENT])
    ms = jnp.mean(ckv * ckv, axis=-1, keepdims=True)
    ckv_ref[0] = (ckv * lax.rsqrt(ms + EPS) * kvg_ref[...]).astype(_BF)
    o += SA_LATENT
    iq_ref[0] = (_dot(hn, w_ref[:, o:o + 512]) * (IDX_DIM ** -0.5)).astype(_BF)
    o += 512
    r = _dot(hn, w_ref[:, o:o + 256])
    ikk_ref[0] = r[:, :128].astype(_BF)
    iw_ref[0] = r[:, 128:] * (N_HEADS ** -0.5)


def _in_sa(x, g, sc, sh, w, kv_g):
    B, S, D = x.shape
    N = w.shape[1]
    H = N_HEADS
    tm = PROJ_ROWS
    row = lambda n: pl.BlockSpec((1, tm, n), lambda b, i: (b, i, 0))
    return pl.pallas_call(
        _in_sa_kernel,
        grid=(B, S // tm),
        in_specs=_in_specs_common(tm, D) + [_full((D, N)), _full((1, SA_LATENT))],
        out_specs=[pl.BlockSpec((1, H, tm, HEAD_DIM), lambda b, i: (b, 0, i, 0)),
                   row(SA_LATENT), row(512), row(128), row(128)],
        out_shape=[jax.ShapeDtypeStruct((B, H, S, HEAD_DIM), _BF),
                   jax.ShapeDtypeStruct((B, S, SA_LATENT), _BF),
                   jax.ShapeDtypeStruct((B, S, 512), _BF),
                   jax.ShapeDtypeStruct((B, S, 128), _BF),
                   jax.ShapeDtypeStruct((B, S, 128), _F32)],
        compiler_params=_params(("parallel", "parallel")),
        name="in_sa",
    )(x, g, sc, sh, w, kv_g)


def _sg_kernel(x_ref, g_ref, sc_ref, sh_ref, w_ref, b_ref, lg_ref, lb_ref, ws_ref, bs_ref, wo_ref, bo_ref,
               g1_ref, o_ref, a_scr, v_scr, z_scr):
    x = x_ref[0]
    hn = _rms_mod(x, g_ref[...], sc_ref[0], sh_ref[0]).astype(_BF)
    tm = x.shape[0]
    W = a_scr.shape[1]
    G = ws_ref.shape[0]
    C = SG_CHUNK
    gw = W // G
    tn = 512
    n = W // tn
    lo = lambda c: slice(c * tn, (c + 1) * tn)
    hi = lambda c: slice(W + c * tn, W + (c + 1) * tn)
    ti = lax.broadcasted_iota(_I32, (C, C), 0)
    si = lax.broadcasted_iota(_I32, (C, C), 1)
    cols = [hi(c) for c in range(n)] + [lo(c) for c in range(n)]
    nxt = _dot(hn, w_ref[:, cols[0]])
    for t in range(2 * n):
        cur = nxt
        if t + 1 < 2 * n:
            nxt = _dot(hn, w_ref[:, cols[t + 1]])
        act = jax.nn.gelu(cur + b_ref[:, cols[t]], approximate=True)
        if t < n:
            a_scr[:, lo(t)] = act
            if t == n - 1:
                v_scr[...] = _layernorm(a_scr[...], lg_ref[...], lb_ref[...]).astype(_BF)
            continue
        for gl in range(tn // gw):
            g = (t - n) * (tn // gw) + gl
            wg = jnp.where(ti >= si, ws_ref[g], 0.0).astype(_BF)
            bcol = bs_ref[:, g:g + 1]
            gcols = slice(g * gw, (g + 1) * gw)
            for r in range(tm // C):
                rows = slice(r * C, (r + 1) * C)
                sv = _dot(wg, v_scr[rows, gcols]) + bcol
                z_scr[rows, gcols] = (act[rows, gl * gw:(gl + 1) * gw] * sv).astype(_BF)
    y = _dot(z_scr[...], wo_ref[...]) + bo_ref[...]
    o_ref[0] = x + g1_ref[0] * y


def _sg_mixer(x, g, sc, sh, w, b, ln_g, ln_b, w_s, b_s_t, w_out, b_out, g1):
    B, S, D = x.shape
    N = w.shape[1]
    W = N // 2
    G = w_s.shape[0]
    tm = SG_ROWS
    return pl.pallas_call(
        _sg_kernel,
        grid=(B, S // tm),
        in_specs=_in_specs_common(tm, D) + [_full((D, N)), _full((1, N)), _full((1, W)), _full((1, W)),
                                            _full((G, SG_CHUNK, SG_CHUNK)), _full((SG_CHUNK, G)),
                                            _full((W, D)), _full((1, D)),
                                            pl.BlockSpec((1, 1, D), lambda b, i: (b, 0, 0))],
        out_specs=pl.BlockSpec((1, tm, D), lambda b, i: (b, i, 0)),
        out_shape=jax.ShapeDtypeStruct((B, S, D), _F32),
        scratch_shapes=[pltpu.VMEM((tm, W), _F32), pltpu.VMEM((tm, W), _BF), pltpu.VMEM((tm, W), _BF)],
        compiler_params=_params(("parallel", "parallel")),
        name="sg_mixer",
    )(x, g, sc, sh, w, b, ln_g, ln_b, w_s, b_s_t, w_out, b_out, g1)


def _attend(qs, keys, values, W, fars, near_biases, block_rows, madd_ref=None):
    nw = min(W, NEAR_W)
    fw = W - nw
    lo = NEAR_W - nw
    n_rows = qs[0].shape[0]
    blocks = [slice(r, r + block_rows) for r in range(0, n_rows, block_rows)]
    logits = []
    for q, far, near_bias in zip(qs, fars, near_biases):
        for rows in blocks:
            we = W - (n_rows - rows.stop)
            s_n = _dot_nt(q[rows], keys(fw, we)) + near_bias(rows, lo, lo + we - fw)
            s_f = None
            if fw:
                s_f = _dot_nt(q[rows], keys(0, fw)) + far
            if madd_ref is not None:
                s_n = s_n + madd_ref[rows, fw:we]
                if fw:
                    s_f = s_f + madd_ref[rows, 0:fw]
            logits.append((s_n, s_f, we))
    outs = []
    for s_n, s_f, we in logits:
        m = jnp.max(s_n, axis=1, keepdims=True)
        if fw:
            m = jnp.maximum(m, jnp.max(s_f, axis=1, keepdims=True))
            p_f = jnp.exp(s_f - m)
        p_n = jnp.exp(s_n - m)
        l = jnp.sum(p_n, axis=1, keepdims=True)
        o = _dot(p_n.astype(_BF), values(fw, we))
        if fw:
            l = l + jnp.sum(p_f, axis=1, keepdims=True)
            o = o + _dot(p_f.astype(_BF), values(0, fw))
        outs.append(o * (1.0 / l))
    nb = len(blocks)
    return [jnp.concatenate(outs[i * nb:(i + 1) * nb], axis=0) for i in range(len(qs))]


def _da_kernel(tbl_ref, q_ref, k_ref, v_ref, bn_ref, lam_ref, g_ref, wo_ref, x_ref, g1_ref, o_ref,
               oh_ref, oc_ref, *, lambda_init):
    a = pl.program_id(1)
    H = q_ref.shape[2]
    T = q_ref.shape[3]
    S = k_ref.shape[3]
    lane = lax.broadcasted_iota(_I32, (T, HEAD_DIM), 1)
    lam = lam_ref[...]
    lam_full = (jnp.exp(jnp.sum(lam[0:1] * lam[1:2], axis=1, keepdims=True))
                - jnp.exp(jnp.sum(lam[2:3] * lam[3:4], axis=1, keepdims=True)) + lambda_init)

    def head(h, W):
        q = q_ref[0, 0, h] * (DA_DIM ** -0.5)
        zero = jnp.zeros_like(q)
        q1 = jnp.where(lane < DA_DIM, q, zero)
        q2 = jnp.where(lane >= DA_DIM, q, zero)
        far = tbl_ref[N_BUCKETS // 2 - 1, h]
        near_bias = lambda rows, lo, hi: bn_ref[h, rows, lo:hi]
        o1, o2 = _attend([q1, q2], lambda lo, hi: k_ref[0, 0, h, lo:hi, :], lambda lo, hi: v_ref[0, 0, h, lo:hi, :],
                         W, [far, far], [near_bias, near_bias], DA_ROWS)
        o = o1 - lam_full * o2
        ms = jnp.mean(o * o, axis=-1, keepdims=True)
        oh_ref[h] = (o * lax.rsqrt(ms + SUBLN_EPS) * g_ref[...]) * (1.0 - lambda_init)

    for br in range(S // T):
        @pl.when(a == br)
        def _(br=br):
            def body(h, carry):
                head(h, (br + 1) * T)
                return carry
            lax.fori_loop(0, H, body, 0)

    for h in range(H):
        oc_ref[:, h * HEAD_DIM:(h + 1) * HEAD_DIM] = oh_ref[h].astype(_BF)
    o_ref[0] = x_ref[0] + g1_ref[0] * _dot(oc_ref[...], wo_ref[...])


def _diff_attn(qkv, rel_table, bn, lam, subln_g, lambda_init, w_out, x, g1):
    B, _, H, S, _ = qkv.shape
    T = ATT_TILE
    D = x.shape[2]
    kern = functools.partial(_da_kernel, lambda_init=lambda_init)
    part = lambda j, rows, idx: pl.BlockSpec((1, 1, H, rows, HEAD_DIM), lambda b, i: (b, j, 0, idx(i), 0))
    return pl.pallas_call(
        kern,
        grid=(B, S // T),
        in_specs=[pl.BlockSpec(memory_space=pltpu.SMEM),
                  part(0, T, lambda i: i), part(1, S, lambda i: 0), part(2, S, lambda i: 0),
                  _full((H, T, NEAR_W)),
                  _full((4, DA_DIM)), _full((1, HEAD_DIM)),
                  _full((H * HEAD_DIM, D)),
                  pl.BlockSpec((1, T, D), lambda b, i: (b, i, 0)),
                  pl.BlockSpec((1, 1, D), lambda b, i: (b, 0, 0))],
        out_specs=pl.BlockSpec((1, T, D), lambda b, i: (b, i, 0)),
        out_shape=jax.ShapeDtypeStruct((B, S, D), _F32),
        scratch_shapes=[pltpu.VMEM((H, T, HEAD_DIM), _F32), pltpu.VMEM((T, H * HEAD_DIM), _BF)],
        compiler_params=_params(("parallel", "arbitrary")),
        name="diff_attn",
    )(rel_table, qkv, qkv, qkv, bn, lam, subln_g, w_out, x, g1)


_INT_MIN = np.int32(-2 ** 31)
_KEY_NEG_INF = np.int32(np.array(-np.inf, np.float32).view(np.int32) ^ np.int32(0x7FFFFFFF))
SEARCH_GROUPS = 4
HEADS_PER_STEP = 2


def _sa_branch(W, tbl_ref, q_ref, ckv_ref, iw_ref, ikk_ref, wuk_ref, wuv_ref, bn_ref,
               iqm_ref, keys_ref, madd_ref, oh_ref, last_ref):
    T = iw_ref.shape[1]
    H = q_ref.shape[1]
    iw = iw_ref[0]

    for cb in range(W // T):
        cols = slice(cb * T, (cb + 1) * T)
        ikt = ikk_ref[0, cols, :]
        score = jnp.zeros((T, T), _F32)
        for hh in range(H):
            score = score + jnp.maximum(_dot_nt(iqm_ref[hh], ikt), 0.0) * iw[:, hh:hh + 1]
        score = jnp.where(score == 0.0, 0.0, score)
        if cb == W // T - 1:
            row = lax.broadcasted_iota(_I32, (T, T), 0)
            col = lax.broadcasted_iota(_I32, (T, T), 1)
            score = jnp.where((col >> CHUNK_SHIFT) <= (row >> CHUNK_SHIFT), score, -jnp.inf)
        bits = lax.bitcast_convert_type(score, _I32)
        keys_ref[:, cols] = bits ^ ((bits >> 31) & np.int32(0x7FFFFFFF))

    def count(pred, rows=slice(None)):
        return jnp.sum(jnp.where(pred(keys_ref[rows, 0:W]), 1.0, 0.0), axis=1, keepdims=True)

    R = T // SEARCH_GROUPS

    def bit_body(it, thrs):
        bit = lax.shift_left(np.int32(1), np.int32(31) - it)
        out = []
        for g in range(SEARCH_GROUPS):
            cand_u = thrs[g] | bit
            cand = cand_u ^ _INT_MIN
            cnt = count(lambda k: k >= cand, slice(g * R, (g + 1) * R))
            out.append(jnp.where(cnt >= float(TOPK), cand_u, thrs[g]))
        return tuple(out)

    thrs = lax.fori_loop(0, 32, bit_body, tuple(jnp.zeros((R, 1), _I32) for _ in range(SEARCH_GROUPS)))
    thr = jnp.concatenate(thrs, axis=0) ^ _INT_MIN

    cnt_gt = count(lambda k: k > thr)
    cnt_ge = count(lambda k: k >= thr)
    need = float(TOPK) - cnt_gt
    last_ref[...] = jnp.full((T, 1), 2 ** 30, _I32)
    tied = jnp.logical_and(cnt_ge - cnt_gt > need, thr > _KEY_NEG_INF)

    @pl.when(jnp.max(jnp.where(tied, 1.0, 0.0)) > 0.0)
    def _():
        col = lax.broadcasted_iota(_I32, (T, W), 1)

        def idx_body(it, v):
            cand = v | lax.shift_left(np.int32(1), np.int32(10) - it)
            below = count(lambda k: jnp.logical_and(k == thr, col < cand))
            return jnp.where(below < need, cand, v)
        last_ref[...] = lax.fori_loop(0, 11, idx_body, jnp.zeros((T, 1), _I32))

    last = last_ref[...]
    k = keys_ref[:, 0:W]
    col = lax.broadcasted_iota(_I32, (T, W), 1)
    sel = jnp.logical_or(k > thr, jnp.logical_and(k == thr, col <= last))
    madd_ref[:, 0:W] = jnp.where(sel, 0.0, NEG)

    def head_body(hp, carry):
        hs = [hp * HEADS_PER_STEP + u for u in range(HEADS_PER_STEP)]
        qlats = [(_dot(q_ref[0, h], wuk_ref[h]) * (HEAD_DIM ** -0.5)).astype(_BF) for h in hs]
        latents = lambda lo, hi: ckv_ref[0, lo:hi, :]
        olats = _attend(qlats, latents, latents, W,
                        [tbl_ref[N_BUCKETS // 2 - 1, h] for h in hs],
                        [functools.partial(lambda h, rows, lo, hi: bn_ref[h, rows, lo:hi], h) for h in hs],
                        SA_ROWS, madd_ref)
        for h, olat in zip(hs, olats):
            oh_ref[h] = _dot(olat.astype(_BF), wuv_ref[h])
        return carry

    lax.fori_loop(0, H // HEADS_PER_STEP, head_body, 0)


def _sa_kernel(tbl_ref, q_ref, ckv_ref, iq_ref, ikk_ref, iw_ref, wuk_ref, wuv_ref, bn_ref, wo_ref, x_ref, g1_ref,
               o_ref, iqm_ref, keys_ref, madd_ref, oh_ref, last_ref, oc_ref):
    a = pl.program_id(1)
    T = iq_ref.shape[1]
    S = ckv_ref.shape[1]
    H = q_ref.shape[1]
    lane = lax.broadcasted_iota(_I32, (T, 128), 1)
    for p in range(H // 2):
        iqp = iq_ref[0, :, p * 128:(p + 1) * 128]
        zero = jnp.zeros_like(iqp)
        iqm_ref[2 * p] = jnp.where(lane < IDX_DIM, iqp, zero)
        iqm_ref[2 * p + 1] = jnp.where(lane >= IDX_DIM, iqp, zero)

    for br in range(S // T):
        @pl.when(a == br)
        def _(br=br):
            _sa_branch((br + 1) * T, tbl_ref, q_ref, ckv_ref, iw_ref, ikk_ref, wuk_ref, wuv_ref, bn_ref,
                       iqm_ref, keys_ref, madd_ref, oh_ref, last_ref)

    for h in range(H):
        oc_ref[:, h * HEAD_DIM:(h + 1) * HEAD_DIM] = oh_ref[h].astype(_BF)
    o_ref[0] = x_ref[0] + g1_ref[0] * _dot(oc_ref[...], wo_ref[...])


def _sparse_attn(q, ckv, iq, ikk, iw, wuk, wuv, rel_table, bn, w_out, x, g1):
    B, H, S, _ = q.shape
    T = ATT_TILE
    D = x.shape[2]
    return pl.pallas_call(
        _sa_kernel,
        grid=(B, S // T),
        in_specs=[pl.BlockSpec(memory_space=pltpu.SMEM),
                  pl.BlockSpec((1, H, T, HEAD_DIM), lambda b, i: (b, 0, i, 0)),
                  pl.BlockSpec((1, S, SA_LATENT), lambda b, i: (b, 0, 0)),
                  pl.BlockSpec((1, T, 512), lambda b, i: (b, i, 0)),
                  pl.BlockSpec((1, S, 128), lambda b, i: (b, 0, 0)),
                  pl.BlockSpec((1, T, 128), lambda b, i: (b, i, 0)),
                  _full((H, HEAD_DIM, SA_LATENT)),
                  _full((H, SA_LATENT, HEAD_DIM)),
                  _full((H, T, NEAR_W)),
                  _full((H * HEAD_DIM, D)),
                  pl.BlockSpec((1, T, D), lambda b, i: (b, i, 0)),
                  pl.BlockSpec((1, 1, D), lambda b, i: (b, 0, 0))],
        out_specs=pl.BlockSpec((1, T, D), lambda b, i: (b, i, 0)),
        out_shape=jax.ShapeDtypeStruct((B, S, D), _F32),
        scratch_shapes=[pltpu.VMEM((H, T, 128), _BF),
                        pltpu.VMEM((T, S), _I32),
                        pltpu.VMEM((T, S), _F32),
                        pltpu.VMEM((H, T, HEAD_DIM), _F32),
                        pltpu.VMEM((T, 1), _I32),
                        pltpu.VMEM((T, H * HEAD_DIM), _BF)],
        compiler_params=_params(("parallel", "arbitrary")),
        name="sparse_attn",
    )(rel_table, q, ckv, iq, ikk, iw, wuk, wuv, bn, w_out, x, g1)


def _cv_kernel(x_ref, g_ref, sc_ref, sh_ref, w1_ref, b1_ref, wdw_ref, bdw_ref, lg_ref, lb_ref, w2_ref, b2_ref,
               g1_ref, o_ref, ext_ref, y_ref):
    i = pl.program_id(1)
    tm = x_ref.shape[1]
    D = x_ref.shape[2]
    n_ext = CONV_HALO + tm

    @pl.when(i == 0)
    def _():
        ext_ref[0, 0:CONV_HALO] = jnp.zeros((CONV_HALO, D), _F32)

    @pl.when(i > 0)
    def _():
        ext_ref[0, 0:CONV_HALO] = ext_ref[0, tm:n_ext]

    hn = _rms_mod(x_ref[0], g_ref[...], sc_ref[0], sh_ref[0]).astype(_BF)
    tn = 512
    n = D // tn
    lo = lambda c: slice(c * tn, (c + 1) * tn)
    hi = lambda c: slice(D + c * tn, D + (c + 1) * tn)
    dots = lambda c: (_dot(hn, w1_ref[:, lo(c)]), _dot(hn, w1_ref[:, hi(c)]))
    nxt = dots(0)
    for c in range(n):
        cur = nxt
        if c + 1 < n:
            nxt = dots(c + 1)
        ext_ref[0, CONV_HALO:n_ext, lo(c)] = ((cur[0] + b1_ref[:, lo(c)])
                                              * jax.nn.sigmoid(cur[1] + b1_ref[:, hi(c)]))
    for r in range(1, SUBLANES):
        ext_ref[r, 0:n_ext - SUBLANES] = ext_ref[0, r:r + n_ext - SUBLANES]
    rb = CONV_ROWS
    base = CONV_HALO - (CONV_WIDTH - 1)

    for row0 in range(0, tm, rb):
        ns = rb // SUBLANES
        accs = [jnp.broadcast_to(bdw_ref[...], (SUBLANES, bdw_ref.shape[1]))] * ns
        for k in range(CONV_WIDTH):
            r = (base + k) % SUBLANES
            al = base + k - r
            w = wdw_ref[k]
            xk = ext_ref[r, row0 + al:row0 + al + rb, :]
            accs = [a + w * xk[s * SUBLANES:(s + 1) * SUBLANES] for s, a in enumerate(accs)]
        acc = jnp.concatenate(accs, axis=0)
        y_ref[row0:row0 + rb] = _silu(_layernorm(acc, lg_ref[...], lb_ref[...]))
    y = _dot(y_ref[...].astype(_BF), w2_ref[...]) + b2_ref[...]
    o_ref[0] = x_ref[0] + g1_ref[0] * y


def _cv_mixer(x, g, sc, sh, w1, b1, w_dw, b_dw, ln_g, ln_b, w2, b2, g1):
    B, S, D = x.shape
    tm = 512
    vec = lambda: _full((1, D))
    return pl.pallas_call(
        _cv_kernel,
        grid=(B, S // tm),
        in_specs=_in_specs_common(tm, D) + [_full((D, 2 * D)), _full((1, 2 * D)),
                                            _full((CONV_WIDTH, SUBLANES, D)), vec(), vec(), vec(),
                                            _full((D, D)), vec(),
                                            pl.BlockSpec((1, 1, D), lambda b, i: (b, 0, 0))],
        out_specs=pl.BlockSpec((1, tm, D), lambda b, i: (b, i, 0)),
        out_shape=jax.ShapeDtypeStruct((B, S, D), _F32),
        scratch_shapes=[pltpu.VMEM((SUBLANES, CONV_HALO + tm, D), _F32), pltpu.VMEM((tm, D), _F32)],
        compiler_params=_params(("parallel", "arbitrary")),
        name="cv_mixer",
    )(x, g, sc, sh, w1, b1, w_dw, b_dw, ln_g, ln_b, w2, b2, g1)


def _ffn_kernel(x_ref, xh_ref, g_ref, sc_ref, sh_ref, g2_ref, wup_ref, wdw_ref, bdw_ref, wdn_ref,
                fg_ref, o_ref, he_ref, act_ref, acc_ref, *, final):
    i = pl.program_id(1)
    tm = x_ref.shape[1]
    F = wdn_ref.shape[0]
    x = x_ref[0]
    he_ref[FFN_HALO:FFN_HALO + tm] = _rms_mod(x, g_ref[...], sc_ref[0], sh_ref[0]).astype(_BF)
    hh = _rms_mod(xh_ref[0], g_ref[...], sc_ref[0], sh_ref[0])
    he_ref[0:FFN_HALO] = jnp.where(i > 0, hh, jnp.zeros_like(hh)).astype(_BF)
    tf = FFN_CHUNK
    n_chunks = F // tf

    def up(c):
        he = he_ref[...]
        return (_dot(he, wup_ref[:, c * tf:(c + 1) * tf]), _dot(he, wup_ref[:, F + c * tf:F + (c + 1) * tf]))

    def conv(a, col0):
        cols = slice(col0, col0 + tf)
        w = wdw_ref[:, cols]
        return (a[FFN_HALO - 2:FFN_HALO - 2 + tm] * w[0:1]
                + a[FFN_HALO - 1:FFN_HALO - 1 + tm] * w[1:2]
                + a[FFN_HALO:FFN_HALO + tm] * w[2:3] + bdw_ref[:, cols])

    nxt = up(0)
    for c in range(n_chunks):
        cur = nxt
        if c + 1 < n_chunks:
            nxt = up(c + 1)
        act = _silu(conv(cur[0], c * tf)) * conv(cur[1], F + c * tf)
        act_ref[:, c * tf:(c + 1) * tf] = act.astype(_BF)
        if (c + 1) % FFN_DOWN_GROUP == 0 or c + 1 == n_chunks:
            lo = (c // FFN_DOWN_GROUP) * FFN_DOWN_GROUP * tf
            part = _dot(act_ref[:, lo:(c + 1) * tf], wdn_ref[lo:(c + 1) * tf, :])
            if lo == 0:
                acc_ref[...] = part
            else:
                acc_ref[...] += part
    xn = x + g2_ref[0] * acc_ref[...]
    if final:
        ms = jnp.mean(xn * xn, axis=-1, keepdims=True)
        xn = xn * lax.rsqrt(ms + EPS) * fg_ref[...]
    o_ref[0] = xn


def _ffn(x, g, sc, sh, g2, w_up, w_dw, b_dw, w_down, final_g, final):
    B, S, D = x.shape
    F = w_down.shape[0]
    tm = FFN_ROWS
    hb = tm // FFN_HALO
    kern = functools.partial(_ffn_kernel, final=final)
    mod = lambda: pl.BlockSpec((1, 1, D), lambda b, i: (b, 0, 0))
    return pl.pallas_call(
        kern,
        grid=(B, S // tm),
        in_specs=[pl.BlockSpec((1, tm, D), lambda b, i: (b, i, 0)),
                  pl.BlockSpec((1, FFN_HALO, D), lambda b, i: (b, jnp.maximum(i * hb - 1, 0), 0)),
                  _full((1, D)), mod(), mod(), mod(),
                  _full((D, 2 * F)), _full((3, 2 * F)), _full((1, 2 * F)), _full((F, D)),
                  _full((1, D))],
        out_specs=pl.BlockSpec((1, tm, D), lambda b, i: (b, i, 0)),
        out_shape=jax.ShapeDtypeStruct((B, S, D), _F32),
        scratch_shapes=[pltpu.VMEM((FFN_HALO + tm, D), _BF), pltpu.VMEM((tm, F), _BF),
                        pltpu.VMEM((tm, D), _F32)],
        compiler_params=_params(("parallel", "parallel")),
        name="ffn",
    )(x, x, g, sc, sh, g2, w_up, w_dw, b_dw, w_down, final_g)


def kernel(x, c, rel_table, ada_w, ada_b, norm_g, final_g, da_w_in, da_lam, da_subln_g, da_w_out, cv_w_pw1, cv_b_pw1, cv_w_dw, cv_b_dw, cv_ln_g, cv_ln_b, cv_w_pw2, cv_b_pw2, sa_w_in, sa_kv_g, sa_w_uk, sa_w_uv, sa_w_out, sg_w_in, sg_b_in, sg_ln_g, sg_ln_b, sg_w_s, sg_b_s, sg_w_out, sg_b_out, ff_w_up, ff_w_dw, ff_b_dw, ff_w_down):
    B, S, D = x.shape
    depth = ada_w.shape[0]
    n_mixers = 4
    H = N_HEADS
    mods = _ada(c, ada_w, ada_b)
    bn = _bias_tiles(rel_table)
    row = lambda v: v.reshape(1, -1)

    for layer in range(depth):
        kind = layer % n_mixers
        j = layer // n_mixers
        sh1, sc1, g1, sh2, sc2, g2 = [m.reshape(B, 1, D) for m in jnp.split(mods[layer], 6, axis=-1)]
        ng1 = row(norm_g[layer, 0])
        if kind == 0:
            lambda_init = 0.8 - 0.6 * math.exp(-0.3 * layer)
            qkv = _in_da(x, ng1, sc1, sh1, da_w_in[j].astype(_BF))
            x = _diff_attn(qkv, rel_table, bn, da_lam[j], row(da_subln_g[j]), lambda_init,
                           da_w_out[j].astype(_BF), x, g1)
        elif kind == 1:
            w_dw = jnp.broadcast_to(cv_w_dw[j][:, None, :], (CONV_WIDTH, SUBLANES, D))
            x = _cv_mixer(x, ng1, sc1, sh1, cv_w_pw1[j].astype(_BF), row(cv_b_pw1[j]), w_dw, row(cv_b_dw[j]),
                          row(cv_ln_g[j]), row(cv_ln_b[j]), cv_w_pw2[j].astype(_BF), row(cv_b_pw2[j]), g1)
        elif kind == 2:
            w = sa_w_in[j]
            o1 = H * HEAD_DIM
            o2 = o1 + SA_LATENT
            o3 = o2 + H * IDX_DIM
            o4 = o3 + IDX_DIM
            w_ik = w[:, o3:o4]
            w_iw = jnp.pad(w[:, o4:], ((0, 0), (0, 128 - (w.shape[1] - o4))))
            w_cat = jnp.concatenate([w[:, :o3], w_ik, w_ik, w_iw], axis=1).astype(_BF)
            q, ckv, iq, ikk, iw = _in_sa(x, ng1, sc1, sh1, w_cat, row(sa_kv_g[j]))
            wuk = jnp.transpose(sa_w_uk[j], (1, 2, 0)).astype(_BF)
            wuv = jnp.transpose(sa_w_uv[j], (1, 0, 2)).astype(_BF)
            x = _sparse_attn(q, ckv, iq, ikk, iw, wuk, wuv, rel_table, bn, sa_w_out[j].astype(_BF), x, g1)
        else:
            x = _sg_mixer(x, ng1, sc1, sh1, sg_w_in[j].astype(_BF), row(sg_b_in[j]), row(sg_ln_g[j]),
                          row(sg_ln_b[j]), sg_w_s[j], sg_b_s[j].T, sg_w_out[j].astype(_BF),
                          row(sg_b_out[j]), g1)
        x = _ffn(x, row(norm_g[layer, 1]), sc2, sh2, g2, ff_w_up[layer].astype(_BF), ff_w_dw[layer],
                 row(ff_b_dw[layer]), ff_w_down[layer].astype(_BF), row(final_g), layer == depth - 1)
    return x
```

```python
import functools
import math

import jax
import jax.numpy as jnp
import numpy as np
from jax import lax
from jax.experimental import pallas as pl
from jax.experimental.pallas import tpu as pltpu

_BF = jnp.bfloat16
_F32 = jnp.float32
_I32 = jnp.int32

EPS = 1e-6
SUBLN_EPS = 1e-5
NEG = -1e30
CHUNK = 64
CHUNK_SHIFT = 6
N_HEADS = 8
HEAD_DIM = 128
DA_DIM = 64
SA_LATENT = 256
IDX_DIM = 64
TOPK = 256
N_BUCKETS = 32
MAX_DISTANCE = 128
SG_CHUNK = 128
SG_GROUPS = 8
CONV_WIDTH = 31
CONV_HALO = 32
CONV_ROWS = 8
SUBLANES = 8
FFN_HALO = 8
PROJ_ROWS = 1024
SG_ROWS = 1024
FFN_ROWS = 1024
FFN_CHUNK = 256
FFN_DOWN_GROUP = 11
ATT_TILE = 512
DA_HEADS_PER_STEP = 1
DA_ROWS = 128
SA_ROWS = 256
NEAR_W = 768
VMEM_LIMIT = 56 * 1024 * 1024


def _dot(a, b):
    return jnp.dot(a, b, preferred_element_type=_F32)


def _dot_nt(a, b):
    return lax.dot_general(a, b, (((1,), (1,)), ((), ())), preferred_element_type=_F32)


def _rms_mod(x, g, sc, sh):
    ms = jnp.mean(x * x, axis=-1, keepdims=True)
    return (x * lax.rsqrt(ms + EPS) * g) * (1.0 + sc) + sh


def _layernorm(x, g, b):
    mu = jnp.mean(x, axis=-1, keepdims=True)
    xc = x - mu
    var = jnp.mean(xc * xc, axis=-1, keepdims=True)
    return xc * lax.rsqrt(var + EPS) * g + b


def _silu(x):
    return x * jax.nn.sigmoid(x)


def _params(sem):
    return pltpu.CompilerParams(dimension_semantics=sem, vmem_limit_bytes=VMEM_LIMIT)


def _full(shape):
    n = len(shape)
    return pl.BlockSpec(shape, lambda *_: (0,) * n, pipeline_mode=pl.Buffered(1))


def _ada_kernel(c_ref, w_ref, b_ref, o_ref):
    ca = _silu(c_ref[...]).astype(_BF)
    o_ref[0] = _dot(ca, w_ref[0].astype(_BF)) + b_ref[0]


def _ada(c, ada_w, ada_b):
    L, D, N = ada_w.shape
    B = c.shape[0]
    tn = N // 4
    return pl.pallas_call(
        _ada_kernel,
        grid=(L, N // tn),
        in_specs=[pl.BlockSpec((B, D), lambda l, j: (0, 0)),
                  pl.BlockSpec((1, D, tn), lambda l, j: (l, 0, j)),
                  pl.BlockSpec((1, 1, tn), lambda l, j: (l, 0, j))],
        out_specs=pl.BlockSpec((1, B, tn), lambda l, j: (l, 0, j)),
        out_shape=jax.ShapeDtypeStruct((L, B, N), _F32),
        compiler_params=_params(("arbitrary", "arbitrary")),
        name="ada",
    )(c, ada_w, ada_b.reshape(L, 1, N))


def _bias_kernel(tbl_ref, bn_ref):
    h = pl.program_id(0)
    shape = bn_ref.shape[1:]
    i = lax.broadcasted_iota(_I32, shape, 0)
    j = lax.broadcasted_iota(_I32, shape, 1) - (NEAR_W - ATT_TILE)
    nb = N_BUCKETS // 2
    max_exact = nb // 2
    rel = j - i
    ret = jnp.where(rel > 0, nb, 0)
    n = jnp.abs(rel)
    nf = jnp.maximum(n, 1).astype(_F32)
    large = max_exact + (jnp.log(nf / max_exact) / math.log(MAX_DISTANCE / max_exact)
                         * (nb - max_exact)).astype(_I32)
    large = jnp.minimum(large, nb - 1)
    bucket = ret + jnp.where(n < max_exact, n, large)
    out = jnp.zeros(shape, _F32)
    for bk in range(N_BUCKETS):
        out = jnp.where(bucket == bk, tbl_ref[bk, h], out)
    visible = (j >> CHUNK_SHIFT) <= (i >> CHUNK_SHIFT)
    bn_ref[0] = jnp.where(visible, out, NEG)


def _bias_tiles(rel_table):
    H = rel_table.shape[1]
    return pl.pallas_call(
        _bias_kernel,
        grid=(H,),
        in_specs=[pl.BlockSpec(memory_space=pltpu.SMEM)],
        out_specs=pl.BlockSpec((1, ATT_TILE, NEAR_W), lambda h: (h, 0, 0)),
        out_shape=jax.ShapeDtypeStruct((H, ATT_TILE, NEAR_W), _F32),
        compiler_params=_params(("arbitrary",)),
        name="bias_tiles",
    )(rel_table)


def _in_specs_common(tm, D):
    return [pl.BlockSpec((1, tm, D), lambda b, i: (b, i, 0)),
            pl.BlockSpec((1, D), lambda b, i: (0, 0)),
            pl.BlockSpec((1, 1, D), lambda b, i: (b, 0, 0)),
            pl.BlockSpec((1, 1, D), lambda b, i: (b, 0, 0))]


def _in_da_kernel(x_ref, g_ref, sc_ref, sh_ref, w_ref, o_ref):
    hn = _rms_mod(x_ref[0], g_ref[...], sc_ref[0], sh_ref[0]).astype(_BF)
    N = w_ref.shape[1]
    H = o_ref.shape[2]
    tn = 512
    per = tn // HEAD_DIM
    for c in range(N // tn):
        r = _dot(hn, w_ref[:, c * tn:(c + 1) * tn])
        for k in range(per):
            idx = c * per + k
            o_ref[0, idx // H, idx % H] = r[:, k * HEAD_DIM:(k + 1) * HEAD_DIM].astype(_BF)


def _in_da(x, g, sc, sh, w):
    B, S, D = x.shape
    N = w.shape[1]
    H = N_HEADS
    tm = PROJ_ROWS
    return pl.pallas_call(
        _in_da_kernel,
        grid=(B, S // tm),
        in_specs=_in_specs_common(tm, D) + [_full((D, N))],
        out_specs=pl.BlockSpec((1, 3, H, tm, HEAD_DIM), lambda b, i: (b, 0, 0, i, 0)),
        out_shape=jax.ShapeDtypeStruct((B, 3, H, S, HEAD_DIM), _BF),
        compiler_params=_params(("parallel", "parallel")),
        name="in_da",
    )(x, g, sc, sh, w)


def _in_sa_kernel(x_ref, g_ref, sc_ref, sh_ref, w_ref, kvg_ref,
                  q_ref, ckv_ref, iq_ref, ikk_ref, iw_ref):
    hn = _rms_mod(x_ref[0], g_ref[...], sc_ref[0], sh_ref[0]).astype(_BF)
    H = q_ref.shape[1]
    for c in range(H // 2):
        r = _dot(hn, w_ref[:, c * 256:(c + 1) * 256])
        q_ref[0, 2 * c] = r[:, :HEAD_DIM].astype(_BF)
        q_ref[0, 2 * c + 1] = r[:, HEAD_DIM:].astype(_BF)
    o = H * HEAD_DIM
    ckv = _dot(hn, w_ref[:, o:o + SA_LATENT])
    ms = jnp.mean(ckv * ckv, axis=-1, keepdims=True)
    ckv_ref[0] = (ckv * lax.rsqrt(ms + EPS) * kvg_ref[...]).astype(_BF)
    o += SA_LATENT
    iq_ref[0] = (_dot(hn, w_ref[:, o:o + 512]) * (IDX_DIM ** -0.5)).astype(_BF)
    o += 512
    r = _dot(hn, w_ref[:, o:o + 256])
    ikk_ref[0] = r[:, :128].astype(_BF)
    iw_ref[0] = r[:, 128:] * (N_HEADS ** -0.5)


def _in_sa(x, g, sc, sh, w, kv_g):
    B, S, D = x.shape
    N = w.shape[1]
    H = N_HEADS
    tm = PROJ_ROWS
    row = lambda n: pl.BlockSpec((1, tm, n), lambda b, i: (b, i, 0))
    return pl.pallas_call(
        _in_sa_kernel,
        grid=(B, S // tm),
        in_specs=_in_specs_common(tm, D) + [_full((D, N)), _full((1, SA_LATENT))],
        out_specs=[pl.BlockSpec((1, H, tm, HEAD_DIM), lambda b, i: (b, 0, i, 0)),
                   row(SA_LATENT), row(512), row(128), row(128)],
        out_shape=[jax.ShapeDtypeStruct((B, H, S, HEAD_DIM), _BF),
                   jax.ShapeDtypeStruct((B, S, SA_LATENT), _BF),
                   jax.ShapeDtypeStruct((B, S, 512), _BF),
                   jax.ShapeDtypeStruct((B, S, 128), _BF),
                   jax.ShapeDtypeStruct((B, S, 128), _F32)],
        compiler_params=_params(("parallel", "parallel")),
        name="in_sa",
    )(x, g, sc, sh, w, kv_g)


def _sg_kernel(x_ref, g_ref, sc_ref, sh_ref, w_ref, b_ref, lg_ref, lb_ref, ws_ref, bs_ref, wo_ref, bo_ref,
               g1_ref, o_ref, a_scr, v_scr, z_scr):
    x = x_ref[0]
    hn = _rms_mod(x, g_ref[...], sc_ref[0], sh_ref[0]).astype(_BF)
    tm = x.shape[0]
    W = a_scr.shape[1]
    G = ws_ref.shape[0]
    C = SG_CHUNK
    gw = W // G
    tn = 512
    n = W // tn
    lo = lambda c: slice(c * tn, (c + 1) * tn)
    hi = lambda c: slice(W + c * tn, W + (c + 1) * tn)
    ti = lax.broadcasted_iota(_I32, (C, C), 0)
    si = lax.broadcasted_iota(_I32, (C, C), 1)
    cols = [hi(c) for c in range(n)] + [lo(c) for c in range(n)]
    nxt = _dot(hn, w_ref[:, cols[0]])
    for t in range(2 * n):
        cur = nxt
        if t + 1 < 2 * n:
            nxt = _dot(hn, w_ref[:, cols[t + 1]])
        act = jax.nn.gelu(cur + b_ref[:, cols[t]], approximate=True)
        if t < n:
            a_scr[:, lo(t)] = act
            if t == n - 1:
                v_scr[...] = _layernorm(a_scr[...], lg_ref[...], lb_ref[...]).astype(_BF)
            continue
        for gl in range(tn // gw):
            g = (t - n) * (tn // gw) + gl
            wg = jnp.where(ti >= si, ws_ref[g], 0.0).astype(_BF)
            bcol = bs_ref[:, g:g + 1]
            gcols = slice(g * gw, (g + 1) * gw)
            for r in range(tm // C):
                rows = slice(r * C, (r + 1) * C)
                sv = _dot(wg, v_scr[rows, gcols]) + bcol
                z_scr[rows, gcols] = (act[rows, gl * gw:(gl + 1) * gw] * sv).astype(_BF)
    y = _dot(z_scr[...], wo_ref[...]) + bo_ref[...]
    o_ref[0] = x + g1_ref[0] * y


def _sg_mixer(x, g, sc, sh, w, b, ln_g, ln_b, w_s, b_s_t, w_out, b_out, g1):
    B, S, D = x.shape
    N = w.shape[1]
    W = N // 2
    G = w_s.shape[0]
    tm = SG_ROWS
    return pl.pallas_call(
        _sg_kernel,
        grid=(B, S // tm),
        in_specs=_in_specs_common(tm, D) + [_full((D, N)), _full((1, N)), _full((1, W)), _full((1, W)),
                                            _full((G, SG_CHUNK, SG_CHUNK)), _full((SG_CHUNK, G)),
                                            _full((W, D)), _full((1, D)),
                                            pl.BlockSpec((1, 1, D), lambda b, i: (b, 0, 0))],
        out_specs=pl.BlockSpec((1, tm, D), lambda b, i: (b, i, 0)),
        out_shape=jax.ShapeDtypeStruct((B, S, D), _F32),
        scratch_shapes=[pltpu.VMEM((tm, W), _F32), pltpu.VMEM((tm, W), _BF), pltpu.VMEM((tm, W), _BF)],
        compiler_params=_params(("parallel", "parallel")),
        name="sg_mixer",
    )(x, g, sc, sh, w, b, ln_g, ln_b, w_s, b_s_t, w_out, b_out, g1)


def _attend(qs, keys, values, W, fars, near_biases, block_rows, madd_ref=None):
    nw = min(W, NEAR_W)
    fw = W - nw
    lo = NEAR_W - nw
    n_rows = qs[0].shape[0]
    blocks = [slice(r, r + block_rows) for r in range(0, n_rows, block_rows)]
    per_q = lambda f: f if isinstance(f, (list, tuple)) else [f] * len(qs)
    logits = []
    for q, far, near_bias, keys, values in zip(qs, fars, near_biases, per_q(keys), per_q(values)):
        for rows in blocks:
            we = W - (n_rows - rows.stop)
            s_n = _dot_nt(q[rows], keys(fw, we)) + near_bias(rows, lo, lo + we - fw)
            s_f = None
            if fw:
                s_f = _dot_nt(q[rows], keys(0, fw)) + far
            if madd_ref is not None:
                s_n = s_n + madd_ref[rows, fw:we]
                if fw:
                    s_f = s_f + madd_ref[rows, 0:fw]
            logits.append((s_n, s_f, we, values))
    outs = []
    for s_n, s_f, we, values in logits:
        m = jnp.max(s_n, axis=1, keepdims=True)
        if fw:
            m = jnp.maximum(m, jnp.max(s_f, axis=1, keepdims=True))
            p_f = jnp.exp(s_f - m)
        p_n = jnp.exp(s_n - m)
        l = jnp.sum(p_n, axis=1, keepdims=True)
        o = _dot(p_n.astype(_BF), values(fw, we))
        if fw:
            l = l + jnp.sum(p_f, axis=1, keepdims=True)
            o = o + _dot(p_f.astype(_BF), values(0, fw))
        outs.append(o * (1.0 / l))
    nb = len(blocks)
    return [jnp.concatenate(outs[i * nb:(i + 1) * nb], axis=0) for i in range(len(qs))]


def _da_kernel(tbl_ref, q_ref, k_ref, v_ref, bn_ref, lam_ref, g_ref, wo_ref, x_ref, g1_ref, o_ref,
               oh_ref, oc_ref, *, lambda_init):
    a = pl.program_id(1)
    H = q_ref.shape[2]
    T = q_ref.shape[3]
    S = k_ref.shape[3]
    lane = lax.broadcasted_iota(_I32, (T, HEAD_DIM), 1)
    lam = lam_ref[...]
    lam_full = (jnp.exp(jnp.sum(lam[0:1] * lam[1:2], axis=1, keepdims=True))
                - jnp.exp(jnp.sum(lam[2:3] * lam[3:4], axis=1, keepdims=True)) + lambda_init)

    def heads(hp, W):
        hs = [hp * DA_HEADS_PER_STEP + u for u in range(DA_HEADS_PER_STEP)]
        qs, fars, biases, keys, values = [], [], [], [], []
        for h in hs:
            q = q_ref[0, 0, h] * (DA_DIM ** -0.5)
            zero = jnp.zeros_like(q)
            qs += [jnp.where(lane < DA_DIM, q, zero), jnp.where(lane >= DA_DIM, q, zero)]
            fars += [tbl_ref[N_BUCKETS // 2 - 1, h]] * 2
            biases += [functools.partial(lambda h, rows, lo, hi: bn_ref[h, rows, lo:hi], h)] * 2
            keys += [functools.partial(lambda h, lo, hi: k_ref[0, 0, h, lo:hi, :], h)] * 2
            values += [functools.partial(lambda h, lo, hi: v_ref[0, 0, h, lo:hi, :], h)] * 2
        outs = _attend(qs, keys, values, W, fars, biases, DA_ROWS)
        for u, h in enumerate(hs):
            o = outs[2 * u] - lam_full * outs[2 * u + 1]
            ms = jnp.mean(o * o, axis=-1, keepdims=True)
            oh_ref[h] = (o * lax.rsqrt(ms + SUBLN_EPS) * g_ref[...]) * (1.0 - lambda_init)

    for br in range(S // T):
        @pl.when(a == br)
        def _(br=br):
            def body(hp, carry):
                heads(hp, (br + 1) * T)
                return carry
            lax.fori_loop(0, H // DA_HEADS_PER_STEP, body, 0)

    for h in range(H):
        oc_ref[:, h * HEAD_DIM:(h + 1) * HEAD_DIM] = oh_ref[h].astype(_BF)
    o_ref[0] = x_ref[0] + g1_ref[0] * _dot(oc_ref[...], wo_ref[...])


def _diff_attn(qkv, rel_table, bn, lam, subln_g, lambda_init, w_out, x, g1):
    B, _, H, S, _ = qkv.shape
    T = ATT_TILE
    D = x.shape[2]
    kern = functools.partial(_da_kernel, lambda_init=lambda_init)
    part = lambda j, rows, idx: pl.BlockSpec((1, 1, H, rows, HEAD_DIM), lambda b, i: (b, j, 0, idx(i), 0))
    return pl.pallas_call(
        kern,
        grid=(B, S // T),
        in_specs=[pl.BlockSpec(memory_space=pltpu.SMEM),
                  part(0, T, lambda i: i), part(1, S, lambda i: 0), part(2, S, lambda i: 0),
                  _full((H, T, NEAR_W)),
                  _full((4, DA_DIM)), _full((1, HEAD_DIM)),
                  _full((H * HEAD_DIM, D)),
                  pl.BlockSpec((1, T, D), lambda b, i: (b, i, 0)),
                  pl.BlockSpec((1, 1, D), lambda b, i: (b, 0, 0))],
        out_specs=pl.BlockSpec((1, T, D), lambda b, i: (b, i, 0)),
        out_shape=jax.ShapeDtypeStruct((B, S, D), _F32),
        scratch_shapes=[pltpu.VMEM((H, T, HEAD_DIM), _F32), pltpu.VMEM((T, H * HEAD_DIM), _BF)],
        compiler_params=_params(("parallel", "arbitrary")),
        name="diff_attn",
    )(rel_table, qkv, qkv, qkv, bn, lam, subln_g, w_out, x, g1)


_INT_MIN = np.int32(-2 ** 31)
_KEY_NEG_INF = np.int32(np.array(-np.inf, np.float32).view(np.int32) ^ np.int32(0x7FFFFFFF))
SEARCH_GROUPS = 4
HEADS_PER_STEP = 2


def _sa_branch(W, tbl_ref, q_ref, ckv_ref, iw_ref, ikk_ref, wuk_ref, wuv_ref, bn_ref,
               iqm_ref, keys_ref, madd_ref, oh_ref, last_ref):
    T = iw_ref.shape[1]
    H = q_ref.shape[1]
    iw = iw_ref[0]

    for cb in range(W // T):
        cols = slice(cb * T, (cb + 1) * T)
        ikt = ikk_ref[0, cols, :]
        score = jnp.zeros((T, T), _F32)
        for hh in range(H):
            score = score + jnp.maximum(_dot_nt(iqm_ref[hh], ikt), 0.0) * iw[:, hh:hh + 1]
        score = jnp.where(score == 0.0, 0.0, score)
        if cb == W // T - 1:
            row = lax.broadcasted_iota(_I32, (T, T), 0)
            col = lax.broadcasted_iota(_I32, (T, T), 1)
            score = jnp.where((col >> CHUNK_SHIFT) <= (row >> CHUNK_SHIFT), score, -jnp.inf)
        bits = lax.bitcast_convert_type(score, _I32)
        keys_ref[:, cols] = bits ^ ((bits >> 31) & np.int32(0x7FFFFFFF))

    def count(pred, rows=slice(None)):
        return jnp.sum(jnp.where(pred(keys_ref[rows, 0:W]), 1.0, 0.0), axis=1, keepdims=True)

    R = T // SEARCH_GROUPS

    def bit_body(it, thrs):
        bit = lax.shift_left(np.int32(1), np.int32(31) - it)
        out = []
        for g in range(SEARCH_GROUPS):
            cand_u = thrs[g] | bit
            cand = cand_u ^ _INT_MIN
            cnt = count(lambda k: k >= cand, slice(g * R, (g + 1) * R))
            out.append(jnp.where(cnt >= float(TOPK), cand_u, thrs[g]))
        return tuple(out)

    thrs = lax.fori_loop(0, 32, bit_body, tuple(jnp.zeros((R, 1), _I32) for _ in range(SEARCH_GROUPS)))
    thr = jnp.concatenate(thrs, axis=0) ^ _INT_MIN

    cnt_gt = count(lambda k: k > thr)
    cnt_ge = count(lambda k: k >= thr)
    need = float(TOPK) - cnt_gt
    last_ref[...] = jnp.full((T, 1), 2 ** 30, _I32)
    tied = jnp.logical_and(cnt_ge - cnt_gt > need, thr > _KEY_NEG_INF)

    @pl.when(jnp.max(jnp.where(tied, 1.0, 0.0)) > 0.0)
    def _():
        col = lax.broadcasted_iota(_I32, (T, W), 1)

        def idx_body(it, v):
            cand = v | lax.shift_left(np.int32(1), np.int32(10) - it)
            below = count(lambda k: jnp.logical_and(k == thr, col < cand))
            return jnp.where(below < need, cand, v)
        last_ref[...] = lax.fori_loop(0, 11, idx_body, jnp.zeros((T, 1), _I32))

    last = last_ref[...]
    k = keys_ref[:, 0:W]
    col = lax.broadcasted_iota(_I32, (T, W), 1)
    sel = jnp.logical_or(k > thr, jnp.logical_and(k == thr, col <= last))
    madd_ref[:, 0:W] = jnp.where(sel, 0.0, NEG)

    def head_body(hp, carry):
        hs = [hp * HEADS_PER_STEP + u for u in range(HEADS_PER_STEP)]
        qlats = [(_dot(q_ref[0, h], wuk_ref[h]) * (HEAD_DIM ** -0.5)).astype(_BF) for h in hs]
        latents = lambda lo, hi: ckv_ref[0, lo:hi, :]
        olats = _attend(qlats, latents, latents, W,
                        [tbl_ref[N_BUCKETS // 2 - 1, h] for h in hs],
                        [functools.partial(lambda h, rows, lo, hi: bn_ref[h, rows, lo:hi], h) for h in hs],
                        SA_ROWS, madd_ref)
        for h, olat in zip(hs, olats):
            oh_ref[h] = _dot(olat.astype(_BF), wuv_ref[h])
        return carry

    lax.fori_loop(0, H // HEADS_PER_STEP, head_body, 0)


def _sa_kernel(tbl_ref, q_ref, ckv_ref, iq_ref, ikk_ref, iw_ref, wuk_ref, wuv_ref, bn_ref, wo_ref, x_ref, g1_ref,
               o_ref, iqm_ref, keys_ref, madd_ref, oh_ref, last_ref, oc_ref):
    a = pl.program_id(1)
    T = iq_ref.shape[1]
    S = ckv_ref.shape[1]
    H = q_ref.shape[1]
    lane = lax.broadcasted_iota(_I32, (T, 128), 1)
    for p in range(H // 2):
        iqp = iq_ref[0, :, p * 128:(p + 1) * 128]
        zero = jnp.zeros_like(iqp)
        iqm_ref[2 * p] = jnp.where(lane < IDX_DIM, iqp, zero)
        iqm_ref[2 * p + 1] = jnp.where(lane >= IDX_DIM, iqp, zero)

    for br in range(S // T):
        @pl.when(a == br)
        def _(br=br):
            _sa_branch((br + 1) * T, tbl_ref, q_ref, ckv_ref, iw_ref, ikk_ref, wuk_ref, wuv_ref, bn_ref,
                       iqm_ref, keys_ref, madd_ref, oh_ref, last_ref)

    for h in range(H):
        oc_ref[:, h * HEAD_DIM:(h + 1) * HEAD_DIM] = oh_ref[h].astype(_BF)
    o_ref[0] = x_ref[0] + g1_ref[0] * _dot(oc_ref[...], wo_ref[...])


def _sparse_attn(q, ckv, iq, ikk, iw, wuk, wuv, rel_table, bn, w_out, x, g1):
    B, H, S, _ = q.shape
    T = ATT_TILE
    D = x.shape[2]
    return pl.pallas_call(
        _sa_kernel,
        grid=(B, S // T),
        in_specs=[pl.BlockSpec(memory_space=pltpu.SMEM),
                  pl.BlockSpec((1, H, T, HEAD_DIM), lambda b, i: (b, 0, i, 0)),
                  pl.BlockSpec((1, S, SA_LATENT), lambda b, i: (b, 0, 0)),
                  pl.BlockSpec((1, T, 512), lambda b, i: (b, i, 0)),
                  pl.BlockSpec((1, S, 128), lambda b, i: (b, 0, 0)),
                  pl.BlockSpec((1, T, 128), lambda b, i: (b, i, 0)),
                  _full((H, HEAD_DIM, SA_LATENT)),
                  _full((H, SA_LATENT, HEAD_DIM)),
                  _full((H, T, NEAR_W)),
                  _full((H * HEAD_DIM, D)),
                  pl.BlockSpec((1, T, D), lambda b, i: (b, i, 0)),
                  pl.BlockSpec((1, 1, D), lambda b, i: (b, 0, 0))],
        out_specs=pl.BlockSpec((1, T, D), lambda b, i: (b, i, 0)),
        out_shape=jax.ShapeDtypeStruct((B, S, D), _F32),
        scratch_shapes=[pltpu.VMEM((H, T, 128), _BF),
                        pltpu.VMEM((T, S), _I32),
                        pltpu.VMEM((T, S), _F32),
                        pltpu.VMEM((H, T, HEAD_DIM), _F32),
                        pltpu.VMEM((T, 1), _I32),
                        pltpu.VMEM((T, H * HEAD_DIM), _BF)],
        compiler_params=_params(("parallel", "arbitrary")),
        name="sparse_attn",
    )(rel_table, q, ckv, iq, ikk, iw, wuk, wuv, bn, w_out, x, g1)


def _cv_kernel(x_ref, g_ref, sc_ref, sh_ref, w1_ref, b1_ref, wdw_ref, bdw_ref, lg_ref, lb_ref, w2_ref, b2_ref,
               g1_ref, o_ref, ext_ref, y_ref):
    i = pl.program_id(1)
    tm = x_ref.shape[1]
    D = x_ref.shape[2]
    n_ext = CONV_HALO + tm

    @pl.when(i == 0)
    def _():
        ext_ref[0, 0:CONV_HALO] = jnp.zeros((CONV_HALO, D), _F32)

    @pl.when(i > 0)
    def _():
        ext_ref[0, 0:CONV_HALO] = ext_ref[0, tm:n_ext]

    hn = _rms_mod(x_ref[0], g_ref[...], sc_ref[0], sh_ref[0]).astype(_BF)
    tn = 512
    n = D // tn
    lo = lambda c: slice(c * tn, (c + 1) * tn)
    hi = lambda c: slice(D + c * tn, D + (c + 1) * tn)
    dots = lambda c: (_dot(hn, w1_ref[:, lo(c)]), _dot(hn, w1_ref[:, hi(c)]))
    nxt = dots(0)
    for c in range(n):
        cur = nxt
        if c + 1 < n:
            nxt = dots(c + 1)
        ext_ref[0, CONV_HALO:n_ext, lo(c)] = ((cur[0] + b1_ref[:, lo(c)])
                                              * jax.nn.sigmoid(cur[1] + b1_ref[:, hi(c)]))
    for r in range(1, SUBLANES):
        ext_ref[r, 0:n_ext - SUBLANES] = ext_ref[0, r:r + n_ext - SUBLANES]
    rb = CONV_ROWS
    base = CONV_HALO - (CONV_WIDTH - 1)

    for row0 in range(0, tm, rb):
        ns = rb // SUBLANES
        accs = [jnp.broadcast_to(bdw_ref[...], (SUBLANES, bdw_ref.shape[1]))] * ns
        for k in range(CONV_WIDTH):
            r = (base + k) % SUBLANES
            al = base + k - r
            w = wdw_ref[k]
            xk = ext_ref[r, row0 + al:row0 + al + rb, :]
            accs = [a + w * xk[s * SUBLANES:(s + 1) * SUBLANES] for s, a in enumerate(accs)]
        acc = jnp.concatenate(accs, axis=0)
        y_ref[row0:row0 + rb] = _silu(_layernorm(acc, lg_ref[...], lb_ref[...]))
    y = _dot(y_ref[...].astype(_BF), w2_ref[...]) + b2_ref[...]
    o_ref[0] = x_ref[0] + g1_ref[0] * y


def _cv_mixer(x, g, sc, sh, w1, b1, w_dw, b_dw, ln_g, ln_b, w2, b2, g1):
    B, S, D = x.shape
    tm = 512
    vec = lambda: _full((1, D))
    return pl.pallas_call(
        _cv_kernel,
        grid=(B, S // tm),
        in_specs=_in_specs_common(tm, D) + [_full((D, 2 * D)), _full((1, 2 * D)),
                                            _full((CONV_WIDTH, SUBLANES, D)), vec(), vec(), vec(),
                                            _full((D, D)), vec(),
                                            pl.BlockSpec((1, 1, D), lambda b, i: (b, 0, 0))],
        out_specs=pl.BlockSpec((1, tm, D), lambda b, i: (b, i, 0)),
        out_shape=jax.ShapeDtypeStruct((B, S, D), _F32),
        scratch_shapes=[pltpu.VMEM((SUBLANES, CONV_HALO + tm, D), _F32), pltpu.VMEM((tm, D), _F32)],
        compiler_params=_params(("parallel", "arbitrary")),
        name="cv_mixer",
    )(x, g, sc, sh, w1, b1, w_dw, b_dw, ln_g, ln_b, w2, b2, g1)


def _ffn_kernel(x_ref, xh_ref, g_ref, sc_ref, sh_ref, g2_ref, wup_ref, wdw_ref, bdw_ref, wdn_ref,
                fg_ref, o_ref, he_ref, act_ref, acc_ref, *, final):
    i = pl.program_id(1)
    tm = x_ref.shape[1]
    F = wdn_ref.shape[0]
    x = x_ref[0]
    he_ref[FFN_HALO:FFN_HALO + tm] = _rms_mod(x, g_ref[...], sc_ref[0], sh_ref[0]).astype(_BF)
    hh = _rms_mod(xh_ref[0], g_ref[...], sc_ref[0], sh_ref[0])
    he_ref[0:FFN_HALO] = jnp.where(i > 0, hh, jnp.zeros_like(hh)).astype(_BF)
    tf = FFN_CHUNK
    n_chunks = F // tf

    def up(c):
        he = he_ref[...]
        return (_dot(he, wup_ref[:, c * tf:(c + 1) * tf]), _dot(he, wup_ref[:, F + c * tf:F + (c + 1) * tf]))

    def conv(a, col0):
        cols = slice(col0, col0 + tf)
        w = wdw_ref[:, cols]
        return (a[FFN_HALO - 2:FFN_HALO - 2 + tm] * w[0:1]
                + a[FFN_HALO - 1:FFN_HALO - 1 + tm] * w[1:2]
                + a[FFN_HALO:FFN_HALO + tm] * w[2:3] + bdw_ref[:, cols])

    nxt = up(0)
    for c in range(n_chunks):
        cur = nxt
        if c + 1 < n_chunks:
            nxt = up(c + 1)
        act = _silu(conv(cur[0], c * tf)) * conv(cur[1], F + c * tf)
        act_ref[:, c * tf:(c + 1) * tf] = act.astype(_BF)
        if (c + 1) % FFN_DOWN_GROUP == 0 or c + 1 == n_chunks:
            lo = (c // FFN_DOWN_GROUP) * FFN_DOWN_GROUP * tf
            part = _dot(act_ref[:, lo:(c + 1) * tf], wdn_ref[lo:(c + 1) * tf, :])
            if lo == 0:
                acc_ref[...] = part
            else:
                acc_ref[...] += part
    xn = x + g2_ref[0] * acc_ref[...]
    if final:
        ms = jnp.mean(xn * xn, axis=-1, keepdims=True)
        xn = xn * lax.rsqrt(ms + EPS) * fg_ref[...]
    o_ref[0] = xn


def _ffn(x, g, sc, sh, g2, w_up, w_dw, b_dw, w_down, final_g, final, layer):
    B, S, D = x.shape
    F = w_down.shape[1]
    tm = FFN_ROWS
    hb = tm // FFN_HALO
    kern = functools.partial(_ffn_kernel, final=final)
    mod = lambda: pl.BlockSpec((1, 1, D), lambda b, i: (b, 0, 0))
    of_layer = lambda r, c: pl.BlockSpec((None, r, c), lambda b, i: (layer, 0, 0), pipeline_mode=pl.Buffered(1))
    return pl.pallas_call(
        kern,
        grid=(B, S // tm),
        in_specs=[pl.BlockSpec((1, tm, D), lambda b, i: (b, i, 0)),
                  pl.BlockSpec((1, FFN_HALO, D), lambda b, i: (b, jnp.maximum(i * hb - 1, 0), 0)),
                  _full((1, D)), mod(), mod(), mod(),
                  of_layer(D, 2 * F), _full((3, 2 * F)), _full((1, 2 * F)), of_layer(F, D),
                  _full((1, D))],
        out_specs=pl.BlockSpec((1, tm, D), lambda b, i: (b, i, 0)),
        out_shape=jax.ShapeDtypeStruct((B, S, D), _F32),
        scratch_shapes=[pltpu.VMEM((FFN_HALO + tm, D), _BF), pltpu.VMEM((tm, F), _BF),
                        pltpu.VMEM((tm, D), _F32)],
        compiler_params=_params(("parallel", "parallel")),
        name="ffn",
    )(x, x, g, sc, sh, g2, w_up, w_dw, b_dw, w_down, final_g)


def kernel(x, c, rel_table, ada_w, ada_b, norm_g, final_g, da_w_in, da_lam, da_subln_g, da_w_out, cv_w_pw1, cv_b_pw1, cv_w_dw, cv_b_dw, cv_ln_g, cv_ln_b, cv_w_pw2, cv_b_pw2, sa_w_in, sa_kv_g, sa_w_uk, sa_w_uv, sa_w_out, sg_w_in, sg_b_in, sg_ln_g, sg_ln_b, sg_w_s, sg_b_s, sg_w_out, sg_b_out, ff_w_up, ff_w_dw, ff_b_dw, ff_w_down):
    B, S, D = x.shape
    depth = ada_w.shape[0]
    n_mixers = 4
    H = N_HEADS
    mods = _ada(c, ada_w, ada_b)
    bn = _bias_tiles(rel_table)
    ff_up = ff_w_up.astype(_BF)
    ff_down = ff_w_down.astype(_BF)
    row = lambda v: v.reshape(1, -1)

    for layer in range(depth):
        kind = layer % n_mixers
        j = layer // n_mixers
        sh1, sc1, g1, sh2, sc2, g2 = [m.reshape(B, 1, D) for m in jnp.split(mods[layer], 6, axis=-1)]
        ng1 = row(norm_g[layer, 0])
        if kind == 0:
            lambda_init = 0.8 - 0.6 * math.exp(-0.3 * layer)
            qkv = _in_da(x, ng1, sc1, sh1, da_w_in[j].astype(_BF))
            x = _diff_attn(qkv, rel_table, bn, da_lam[j], row(da_subln_g[j]), lambda_init,
                           da_w_out[j].astype(_BF), x, g1)
        elif kind == 1:
            w_dw = jnp.broadcast_to(cv_w_dw[j][:, None, :], (CONV_WIDTH, SUBLANES, D))
            x = _cv_mixer(x, ng1, sc1, sh1, cv_w_pw1[j].astype(_BF), row(cv_b_pw1[j]), w_dw, row(cv_b_dw[j]),
                          row(cv_ln_g[j]), row(cv_ln_b[j]), cv_w_pw2[j].astype(_BF), row(cv_b_pw2[j]), g1)
        elif kind == 2:
            w = sa_w_in[j]
            o1 = H * HEAD_DIM
            o2 = o1 + SA_LATENT
            o3 = o2 + H * IDX_DIM
            o4 = o3 + IDX_DIM
            w_ik = w[:, o3:o4]
            w_iw = jnp.pad(w[:, o4:], ((0, 0), (0, 128 - (w.shape[1] - o4))))
            w_cat = jnp.concatenate([w[:, :o3], w_ik, w_ik, w_iw], axis=1).astype(_BF)
            q, ckv, iq, ikk, iw = _in_sa(x, ng1, sc1, sh1, w_cat, row(sa_kv_g[j]))
            wuk = jnp.transpose(sa_w_uk[j], (1, 2, 0)).astype(_BF)
            wuv = jnp.transpose(sa_w_uv[j], (1, 0, 2)).astype(_BF)
            x = _sparse_attn(q, ckv, iq, ikk, iw, wuk, wuv, rel_table, bn, sa_w_out[j].astype(_BF), x, g1)
        else:
            x = _sg_mixer(x, ng1, sc1, sh1, sg_w_in[j].astype(_BF), row(sg_b_in[j]), row(sg_ln_g[j]),
                          row(sg_ln_b[j]), sg_w_s[j], sg_b_s[j].T, sg_w_out[j].astype(_BF),
                          row(sg_b_out[j]), g1)
        x = _ffn(x, row(norm_g[layer, 1]), sc2, sh2, g2, ff_up, ff_w_dw[layer],
                 row(ff_b_dw[layer]), ff_down, row(final_g), layer == depth - 1, layer)
    return x
```

```python
import functools
import math

import jax
import jax.numpy as jnp
import numpy as np
from jax import lax
from jax.experimental import pallas as pl
from jax.experimental.pallas import tpu as pltpu

_BF = jnp.bfloat16
_F32 = jnp.float32
_I32 = jnp.int32

EPS = 1e-6
SUBLN_EPS = 1e-5
NEG = -1e30
CHUNK = 64
CHUNK_SHIFT = 6
N_HEADS = 8
HEAD_DIM = 128
DA_DIM = 64
SA_LATENT = 256
IDX_DIM = 64
TOPK = 256
N_BUCKETS = 32
MAX_DISTANCE = 128
SG_CHUNK = 128
CONV_WIDTH = 31
CONV_HALO = 32
CONV_ROWS = 8
SUBLANES = 8
FFN_HALO = 8
PROJ_ROWS = 1024
SG_ROWS = 1024
FFN_ROWS = 1024
FFN_CHUNK = 256
FFN_DOWN_GROUP = 11
ATT_TILE = 512
DA_HEADS_PER_STEP = 1
DA_ROWS = 128
SA_ROWS = 256
NEAR_W = 768
VMEM_LIMIT = 56 * 1024 * 1024


def _dot(a, b):
    return jnp.dot(a, b, preferred_element_type=_F32)


def _dot_nt(a, b):
    return lax.dot_general(a, b, (((1,), (1,)), ((), ())), preferred_element_type=_F32)


def _rms_mod(x, g, sc, sh):
    ms = jnp.mean(x * x, axis=-1, keepdims=True)
    return (x * lax.rsqrt(ms + EPS) * g) * (1.0 + sc) + sh


def _layernorm(x, g, b):
    mu = jnp.mean(x, axis=-1, keepdims=True)
    xc = x - mu
    var = jnp.mean(xc * xc, axis=-1, keepdims=True)
    return xc * lax.rsqrt(var + EPS) * g + b


def _silu(x):
    return x * jax.nn.sigmoid(x)


def _params(sem):
    return pltpu.CompilerParams(dimension_semantics=sem, vmem_limit_bytes=VMEM_LIMIT)


def _full(shape):
    n = len(shape)
    return pl.BlockSpec(shape, lambda *_: (0,) * n, pipeline_mode=pl.Buffered(1))


def _ada_kernel(c_ref, w_ref, b_ref, o_ref):
    ca = _silu(c_ref[...]).astype(_BF)
    o_ref[0] = _dot(ca, w_ref[0].astype(_BF)) + b_ref[0]


def _ada(c, ada_w, ada_b):
    L, D, N = ada_w.shape
    B = c.shape[0]
    tn = N // 4
    return pl.pallas_call(
        _ada_kernel,
        grid=(L, N // tn),
        in_specs=[pl.BlockSpec((B, D), lambda l, j: (0, 0)),
                  pl.BlockSpec((1, D, tn), lambda l, j: (l, 0, j)),
                  pl.BlockSpec((1, 1, tn), lambda l, j: (l, 0, j))],
        out_specs=pl.BlockSpec((1, B, tn), lambda l, j: (l, 0, j)),
        out_shape=jax.ShapeDtypeStruct((L, B, N), _F32),
        compiler_params=_params(("arbitrary", "arbitrary")),
        name="ada",
    )(c, ada_w, ada_b.reshape(L, 1, N))


def _bias_kernel(tbl_ref, bn_ref):
    h = pl.program_id(0)
    shape = bn_ref.shape[1:]
    i = lax.broadcasted_iota(_I32, shape, 0)
    j = lax.broadcasted_iota(_I32, shape, 1) - (NEAR_W - ATT_TILE)
    nb = N_BUCKETS // 2
    max_exact = nb // 2
    rel = j - i
    ret = jnp.where(rel > 0, nb, 0)
    n = jnp.abs(rel)
    nf = jnp.maximum(n, 1).astype(_F32)
    large = max_exact + (jnp.log(nf / max_exact) / math.log(MAX_DISTANCE / max_exact)
                         * (nb - max_exact)).astype(_I32)
    large = jnp.minimum(large, nb - 1)
    bucket = ret + jnp.where(n < max_exact, n, large)
    out = jnp.zeros(shape, _F32)
    for bk in range(N_BUCKETS):
        out = jnp.where(bucket == bk, tbl_ref[bk, h], out)
    visible = (j >> CHUNK_SHIFT) <= (i >> CHUNK_SHIFT)
    bn_ref[0] = jnp.where(visible, out, NEG)


def _bias_tiles(rel_table):
    H = rel_table.shape[1]
    return pl.pallas_call(
        _bias_kernel,
        grid=(H,),
        in_specs=[pl.BlockSpec(memory_space=pltpu.SMEM)],
        out_specs=pl.BlockSpec((1, ATT_TILE, NEAR_W), lambda h: (h, 0, 0)),
        out_shape=jax.ShapeDtypeStruct((H, ATT_TILE, NEAR_W), _F32),
        compiler_params=_params(("arbitrary",)),
        name="bias_tiles",
    )(rel_table)


def _in_specs_common(tm, D):
    return [pl.BlockSpec((1, tm, D), lambda b, i: (b, i, 0)),
            pl.BlockSpec((1, D), lambda b, i: (0, 0)),
            pl.BlockSpec((1, 1, D), lambda b, i: (b, 0, 0)),
            pl.BlockSpec((1, 1, D), lambda b, i: (b, 0, 0))]


def _in_da_kernel(x_ref, g_ref, sc_ref, sh_ref, w_ref, o_ref):
    hn = _rms_mod(x_ref[0], g_ref[...], sc_ref[0], sh_ref[0]).astype(_BF)
    N = w_ref.shape[1]
    H = o_ref.shape[2]
    tn = 512
    per = tn // HEAD_DIM
    for c in range(N // tn):
        r = _dot(hn, w_ref[:, c * tn:(c + 1) * tn])
        for k in range(per):
            idx = c * per + k
            o_ref[0, idx // H, idx % H] = r[:, k * HEAD_DIM:(k + 1) * HEAD_DIM].astype(_BF)


def _in_da(x, g, sc, sh, w):
    B, S, D = x.shape
    N = w.shape[1]
    H = N_HEADS
    tm = PROJ_ROWS
    return pl.pallas_call(
        _in_da_kernel,
        grid=(B, S // tm),
        in_specs=_in_specs_common(tm, D) + [_full((D, N))],
        out_specs=pl.BlockSpec((1, 3, H, tm, HEAD_DIM), lambda b, i: (b, 0, 0, i, 0)),
        out_shape=jax.ShapeDtypeStruct((B, 3, H, S, HEAD_DIM), _BF),
        compiler_params=_params(("parallel", "parallel")),
        name="in_da",
    )(x, g, sc, sh, w)


def _in_sa_kernel(x_ref, g_ref, sc_ref, sh_ref, w_ref, kvg_ref,
                  q_ref, ckv_ref, iq_ref, ikk_ref, iw_ref):
    hn = _rms_mod(x_ref[0], g_ref[...], sc_ref[0], sh_ref[0]).astype(_BF)
    H = q_ref.shape[1]
    for c in range(H // 2):
        r = _dot(hn, w_ref[:, c * 256:(c + 1) * 256])
        q_ref[0, 2 * c] = r[:, :HEAD_DIM].astype(_BF)
        q_ref[0, 2 * c + 1] = r[:, HEAD_DIM:].astype(_BF)
    o = H * HEAD_DIM
    ckv = _dot(hn, w_ref[:, o:o + SA_LATENT])
    ms = jnp.mean(ckv * ckv, axis=-1, keepdims=True)
    ckv_ref[0] = (ckv * lax.rsqrt(ms + EPS) * kvg_ref[...]).astype(_BF)
    o += SA_LATENT
    iq_ref[0] = (_dot(hn, w_ref[:, o:o + 512]) * (IDX_DIM ** -0.5)).astype(_BF)
    o += 512
    r = _dot(hn, w_ref[:, o:o + 256])
    ikk_ref[0] = r[:, :128].astype(_BF)
    iw_ref[0] = r[:, 128:] * (N_HEADS ** -0.5)


def _in_sa(x, g, sc, sh, w, kv_g):
    B, S, D = x.shape
    N = w.shape[1]
    H = N_HEADS
    tm = PROJ_ROWS
    row = lambda n: pl.BlockSpec((1, tm, n), lambda b, i: (b, i, 0))
    return pl.pallas_call(
        _in_sa_kernel,
        grid=(B, S // tm),
        in_specs=_in_specs_common(tm, D) + [_full((D, N)), _full((1, SA_LATENT))],
        out_specs=[pl.BlockSpec((1, H, tm, HEAD_DIM), lambda b, i: (b, 0, i, 0)),
                   row(SA_LATENT), row(512), row(128), row(128)],
        out_shape=[jax.ShapeDtypeStruct((B, H, S, HEAD_DIM), _BF),
                   jax.ShapeDtypeStruct((B, S, SA_LATENT), _BF),
                   jax.ShapeDtypeStruct((B, S, 512), _BF),
                   jax.ShapeDtypeStruct((B, S, 128), _BF),
                   jax.ShapeDtypeStruct((B, S, 128), _F32)],
        compiler_params=_params(("parallel", "parallel")),
        name="in_sa",
    )(x, g, sc, sh, w, kv_g)


def _sg_kernel(x_ref, g_ref, sc_ref, sh_ref, w_ref, b_ref, lg_ref, lb_ref, ws_ref, bs_ref, wo_ref, bo_ref,
               g1_ref, o_ref, a_scr, v_scr, z_scr):
    x = x_ref[0]
    hn = _rms_mod(x, g_ref[...], sc_ref[0], sh_ref[0]).astype(_BF)
    tm = x.shape[0]
    W = a_scr.shape[1]
    G = ws_ref.shape[0]
    C = SG_CHUNK
    gw = W // G
    tn = 512
    n = W // tn
    lo = lambda c: slice(c * tn, (c + 1) * tn)
    hi = lambda c: slice(W + c * tn, W + (c + 1) * tn)
    ti = lax.broadcasted_iota(_I32, (C, C), 0)
    si = lax.broadcasted_iota(_I32, (C, C), 1)
    cols = [hi(c) for c in range(n)] + [lo(c) for c in range(n)]
    nxt = _dot(hn, w_ref[:, cols[0]])
    for t in range(2 * n):
        cur = nxt
        if t + 1 < 2 * n:
            nxt = _dot(hn, w_ref[:, cols[t + 1]])
        act = jax.nn.gelu(cur + b_ref[:, cols[t]], approximate=True)
        if t < n:
            a_scr[:, lo(t)] = act
            if t == n - 1:
                v_scr[...] = _layernorm(a_scr[...], lg_ref[...], lb_ref[...]).astype(_BF)
            continue
        for gl in range(tn // gw):
            g = (t - n) * (tn // gw) + gl
            wg = jnp.where(ti >= si, ws_ref[g], 0.0).astype(_BF)
            bcol = bs_ref[:, g:g + 1]
            gcols = slice(g * gw, (g + 1) * gw)
            for r in range(tm // C):
                rows = slice(r * C, (r + 1) * C)
                sv = _dot(wg, v_scr[rows, gcols]) + bcol
                z_scr[rows, gcols] = (act[rows, gl * gw:(gl + 1) * gw] * sv).astype(_BF)
    y = _dot(z_scr[...], wo_ref[...]) + bo_ref[...]
    o_ref[0] = x + g1_ref[0] * y


def _sg_mixer(x, g, sc, sh, w, b, ln_g, ln_b, w_s, b_s_t, w_out, b_out, g1):
    B, S, D = x.shape
    N = w.shape[1]
    W = N // 2
    G = w_s.shape[0]
    tm = SG_ROWS
    return pl.pallas_call(
        _sg_kernel,
        grid=(B, S // tm),
        in_specs=_in_specs_common(tm, D) + [_full((D, N)), _full((1, N)), _full((1, W)), _full((1, W)),
                                            _full((G, SG_CHUNK, SG_CHUNK)), _full((SG_CHUNK, G)),
                                            _full((W, D)), _full((1, D)),
                                            pl.BlockSpec((1, 1, D), lambda b, i: (b, 0, 0))],
        out_specs=pl.BlockSpec((1, tm, D), lambda b, i: (b, i, 0)),
        out_shape=jax.ShapeDtypeStruct((B, S, D), _F32),
        scratch_shapes=[pltpu.VMEM((tm, W), _F32), pltpu.VMEM((tm, W), _BF), pltpu.VMEM((tm, W), _BF)],
        compiler_params=_params(("parallel", "parallel")),
        name="sg_mixer",
    )(x, g, sc, sh, w, b, ln_g, ln_b, w_s, b_s_t, w_out, b_out, g1)


def _attend(qs, keys, values, W, fars, near_biases, block_rows, madd_ref=None):
    nw = min(W, NEAR_W)
    fw = W - nw
    lo = NEAR_W - nw
    n_rows = qs[0].shape[0]
    blocks = [slice(r, r + block_rows) for r in range(0, n_rows, block_rows)]
    per_q = lambda f: f if isinstance(f, (list, tuple)) else [f] * len(qs)
    logits = []
    for q, far, near_bias, keys, values in zip(qs, fars, near_biases, per_q(keys), per_q(values)):
        for rows in blocks:
            we = W - (n_rows - rows.stop)
            s_n = _dot_nt(q[rows], keys(fw, we)) + near_bias(rows, lo, lo + we - fw)
            s_f = None
            if fw:
                s_f = _dot_nt(q[rows], keys(0, fw)) + far
            if madd_ref is not None:
                s_n = s_n + madd_ref[rows, fw:we]
                if fw:
                    s_f = s_f + madd_ref[rows, 0:fw]
            logits.append((s_n, s_f, we, values))
    outs = []
    for s_n, s_f, we, values in logits:
        m = jnp.max(s_n, axis=1, keepdims=True)
        if fw:
            m = jnp.maximum(m, jnp.max(s_f, axis=1, keepdims=True))
            p_f = jnp.exp(s_f - m)
        p_n = jnp.exp(s_n - m)
        l = jnp.sum(p_n, axis=1, keepdims=True)
        o = _dot(p_n.astype(_BF), values(fw, we))
        if fw:
            l = l + jnp.sum(p_f, axis=1, keepdims=True)
            o = o + _dot(p_f.astype(_BF), values(0, fw))
        outs.append(o * (1.0 / l))
    nb = len(blocks)
    return [jnp.concatenate(outs[i * nb:(i + 1) * nb], axis=0) for i in range(len(qs))]


def _da_kernel(tbl_ref, q_ref, k_ref, v_ref, bn_ref, lam_ref, g_ref, wo_ref, x_ref, g1_ref, o_ref,
               oh_ref, oc_ref, *, lambda_init):
    a = pl.program_id(1)
    H = q_ref.shape[2]
    T = q_ref.shape[3]
    S = k_ref.shape[3]
    lane = lax.broadcasted_iota(_I32, (T, HEAD_DIM), 1)
    lam = lam_ref[...]
    lam_full = (jnp.exp(jnp.sum(lam[0:1] * lam[1:2], axis=1, keepdims=True))
                - jnp.exp(jnp.sum(lam[2:3] * lam[3:4], axis=1, keepdims=True)) + lambda_init)

    def heads(hp, W):
        hs = [hp * DA_HEADS_PER_STEP + u for u in range(DA_HEADS_PER_STEP)]
        qs, fars, biases, keys, values = [], [], [], [], []
        for h in hs:
            q = q_ref[0, 0, h] * (DA_DIM ** -0.5)
            zero = jnp.zeros_like(q)
            qs += [jnp.where(lane < DA_DIM, q, zero), jnp.where(lane >= DA_DIM, q, zero)]
            fars += [tbl_ref[N_BUCKETS // 2 - 1, h]] * 2
            biases += [functools.partial(lambda h, rows, lo, hi: bn_ref[h, rows, lo:hi], h)] * 2
            keys += [functools.partial(lambda h, lo, hi: k_ref[0, 0, h, lo:hi, :], h)] * 2
            values += [functools.partial(lambda h, lo, hi: v_ref[0, 0, h, lo:hi, :], h)] * 2
        outs = _attend(qs, keys, values, W, fars, biases, DA_ROWS)
        for u, h in enumerate(hs):
            o = outs[2 * u] - lam_full * outs[2 * u + 1]
            ms = jnp.mean(o * o, axis=-1, keepdims=True)
            oh_ref[h] = (o * lax.rsqrt(ms + SUBLN_EPS) * g_ref[...]) * (1.0 - lambda_init)

    for br in range(S // T):
        @pl.when(a == br)
        def _(br=br):
            def body(hp, carry):
                heads(hp, (br + 1) * T)
                return carry
            lax.fori_loop(0, H // DA_HEADS_PER_STEP, body, 0)

    for h in range(H):
        oc_ref[:, h * HEAD_DIM:(h + 1) * HEAD_DIM] = oh_ref[h].astype(_BF)
    o_ref[0] = x_ref[0] + g1_ref[0] * _dot(oc_ref[...], wo_ref[...])


def _diff_attn(qkv, rel_table, bn, lam, subln_g, lambda_init, w_out, x, g1):
    B, _, H, S, _ = qkv.shape
    T = ATT_TILE
    D = x.shape[2]
    kern = functools.partial(_da_kernel, lambda_init=lambda_init)
    part = lambda j, rows, idx: pl.BlockSpec((1, 1, H, rows, HEAD_DIM), lambda b, i: (b, j, 0, idx(i), 0))
    return pl.pallas_call(
        kern,
        grid=(B, S // T),
        in_specs=[pl.BlockSpec(memory_space=pltpu.SMEM),
                  part(0, T, lambda i: i), part(1, S, lambda i: 0), part(2, S, lambda i: 0),
                  _full((H, T, NEAR_W)),
                  _full((4, DA_DIM)), _full((1, HEAD_DIM)),
                  _full((H * HEAD_DIM, D)),
                  pl.BlockSpec((1, T, D), lambda b, i: (b, i, 0)),
                  pl.BlockSpec((1, 1, D), lambda b, i: (b, 0, 0))],
        out_specs=pl.BlockSpec((1, T, D), lambda b, i: (b, i, 0)),
        out_shape=jax.ShapeDtypeStruct((B, S, D), _F32),
        scratch_shapes=[pltpu.VMEM((H, T, HEAD_DIM), _F32), pltpu.VMEM((T, H * HEAD_DIM), _BF)],
        compiler_params=_params(("parallel", "arbitrary")),
        name="diff_attn",
    )(rel_table, qkv, qkv, qkv, bn, lam, subln_g, w_out, x, g1)


_INT_MIN = np.int32(-2 ** 31)
_KEY_NEG_INF = np.int32(np.array(-np.inf, np.float32).view(np.int32) ^ np.int32(0x7FFFFFFF))
SEARCH_GROUPS = 4
HEADS_PER_STEP = 2


def _sa_branch(W, tbl_ref, q_ref, ckv_ref, iw_ref, ikk_ref, wuk_ref, wuv_ref, bn_ref,
               iqm_ref, keys_ref, madd_ref, oh_ref):
    T = iw_ref.shape[1]
    H = q_ref.shape[1]
    iw = iw_ref[0]

    for cb in range(W // T):
        cols = slice(cb * T, (cb + 1) * T)
        ikt = ikk_ref[0, cols, :]
        score = jnp.zeros((T, T), _F32)
        for hh in range(H):
            score = score + jnp.maximum(_dot_nt(iqm_ref[hh], ikt), 0.0) * iw[:, hh:hh + 1]
        score = jnp.where(score == 0.0, 0.0, score)
        if cb == W // T - 1:
            row = lax.broadcasted_iota(_I32, (T, T), 0)
            col = lax.broadcasted_iota(_I32, (T, T), 1)
            score = jnp.where((col >> CHUNK_SHIFT) <= (row >> CHUNK_SHIFT), score, -jnp.inf)
        bits = lax.bitcast_convert_type(score, _I32)
        keys_ref[:, cols] = bits ^ ((bits >> 31) & np.int32(0x7FFFFFFF))

    def count(pred, rows=slice(None)):
        return jnp.sum(jnp.where(pred(keys_ref[rows, 0:W]), 1.0, 0.0), axis=1, keepdims=True)

    R = T // SEARCH_GROUPS

    def bit_body(it, carry):
        thrs, cnts = carry
        bit = lax.shift_left(np.int32(1), np.int32(31) - it)
        new_thrs, new_cnts = [], []
        for g in range(SEARCH_GROUPS):
            cand_u = thrs[g] | bit
            cand = cand_u ^ _INT_MIN
            cnt = count(lambda k: k >= cand, slice(g * R, (g + 1) * R))
            take = cnt >= float(TOPK)
            new_thrs.append(jnp.where(take, cand_u, thrs[g]))
            new_cnts.append(jnp.where(take, cnt, cnts[g]))
        return tuple(new_thrs), tuple(new_cnts)

    thrs, cnts = lax.fori_loop(0, 32, bit_body,
                               (tuple(jnp.zeros((R, 1), _I32) for _ in range(SEARCH_GROUPS)),
                                tuple(jnp.full((R, 1), float(W), _F32) for _ in range(SEARCH_GROUPS))))
    thr = jnp.concatenate(thrs, axis=0) ^ _INT_MIN
    cnt_ge = jnp.concatenate(cnts, axis=0)

    tied = jnp.logical_and(cnt_ge > float(TOPK), thr > _KEY_NEG_INF)
    any_tied = jnp.max(jnp.where(tied, 1.0, 0.0)) > 0.0

    @pl.when(any_tied)
    def _():
        col = lax.broadcasted_iota(_I32, (T, W), 1)
        need = float(TOPK) - count(lambda k: k > thr)

        def idx_body(it, v):
            cand = v | lax.shift_left(np.int32(1), np.int32(10) - it)
            below = count(lambda k: jnp.logical_and(k == thr, col < cand))
            return jnp.where(below < need, cand, v)
        last = lax.fori_loop(0, 11, idx_body, jnp.zeros((T, 1), _I32))
        k = keys_ref[:, 0:W]
        sel = jnp.logical_or(k > thr, jnp.logical_and(k == thr, col <= last))
        madd_ref[:, 0:W] = jnp.where(sel, 0.0, NEG)

    @pl.when(jnp.logical_not(any_tied))
    def _():
        madd_ref[:, 0:W] = jnp.where(keys_ref[:, 0:W] >= thr, 0.0, NEG)

    def head_body(hp, carry):
        hs = [hp * HEADS_PER_STEP + u for u in range(HEADS_PER_STEP)]
        qlats = [(_dot(q_ref[0, h], wuk_ref[h]) * (HEAD_DIM ** -0.5)).astype(_BF) for h in hs]
        latents = lambda lo, hi: ckv_ref[0, lo:hi, :]
        olats = _attend(qlats, latents, latents, W,
                        [tbl_ref[N_BUCKETS // 2 - 1, h] for h in hs],
                        [functools.partial(lambda h, rows, lo, hi: bn_ref[h, rows, lo:hi], h) for h in hs],
                        SA_ROWS, madd_ref)
        for h, olat in zip(hs, olats):
            oh_ref[h] = _dot(olat.astype(_BF), wuv_ref[h])
        return carry

    lax.fori_loop(0, H // HEADS_PER_STEP, head_body, 0)


def _sa_kernel(tbl_ref, q_ref, ckv_ref, iq_ref, ikk_ref, iw_ref, wuk_ref, wuv_ref, bn_ref, wo_ref, x_ref, g1_ref,
               o_ref, iqm_ref, keys_ref, madd_ref, oh_ref, oc_ref):
    a = pl.program_id(1)
    T = iq_ref.shape[1]
    S = ckv_ref.shape[1]
    H = q_ref.shape[1]
    lane = lax.broadcasted_iota(_I32, (T, 128), 1)
    for p in range(H // 2):
        iqp = iq_ref[0, :, p * 128:(p + 1) * 128]
        zero = jnp.zeros_like(iqp)
        iqm_ref[2 * p] = jnp.where(lane < IDX_DIM, iqp, zero)
        iqm_ref[2 * p + 1] = jnp.where(lane >= IDX_DIM, iqp, zero)

    for br in range(S // T):
        @pl.when(a == br)
        def _(br=br):
            _sa_branch((br + 1) * T, tbl_ref, q_ref, ckv_ref, iw_ref, ikk_ref, wuk_ref, wuv_ref, bn_ref,
                       iqm_ref, keys_ref, madd_ref, oh_ref)

    for h in range(H):
        oc_ref[:, h * HEAD_DIM:(h + 1) * HEAD_DIM] = oh_ref[h].astype(_BF)
    o_ref[0] = x_ref[0] + g1_ref[0] * _dot(oc_ref[...], wo_ref[...])


def _sparse_attn(q, ckv, iq, ikk, iw, wuk, wuv, rel_table, bn, w_out, x, g1):
    B, H, S, _ = q.shape
    T = ATT_TILE
    D = x.shape[2]
    return pl.pallas_call(
        _sa_kernel,
        grid=(B, S // T),
        in_specs=[pl.BlockSpec(memory_space=pltpu.SMEM),
                  pl.BlockSpec((1, H, T, HEAD_DIM), lambda b, i: (b, 0, i, 0)),
                  pl.BlockSpec((1, S, SA_LATENT), lambda b, i: (b, 0, 0)),
                  pl.BlockSpec((1, T, 512), lambda b, i: (b, i, 0)),
                  pl.BlockSpec((1, S, 128), lambda b, i: (b, 0, 0)),
                  pl.BlockSpec((1, T, 128), lambda b, i: (b, i, 0)),
                  _full((H, HEAD_DIM, SA_LATENT)),
                  _full((H, SA_LATENT, HEAD_DIM)),
                  _full((H, T, NEAR_W)),
                  _full((H * HEAD_DIM, D)),
                  pl.BlockSpec((1, T, D), lambda b, i: (b, i, 0)),
                  pl.BlockSpec((1, 1, D), lambda b, i: (b, 0, 0))],
        out_specs=pl.BlockSpec((1, T, D), lambda b, i: (b, i, 0)),
        out_shape=jax.ShapeDtypeStruct((B, S, D), _F32),
        scratch_shapes=[pltpu.VMEM((H, T, 128), _BF),
                        pltpu.VMEM((T, S), _I32),
                        pltpu.VMEM((T, S), _F32),
                        pltpu.VMEM((H, T, HEAD_DIM), _F32),
                        pltpu.VMEM((T, H * HEAD_DIM), _BF)],
        compiler_params=_params(("parallel", "arbitrary")),
        name="sparse_attn",
    )(rel_table, q, ckv, iq, ikk, iw, wuk, wuv, bn, w_out, x, g1)


def _cv_kernel(x_ref, g_ref, sc_ref, sh_ref, w1_ref, b1_ref, wdw_ref, bdw_ref, lg_ref, lb_ref, w2_ref, b2_ref,
               g1_ref, o_ref, ext_ref, y_ref):
    i = pl.program_id(1)
    tm = x_ref.shape[1]
    D = x_ref.shape[2]
    n_ext = CONV_HALO + tm

    @pl.when(i == 0)
    def _():
        ext_ref[0, 0:CONV_HALO] = jnp.zeros((CONV_HALO, D), _F32)

    @pl.when(i > 0)
    def _():
        ext_ref[0, 0:CONV_HALO] = ext_ref[0, tm:n_ext]

    hn = _rms_mod(x_ref[0], g_ref[...], sc_ref[0], sh_ref[0]).astype(_BF)
    tn = 512
    n = D // tn
    lo = lambda c: slice(c * tn, (c + 1) * tn)
    hi = lambda c: slice(D + c * tn, D + (c + 1) * tn)
    dots = lambda c: (_dot(hn, w1_ref[:, lo(c)]), _dot(hn, w1_ref[:, hi(c)]))
    nxt = dots(0)
    for c in range(n):
        cur = nxt
        if c + 1 < n:
            nxt = dots(c + 1)
        ext_ref[0, CONV_HALO:n_ext, lo(c)] = ((cur[0] + b1_ref[:, lo(c)])
                                              * jax.nn.sigmoid(cur[1] + b1_ref[:, hi(c)]))
    for r in range(1, SUBLANES):
        ext_ref[r, 0:n_ext - SUBLANES] = ext_ref[0, r:r + n_ext - SUBLANES]
    rb = CONV_ROWS
    base = CONV_HALO - (CONV_WIDTH - 1)

    for row0 in range(0, tm, rb):
        ns = rb // SUBLANES
        accs = [jnp.broadcast_to(bdw_ref[...], (SUBLANES, bdw_ref.shape[1]))] * ns
        for k in range(CONV_WIDTH):
            r = (base + k) % SUBLANES
            al = base + k - r
            w = wdw_ref[k]
            xk = ext_ref[r, row0 + al:row0 + al + rb, :]
            accs = [a + w * xk[s * SUBLANES:(s + 1) * SUBLANES] for s, a in enumerate(accs)]
        acc = jnp.concatenate(accs, axis=0)
        y_ref[row0:row0 + rb] = _silu(_layernorm(acc, lg_ref[...], lb_ref[...]))
    y = _dot(y_ref[...].astype(_BF), w2_ref[...]) + b2_ref[...]
    o_ref[0] = x_ref[0] + g1_ref[0] * y


def _cv_mixer(x, g, sc, sh, w1, b1, w_dw, b_dw, ln_g, ln_b, w2, b2, g1):
    B, S, D = x.shape
    tm = 512
    vec = lambda: _full((1, D))
    return pl.pallas_call(
        _cv_kernel,
        grid=(B, S // tm),
        in_specs=_in_specs_common(tm, D) + [_full((D, 2 * D)), _full((1, 2 * D)),
                                            _full((CONV_WIDTH, SUBLANES, D)), vec(), vec(), vec(),
                                            _full((D, D)), vec(),
                                            pl.BlockSpec((1, 1, D), lambda b, i: (b, 0, 0))],
        out_specs=pl.BlockSpec((1, tm, D), lambda b, i: (b, i, 0)),
        out_shape=jax.ShapeDtypeStruct((B, S, D), _F32),
        scratch_shapes=[pltpu.VMEM((SUBLANES, CONV_HALO + tm, D), _F32), pltpu.VMEM((tm, D), _F32)],
        compiler_params=_params(("parallel", "arbitrary")),
        name="cv_mixer",
    )(x, g, sc, sh, w1, b1, w_dw, b_dw, ln_g, ln_b, w2, b2, g1)


def _ffn_kernel(x_ref, xh_ref, g_ref, sc_ref, sh_ref, g2_ref, wup_ref, wdw_ref, bdw_ref, wdn_ref,
                fg_ref, o_ref, he_ref, act_ref, acc_ref, *, final):
    i = pl.program_id(1)
    tm = x_ref.shape[1]
    F = wdn_ref.shape[0]
    x = x_ref[0]
    he_ref[FFN_HALO:FFN_HALO + tm] = _rms_mod(x, g_ref[...], sc_ref[0], sh_ref[0]).astype(_BF)
    hh = _rms_mod(xh_ref[0], g_ref[...], sc_ref[0], sh_ref[0])
    he_ref[0:FFN_HALO] = jnp.where(i > 0, hh, jnp.zeros_like(hh)).astype(_BF)
    tf = FFN_CHUNK
    n_chunks = F // tf

    def up(c):
        he = he_ref[...]
        return (_dot(he, wup_ref[:, c * tf:(c + 1) * tf]), _dot(he, wup_ref[:, F + c * tf:F + (c + 1) * tf]))

    def conv(a, col0):
        cols = slice(col0, col0 + tf)
        w = wdw_ref[:, cols]
        return (a[FFN_HALO - 2:FFN_HALO - 2 + tm] * w[0:1]
                + a[FFN_HALO - 1:FFN_HALO - 1 + tm] * w[1:2]
                + a[FFN_HALO:FFN_HALO + tm] * w[2:3] + bdw_ref[:, cols])

    nxt = up(0)
    for c in range(n_chunks):
        cur = nxt
        if c + 1 < n_chunks:
            nxt = up(c + 1)
        act = _silu(conv(cur[0], c * tf)) * conv(cur[1], F + c * tf)
        act_ref[:, c * tf:(c + 1) * tf] = act.astype(_BF)
        if (c + 1) % FFN_DOWN_GROUP == 0 or c + 1 == n_chunks:
            lo = (c // FFN_DOWN_GROUP) * FFN_DOWN_GROUP * tf
            part = _dot(act_ref[:, lo:(c + 1) * tf], wdn_ref[lo:(c + 1) * tf, :])
            if lo == 0:
                acc_ref[...] = part
            else:
                acc_ref[...] += part
    xn = x + g2_ref[0] * acc_ref[...]
    if final:
        ms = jnp.mean(xn * xn, axis=-1, keepdims=True)
        xn = xn * lax.rsqrt(ms + EPS) * fg_ref[...]
    o_ref[0] = xn


def _ffn(x, g, sc, sh, g2, w_up, w_dw, b_dw, w_down, final_g, final, layer):
    B, S, D = x.shape
    F = w_down.shape[1]
    tm = FFN_ROWS
    hb = tm // FFN_HALO
    kern = functools.partial(_ffn_kernel, final=final)
    mod = lambda: pl.BlockSpec((1, 1, D), lambda b, i: (b, 0, 0))
    of_layer = lambda r, c: pl.BlockSpec((None, r, c), lambda b, i: (layer, 0, 0), pipeline_mode=pl.Buffered(1))
    return pl.pallas_call(
        kern,
        grid=(B, S // tm),
        in_specs=[pl.BlockSpec((1, tm, D), lambda b, i: (b, i, 0)),
                  pl.BlockSpec((1, FFN_HALO, D), lambda b, i: (b, jnp.maximum(i * hb - 1, 0), 0)),
                  _full((1, D)), mod(), mod(), mod(),
                  of_layer(D, 2 * F), _full((3, 2 * F)), _full((1, 2 * F)), of_layer(F, D),
                  _full((1, D))],
        out_specs=pl.BlockSpec((1, tm, D), lambda b, i: (b, i, 0)),
        out_shape=jax.ShapeDtypeStruct((B, S, D), _F32),
        scratch_shapes=[pltpu.VMEM((FFN_HALO + tm, D), _BF), pltpu.VMEM((tm, F), _BF),
                        pltpu.VMEM((tm, D), _F32)],
        compiler_params=_params(("parallel", "parallel")),
        name="ffn",
    )(x, x, g, sc, sh, g2, w_up, w_dw, b_dw, w_down, final_g)


def kernel(x, c, rel_table, ada_w, ada_b, norm_g, final_g, da_w_in, da_lam, da_subln_g, da_w_out, cv_w_pw1, cv_b_pw1, cv_w_dw, cv_b_dw, cv_ln_g, cv_ln_b, cv_w_pw2, cv_b_pw2, sa_w_in, sa_kv_g, sa_w_uk, sa_w_uv, sa_w_out, sg_w_in, sg_b_in, sg_ln_g, sg_ln_b, sg_w_s, sg_b_s, sg_w_out, sg_b_out, ff_w_up, ff_w_dw, ff_b_dw, ff_w_down):
    B, S, D = x.shape
    depth = ada_w.shape[0]
    n_mixers = 4
    H = N_HEADS
    mods = _ada(c, ada_w, ada_b)
    bn = _bias_tiles(rel_table)
    ff_up = ff_w_up.astype(_BF)
    ff_down = ff_w_down.astype(_BF)
    row = lambda v: v.reshape(1, -1)

    for layer in range(depth):
        kind = layer % n_mixers
        j = layer // n_mixers
        sh1, sc1, g1, sh2, sc2, g2 = [m.reshape(B, 1, D) for m in jnp.split(mods[layer], 6, axis=-1)]
        ng1 = row(norm_g[layer, 0])
        if kind == 0:
            lambda_init = 0.8 - 0.6 * math.exp(-0.3 * layer)
            qkv = _in_da(x, ng1, sc1, sh1, da_w_in[j].astype(_BF))
            x = _diff_attn(qkv, rel_table, bn, da_lam[j], row(da_subln_g[j]), lambda_init,
                           da_w_out[j].astype(_BF), x, g1)
        elif kind == 1:
            w_dw = jnp.broadcast_to(cv_w_dw[j][:, None, :], (CONV_WIDTH, SUBLANES, D))
            x = _cv_mixer(x, ng1, sc1, sh1, cv_w_pw1[j].astype(_BF), row(cv_b_pw1[j]), w_dw, row(cv_b_dw[j]),
                          row(cv_ln_g[j]), row(cv_ln_b[j]), cv_w_pw2[j].astype(_BF), row(cv_b_pw2[j]), g1)
        elif kind == 2:
            w = sa_w_in[j]
            o1 = H * HEAD_DIM
            o2 = o1 + SA_LATENT
            o3 = o2 + H * IDX_DIM
            o4 = o3 + IDX_DIM
            w_ik = w[:, o3:o4]
            w_iw = jnp.pad(w[:, o4:], ((0, 0), (0, 128 - (w.shape[1] - o4))))
            w_cat = jnp.concatenate([w[:, :o3], w_ik, w_ik, w_iw], axis=1).astype(_BF)
            q, ckv, iq, ikk, iw = _in_sa(x, ng1, sc1, sh1, w_cat, row(sa_kv_g[j]))
            wuk = jnp.transpose(sa_w_uk[j], (1, 2, 0)).astype(_BF)
            wuv = jnp.transpose(sa_w_uv[j], (1, 0, 2)).astype(_BF)
            x = _sparse_attn(q, ckv, iq, ikk, iw, wuk, wuv, rel_table, bn, sa_w_out[j].astype(_BF), x, g1)
        else:
            x = _sg_mixer(x, ng1, sc1, sh1, sg_w_in[j].astype(_BF), row(sg_b_in[j]), row(sg_ln_g[j]),
                          row(sg_ln_b[j]), sg_w_s[j], sg_b_s[j].T, sg_w_out[j].astype(_BF),
                          row(sg_b_out[j]), g1)
        x = _ffn(x, row(norm_g[layer, 1]), sc2, sh2, g2, ff_up, ff_w_dw[layer],
                 row(ff_b_dw[layer]), ff_down, row(final_g), layer == depth - 1, layer)
    return x
```

```python
import functools
import math

import jax
import jax.numpy as jnp
import numpy as np
from jax import lax
from jax.experimental import pallas as pl
from jax.experimental.pallas import tpu as pltpu

_BF = jnp.bfloat16
_F32 = jnp.float32
_I32 = jnp.int32

EPS = 1e-6
SUBLN_EPS = 1e-5
NEG = -1e30
CHUNK = 64
CHUNK_SHIFT = 6
N_HEADS = 8
HEAD_DIM = 128
DA_DIM = 64
SA_LATENT = 256
IDX_DIM = 64
TOPK = 256
N_BUCKETS = 32
MAX_DISTANCE = 128
SG_CHUNK = 128
SG_GROUPS = 8
CONV_WIDTH = 31
CONV_HALO = 32
CONV_ROWS = 16
SUBLANES = 8
FFN_HALO = 8
PROJ_ROWS = 1024
SG_ROWS = 1024
FFN_ROWS = 1024
FFN_CHUNK = 256
FFN_DOWN_GROUP = 11
ATT_TILE = 512
DA_HEADS_PER_STEP = 1
DA_ROWS = 128
SA_ROWS = 256
NEAR_W = 768
VMEM_LIMIT = 56 * 1024 * 1024


def _dot(a, b):
    return jnp.dot(a, b, preferred_element_type=_F32)


def _dot_nt(a, b):
    return lax.dot_general(a, b, (((1,), (1,)), ((), ())), preferred_element_type=_F32)


def _rms_mod(x, g, sc, sh):
    ms = jnp.mean(x * x, axis=-1, keepdims=True)
    return (x * lax.rsqrt(ms + EPS) * g) * (1.0 + sc) + sh


def _layernorm(x, g, b):
    mu = jnp.mean(x, axis=-1, keepdims=True)
    xc = x - mu
    var = jnp.mean(xc * xc, axis=-1, keepdims=True)
    return xc * lax.rsqrt(var + EPS) * g + b


def _silu(x):
    return x * jax.nn.sigmoid(x)


def _params(sem):
    return pltpu.CompilerParams(dimension_semantics=sem, vmem_limit_bytes=VMEM_LIMIT)


def _full(shape):
    n = len(shape)
    return pl.BlockSpec(shape, lambda *_: (0,) * n, pipeline_mode=pl.Buffered(1))


def _ada_kernel(c_ref, w_ref, b_ref, o_ref):
    ca = _silu(c_ref[...]).astype(_BF)
    o_ref[0] = _dot(ca, w_ref[0].astype(_BF)) + b_ref[0]


def _ada(c, ada_w, ada_b):
    L, D, N = ada_w.shape
    B = c.shape[0]
    tn = N // 4
    return pl.pallas_call(
        _ada_kernel,
        grid=(L, N // tn),
        in_specs=[pl.BlockSpec((B, D), lambda l, j: (0, 0)),
                  pl.BlockSpec((1, D, tn), lambda l, j: (l, 0, j)),
                  pl.BlockSpec((1, 1, tn), lambda l, j: (l, 0, j))],
        out_specs=pl.BlockSpec((1, B, tn), lambda l, j: (l, 0, j)),
        out_shape=jax.ShapeDtypeStruct((L, B, N), _F32),
        compiler_params=_params(("arbitrary", "arbitrary")),
        name="ada",
    )(c, ada_w, ada_b.reshape(L, 1, N))


def _bias_kernel(tbl_ref, bn_ref):
    h = pl.program_id(0)
    shape = bn_ref.shape[1:]
    i = lax.broadcasted_iota(_I32, shape, 0)
    j = lax.broadcasted_iota(_I32, shape, 1) - (NEAR_W - ATT_TILE)
    nb = N_BUCKETS // 2
    max_exact = nb // 2
    rel = j - i
    ret = jnp.where(rel > 0, nb, 0)
    n = jnp.abs(rel)
    nf = jnp.maximum(n, 1).astype(_F32)
    large = max_exact + (jnp.log(nf / max_exact) / math.log(MAX_DISTANCE / max_exact)
                         * (nb - max_exact)).astype(_I32)
    large = jnp.minimum(large, nb - 1)
    bucket = ret + jnp.where(n < max_exact, n, large)
    out = jnp.zeros(shape, _F32)
    for bk in range(N_BUCKETS):
        out = jnp.where(bucket == bk, tbl_ref[bk, h], out)
    visible = (j >> CHUNK_SHIFT) <= (i >> CHUNK_SHIFT)
    bn_ref[0] = jnp.where(visible, out, NEG)


def _bias_tiles(rel_table):
    H = rel_table.shape[1]
    return pl.pallas_call(
        _bias_kernel,
        grid=(H,),
        in_specs=[pl.BlockSpec(memory_space=pltpu.SMEM)],
        out_specs=pl.BlockSpec((1, ATT_TILE, NEAR_W), lambda h: (h, 0, 0)),
        out_shape=jax.ShapeDtypeStruct((H, ATT_TILE, NEAR_W), _F32),
        compiler_params=_params(("arbitrary",)),
        name="bias_tiles",
    )(rel_table)


def _in_specs_common(tm, D):
    return [pl.BlockSpec((1, tm, D), lambda b, i: (b, i, 0)),
            pl.BlockSpec((1, D), lambda b, i: (0, 0)),
            pl.BlockSpec((1, 1, D), lambda b, i: (b, 0, 0)),
            pl.BlockSpec((1, 1, D), lambda b, i: (b, 0, 0))]


def _in_da_kernel(x_ref, g_ref, sc_ref, sh_ref, w_ref, o_ref):
    hn = _rms_mod(x_ref[0], g_ref[...], sc_ref[0], sh_ref[0]).astype(_BF)
    N = w_ref.shape[1]
    H = o_ref.shape[2]
    tn = 512
    per = tn // HEAD_DIM
    for c in range(N // tn):
        r = _dot(hn, w_ref[:, c * tn:(c + 1) * tn])
        for k in range(per):
            idx = c * per + k
            o_ref[0, idx // H, idx % H] = r[:, k * HEAD_DIM:(k + 1) * HEAD_DIM].astype(_BF)


def _in_da(x, g, sc, sh, w):
    B, S, D = x.shape
    N = w.shape[1]
    H = N_HEADS
    tm = PROJ_ROWS
    return pl.pallas_call(
        _in_da_kernel,
        grid=(B, S // tm),
        in_specs=_in_specs_common(tm, D) + [_full((D, N))],
        out_specs=pl.BlockSpec((1, 3, H, tm, HEAD_DIM), lambda b, i: (b, 0, 0, i, 0)),
        out_shape=jax.ShapeDtypeStruct((B, 3, H, S, HEAD_DIM), _BF),
        compiler_params=_params(("parallel", "parallel")),
        name="in_da",
    )(x, g, sc, sh, w)


def _in_sa_kernel(x_ref, g_ref, sc_ref, sh_ref, w_ref, kvg_ref,
                  q_ref, ckv_ref, iq_ref, ikk_ref, iw_ref):
    hn = _rms_mod(x_ref[0], g_ref[...], sc_ref[0], sh_ref[0]).astype(_BF)
    H = q_ref.shape[1]
    for c in range(H // 2):
        r = _dot(hn, w_ref[:, c * 256:(c + 1) * 256])
        q_ref[0, 2 * c] = r[:, :HEAD_DIM].astype(_BF)
        q_ref[0, 2 * c + 1] = r[:, HEAD_DIM:].astype(_BF)
    o = H * HEAD_DIM
    ckv = _dot(hn, w_ref[:, o:o + SA_LATENT])
    ms = jnp.mean(ckv * ckv, axis=-1, keepdims=True)
    ckv_ref[0] = (ckv * lax.rsqrt(ms + EPS) * kvg_ref[...]).astype(_BF)
    o += SA_LATENT
    iq_ref[0] = (_dot(hn, w_ref[:, o:o + 512]) * (IDX_DIM ** -0.5)).astype(_BF)
    o += 512
    r = _dot(hn, w_ref[:, o:o + 256])
    ikk_ref[0] = r[:, :128].astype(_BF)
    iw_ref[0] = r[:, 128:] * (N_HEADS ** -0.5)


def _in_sa(x, g, sc, sh, w, kv_g):
    B, S, D = x.shape
    N = w.shape[1]
    H = N_HEADS
    tm = PROJ_ROWS
    row = lambda n: pl.BlockSpec((1, tm, n), lambda b, i: (b, i, 0))
    return pl.pallas_call(
        _in_sa_kernel,
        grid=(B, S // tm),
        in_specs=_in_specs_common(tm, D) + [_full((D, N)), _full((1, SA_LATENT))],
        out_specs=[pl.BlockSpec((1, H, tm, HEAD_DIM), lambda b, i: (b, 0, i, 0)),
                   row(SA_LATENT), row(512), row(128), row(128)],
        out_shape=[jax.ShapeDtypeStruct((B, H, S, HEAD_DIM), _BF),
                   jax.ShapeDtypeStruct((B, S, SA_LATENT), _BF),
                   jax.ShapeDtypeStruct((B, S, 512), _BF),
                   jax.ShapeDtypeStruct((B, S, 128), _BF),
                   jax.ShapeDtypeStruct((B, S, 128), _F32)],
        compiler_params=_params(("parallel", "parallel")),
        name="in_sa",
    )(x, g, sc, sh, w, kv_g)


def _sg_kernel(x_ref, g_ref, sc_ref, sh_ref, w_ref, b_ref, lg_ref, lb_ref, ws_ref, bs_ref, wo_ref, bo_ref,
               g1_ref, o_ref, a_scr, v_scr, z_scr):
    x = x_ref[0]
    hn = _rms_mod(x, g_ref[...], sc_ref[0], sh_ref[0]).astype(_BF)
    tm = x.shape[0]
    W = a_scr.shape[1]
    G = ws_ref.shape[0]
    C = SG_CHUNK
    gw = W // G
    tn = 512
    n = W // tn
    lo = lambda c: slice(c * tn, (c + 1) * tn)
    hi = lambda c: slice(W + c * tn, W + (c + 1) * tn)
    ti = lax.broadcasted_iota(_I32, (C, C), 0)
    si = lax.broadcasted_iota(_I32, (C, C), 1)
    cols = [hi(c) for c in range(n)] + [lo(c) for c in range(n)]
    nxt = _dot(hn, w_ref[:, cols[0]])
    for t in range(2 * n):
        cur = nxt
        if t + 1 < 2 * n:
            nxt = _dot(hn, w_ref[:, cols[t + 1]])
        act = jax.nn.gelu(cur + b_ref[:, cols[t]], approximate=True)
        if t < n:
            a_scr[:, lo(t)] = act
            if t == n - 1:
                v_scr[...] = _layernorm(a_scr[...], lg_ref[...], lb_ref[...]).astype(_BF)
            continue
        for gl in range(tn // gw):
            g = (t - n) * (tn // gw) + gl
            wg = jnp.where(ti >= si, ws_ref[g], 0.0).astype(_BF)
            bcol = bs_ref[:, g:g + 1]
            gcols = slice(g * gw, (g + 1) * gw)
            for r in range(tm // C):
                rows = slice(r * C, (r + 1) * C)
                sv = _dot(wg, v_scr[rows, gcols]) + bcol
                z_scr[rows, gcols] = (act[rows, gl * gw:(gl + 1) * gw] * sv).astype(_BF)
    y = _dot(z_scr[...], wo_ref[...]) + bo_ref[...]
    o_ref[0] = x + g1_ref[0] * y


def _sg_mixer(x, g, sc, sh, w, b, ln_g, ln_b, w_s, b_s_t, w_out, b_out, g1):
    B, S, D = x.shape
    N = w.shape[1]
    W = N // 2
    G = w_s.shape[0]
    tm = SG_ROWS
    return pl.pallas_call(
        _sg_kernel,
        grid=(B, S // tm),
        in_specs=_in_specs_common(tm, D) + [_full((D, N)), _full((1, N)), _full((1, W)), _full((1, W)),
                                            _full((G, SG_CHUNK, SG_CHUNK)), _full((SG_CHUNK, G)),
                                            _full((W, D)), _full((1, D)),
                                            pl.BlockSpec((1, 1, D), lambda b, i: (b, 0, 0))],
        out_specs=pl.BlockSpec((1, tm, D), lambda b, i: (b, i, 0)),
        out_shape=jax.ShapeDtypeStruct((B, S, D), _F32),
        scratch_shapes=[pltpu.VMEM((tm, W), _F32), pltpu.VMEM((tm, W), _BF), pltpu.VMEM((tm, W), _BF)],
        compiler_params=_params(("parallel", "parallel")),
        name="sg_mixer",
    )(x, g, sc, sh, w, b, ln_g, ln_b, w_s, b_s_t, w_out, b_out, g1)


def _attend(qs, keys, values, W, fars, near_biases, block_rows, madd_ref=None):
    nw = min(W, NEAR_W)
    fw = W - nw
    lo = NEAR_W - nw
    n_rows = qs[0].shape[0]
    blocks = [slice(r, r + block_rows) for r in range(0, n_rows, block_rows)]
    per_q = lambda f: f if isinstance(f, (list, tuple)) else [f] * len(qs)
    logits = []
    for q, far, near_bias, keys, values in zip(qs, fars, near_biases, per_q(keys), per_q(values)):
        for rows in blocks:
            we = W - (n_rows - rows.stop)
            s_n = _dot_nt(q[rows], keys(fw, we)) + near_bias(rows, lo, lo + we - fw)
            s_f = None
            if fw:
                s_f = _dot_nt(q[rows], keys(0, fw)) + far
            if madd_ref is not None:
                s_n = s_n + madd_ref[rows, fw:we]
                if fw:
                    s_f = s_f + madd_ref[rows, 0:fw]
            logits.append((s_n, s_f, we, values))
    outs = []
    for s_n, s_f, we, values in logits:
        m = jnp.max(s_n, axis=1, keepdims=True)
        if fw:
            m = jnp.maximum(m, jnp.max(s_f, axis=1, keepdims=True))
            p_f = jnp.exp(s_f - m)
        p_n = jnp.exp(s_n - m)
        l = jnp.sum(p_n, axis=1, keepdims=True)
        o = _dot(p_n.astype(_BF), values(fw, we))
        if fw:
            l = l + jnp.sum(p_f, axis=1, keepdims=True)
            o = o + _dot(p_f.astype(_BF), values(0, fw))
        outs.append(o * (1.0 / l))
    nb = len(blocks)
    return [jnp.concatenate(outs[i * nb:(i + 1) * nb], axis=0) for i in range(len(qs))]


def _da_kernel(tbl_ref, q_ref, k_ref, v_ref, bn_ref, lam_ref, g_ref, wo_ref, x_ref, g1_ref, o_ref,
               oh_ref, oc_ref, *, lambda_init):
    a = pl.program_id(1)
    H = q_ref.shape[2]
    T = q_ref.shape[3]
    S = k_ref.shape[3]
    lane = lax.broadcasted_iota(_I32, (T, HEAD_DIM), 1)
    lam = lam_ref[...]
    lam_full = (jnp.exp(jnp.sum(lam[0:1] * lam[1:2], axis=1, keepdims=True))
                - jnp.exp(jnp.sum(lam[2:3] * lam[3:4], axis=1, keepdims=True)) + lambda_init)

    def heads(hp, W):
        hs = [hp * DA_HEADS_PER_STEP + u for u in range(DA_HEADS_PER_STEP)]
        qs, fars, biases, keys, values = [], [], [], [], []
        for h in hs:
            q = q_ref[0, 0, h] * (DA_DIM ** -0.5)
            zero = jnp.zeros_like(q)
            qs += [jnp.where(lane < DA_DIM, q, zero), jnp.where(lane >= DA_DIM, q, zero)]
            fars += [tbl_ref[N_BUCKETS // 2 - 1, h]] * 2
            biases += [functools.partial(lambda h, rows, lo, hi: bn_ref[h, rows, lo:hi], h)] * 2
            keys += [functools.partial(lambda h, lo, hi: k_ref[0, 0, h, lo:hi, :], h)] * 2
            values += [functools.partial(lambda h, lo, hi: v_ref[0, 0, h, lo:hi, :], h)] * 2
        outs = _attend(qs, keys, values, W, fars, biases, DA_ROWS)
        for u, h in enumerate(hs):
            o = outs[2 * u] - lam_full * outs[2 * u + 1]
            ms = jnp.mean(o * o, axis=-1, keepdims=True)
            oh_ref[h] = (o * lax.rsqrt(ms + SUBLN_EPS) * g_ref[...]) * (1.0 - lambda_init)

    for br in range(S // T):
        @pl.when(a == br)
        def _(br=br):
            def body(hp, carry):
                heads(hp, (br + 1) * T)
                return carry
            lax.fori_loop(0, H // DA_HEADS_PER_STEP, body, 0)

    for h in range(H):
        oc_ref[:, h * HEAD_DIM:(h + 1) * HEAD_DIM] = oh_ref[h].astype(_BF)
    o_ref[0] = x_ref[0] + g1_ref[0] * _dot(oc_ref[...], wo_ref[...])


def _diff_attn(qkv, rel_table, bn, lam, subln_g, lambda_init, w_out, x, g1):
    B, _, H, S, _ = qkv.shape
    T = ATT_TILE
    D = x.shape[2]
    kern = functools.partial(_da_kernel, lambda_init=lambda_init)
    part = lambda j, rows, idx: pl.BlockSpec((1, 1, H, rows, HEAD_DIM), lambda b, i: (b, j, 0, idx(i), 0))
    return pl.pallas_call(
        kern,
        grid=(B, S // T),
        in_specs=[pl.BlockSpec(memory_space=pltpu.SMEM),
                  part(0, T, lambda i: i), part(1, S, lambda i: 0), part(2, S, lambda i: 0),
                  _full((H, T, NEAR_W)),
                  _full((4, DA_DIM)), _full((1, HEAD_DIM)),
                  _full((H * HEAD_DIM, D)),
                  pl.BlockSpec((1, T, D), lambda b, i: (b, i, 0)),
                  pl.BlockSpec((1, 1, D), lambda b, i: (b, 0, 0))],
        out_specs=pl.BlockSpec((1, T, D), lambda b, i: (b, i, 0)),
        out_shape=jax.ShapeDtypeStruct((B, S, D), _F32),
        scratch_shapes=[pltpu.VMEM((H, T, HEAD_DIM), _F32), pltpu.VMEM((T, H * HEAD_DIM), _BF)],
        compiler_params=_params(("parallel", "arbitrary")),
        name="diff_attn",
    )(rel_table, qkv, qkv, qkv, bn, lam, subln_g, w_out, x, g1)


_INT_MIN = np.int32(-2 ** 31)
_KEY_NEG_INF = np.int32(np.array(-np.inf, np.float32).view(np.int32) ^ np.int32(0x7FFFFFFF))
SEARCH_GROUPS = 4
HEADS_PER_STEP = 2


def _sa_branch(W, tbl_ref, q_ref, ckv_ref, iw_ref, ikk_ref, wuk_ref, wuv_ref, bn_ref,
               iqm_ref, keys_ref, madd_ref, oh_ref, last_ref):
    T = iw_ref.shape[1]
    H = q_ref.shape[1]
    iw = iw_ref[0]

    for cb in range(W // T):
        cols = slice(cb * T, (cb + 1) * T)
        ikt = ikk_ref[0, cols, :]
        score = jnp.zeros((T, T), _F32)
        for hh in range(H):
            score = score + jnp.maximum(_dot_nt(iqm_ref[hh], ikt), 0.0) * iw[:, hh:hh + 1]
        score = jnp.where(score == 0.0, 0.0, score)
        if cb == W // T - 1:
            row = lax.broadcasted_iota(_I32, (T, T), 0)
            col = lax.broadcasted_iota(_I32, (T, T), 1)
            score = jnp.where((col >> CHUNK_SHIFT) <= (row >> CHUNK_SHIFT), score, -jnp.inf)
        bits = lax.bitcast_convert_type(score, _I32)
        keys_ref[:, cols] = bits ^ ((bits >> 31) & np.int32(0x7FFFFFFF))

    def count(pred, rows=slice(None)):
        return jnp.sum(jnp.where(pred(keys_ref[rows, 0:W]), 1.0, 0.0), axis=1, keepdims=True)

    R = T // SEARCH_GROUPS

    def bit_body(it, thrs):
        bit = lax.shift_left(np.int32(1), np.int32(31) - it)
        out = []
        for g in range(SEARCH_GROUPS):
            cand_u = thrs[g] | bit
            cand = cand_u ^ _INT_MIN
            cnt = count(lambda k: k >= cand, slice(g * R, (g + 1) * R))
            out.append(jnp.where(cnt >= float(TOPK), cand_u, thrs[g]))
        return tuple(out)

    thrs = lax.fori_loop(0, 32, bit_body, tuple(jnp.zeros((R, 1), _I32) for _ in range(SEARCH_GROUPS)))
    thr = jnp.concatenate(thrs, axis=0) ^ _INT_MIN

    cnt_gt = count(lambda k: k > thr)
    cnt_ge = count(lambda k: k >= thr)
    need = float(TOPK) - cnt_gt
    last_ref[...] = jnp.full((T, 1), 2 ** 30, _I32)
    tied = jnp.logical_and(cnt_ge - cnt_gt > need, thr > _KEY_NEG_INF)

    @pl.when(jnp.max(jnp.where(tied, 1.0, 0.0)) > 0.0)
    def _():
        col = lax.broadcasted_iota(_I32, (T, W), 1)

        def idx_body(it, v):
            cand = v | lax.shift_left(np.int32(1), np.int32(10) - it)
            below = count(lambda k: jnp.logical_and(k == thr, col < cand))
            return jnp.where(below < need, cand, v)
        last_ref[...] = lax.fori_loop(0, 11, idx_body, jnp.zeros((T, 1), _I32))

    last = last_ref[...]
    k = keys_ref[:, 0:W]
    col = lax.broadcasted_iota(_I32, (T, W), 1)
    sel = jnp.logical_or(k > thr, jnp.logical_and(k == thr, col <= last))
    madd_ref[:, 0:W] = jnp.where(sel, 0.0, NEG)

    def head_body(hp, carry):
        hs = [hp * HEADS_PER_STEP + u for u in range(HEADS_PER_STEP)]
        qlats = [(_dot(q_ref[0, h], wuk_ref[h]) * (HEAD_DIM ** -0.5)).astype(_BF) for h in hs]
        latents = lambda lo, hi: ckv_ref[0, lo:hi, :]
        olats = _attend(qlats, latents, latents, W,
                        [tbl_ref[N_BUCKETS // 2 - 1, h] for h in hs],
                        [functools.partial(lambda h, rows, lo, hi: bn_ref[h, rows, lo:hi], h) for h in hs],
                        SA_ROWS, madd_ref)
        for h, olat in zip(hs, olats):
            oh_ref[h] = _dot(olat.astype(_BF), wuv_ref[h])
        return carry

    lax.fori_loop(0, H // HEADS_PER_STEP, head_body, 0)


def _sa_kernel(tbl_ref, q_ref, ckv_ref, iq_ref, ikk_ref, iw_ref, wuk_ref, wuv_ref, bn_ref, wo_ref, x_ref, g1_ref,
               o_ref, iqm_ref, keys_ref, madd_ref, oh_ref, last_ref, oc_ref):
    a = pl.program_id(1)
    T = iq_ref.shape[1]
    S = ckv_ref.shape[1]
    H = q_ref.shape[1]
    lane = lax.broadcasted_iota(_I32, (T, 128), 1)
    for p in range(H // 2):
        iqp = iq_ref[0, :, p * 128:(p + 1) * 128]
        zero = jnp.zeros_like(iqp)
        iqm_ref[2 * p] = jnp.where(lane < IDX_DIM, iqp, zero)
        iqm_ref[2 * p + 1] = jnp.where(lane >= IDX_DIM, iqp, zero)

    for br in range(S // T):
        @pl.when(a == br)
        def _(br=br):
            _sa_branch((br + 1) * T, tbl_ref, q_ref, ckv_ref, iw_ref, ikk_ref, wuk_ref, wuv_ref, bn_ref,
                       iqm_ref, keys_ref, madd_ref, oh_ref, last_ref)

    for h in range(H):
        oc_ref[:, h * HEAD_DIM:(h + 1) * HEAD_DIM] = oh_ref[h].astype(_BF)
    o_ref[0] = x_ref[0] + g1_ref[0] * _dot(oc_ref[...], wo_ref[...])


def _sparse_attn(q, ckv, iq, ikk, iw, wuk, wuv, rel_table, bn, w_out, x, g1):
    B, H, S, _ = q.shape
    T = ATT_TILE
    D = x.shape[2]
    return pl.pallas_call(
        _sa_kernel,
        grid=(B, S // T),
        in_specs=[pl.BlockSpec(memory_space=pltpu.SMEM),
                  pl.BlockSpec((1, H, T, HEAD_DIM), lambda b, i: (b, 0, i, 0)),
                  pl.BlockSpec((1, S, SA_LATENT), lambda b, i: (b, 0, 0)),
                  pl.BlockSpec((1, T, 512), lambda b, i: (b, i, 0)),
                  pl.BlockSpec((1, S, 128), lambda b, i: (b, 0, 0)),
                  pl.BlockSpec((1, T, 128), lambda b, i: (b, i, 0)),
                  _full((H, HEAD_DIM, SA_LATENT)),
                  _full((H, SA_LATENT, HEAD_DIM)),
                  _full((H, T, NEAR_W)),
                  _full((H * HEAD_DIM, D)),
                  pl.BlockSpec((1, T, D), lambda b, i: (b, i, 0)),
                  pl.BlockSpec((1, 1, D), lambda b, i: (b, 0, 0))],
        out_specs=pl.BlockSpec((1, T, D), lambda b, i: (b, i, 0)),
        out_shape=jax.ShapeDtypeStruct((B, S, D), _F32),
        scratch_shapes=[pltpu.VMEM((H, T, 128), _BF),
                        pltpu.VMEM((T, S), _I32),
                        pltpu.VMEM((T, S), _F32),
                        pltpu.VMEM((H, T, HEAD_DIM), _F32),
                        pltpu.VMEM((T, 1), _I32),
                        pltpu.VMEM((T, H * HEAD_DIM), _BF)],
        compiler_params=_params(("parallel", "arbitrary")),
        name="sparse_attn",
    )(rel_table, q, ckv, iq, ikk, iw, wuk, wuv, bn, w_out, x, g1)


def _cv_kernel(x_ref, g_ref, sc_ref, sh_ref, w1_ref, b1_ref, wdw_ref, bdw_ref, lg_ref, lb_ref, w2_ref, b2_ref,
               g1_ref, o_ref, ext_ref, y_ref):
    i = pl.program_id(1)
    tm = x_ref.shape[1]
    D = x_ref.shape[2]
    n_ext = CONV_HALO + tm

    @pl.when(i == 0)
    def _():
        ext_ref[0, 0:CONV_HALO] = jnp.zeros((CONV_HALO, D), _F32)

    @pl.when(i > 0)
    def _():
        ext_ref[0, 0:CONV_HALO] = ext_ref[0, tm:n_ext]

    hn = _rms_mod(x_ref[0], g_ref[...], sc_ref[0], sh_ref[0]).astype(_BF)
    tn = 512
    n = D // tn
    lo = lambda c: slice(c * tn, (c + 1) * tn)
    hi = lambda c: slice(D + c * tn, D + (c + 1) * tn)
    dots = lambda c: (_dot(hn, w1_ref[:, lo(c)]), _dot(hn, w1_ref[:, hi(c)]))
    nxt = dots(0)
    for c in range(n):
        cur = nxt
        if c + 1 < n:
            nxt = dots(c + 1)
        ext_ref[0, CONV_HALO:n_ext, lo(c)] = ((cur[0] + b1_ref[:, lo(c)])
                                              * jax.nn.sigmoid(cur[1] + b1_ref[:, hi(c)]))
    for r in range(1, SUBLANES):
        ext_ref[r, 0:n_ext - SUBLANES] = ext_ref[0, r:r + n_ext - SUBLANES]
    rb = CONV_ROWS
    base = CONV_HALO - (CONV_WIDTH - 1)

    for row0 in range(0, tm, rb):
        ns = rb // SUBLANES
        accs = [jnp.broadcast_to(bdw_ref[...], (SUBLANES, bdw_ref.shape[1]))] * ns
        for k in range(CONV_WIDTH):
            r = (base + k) % SUBLANES
            al = base + k - r
            w = wdw_ref[k]
            xk = ext_ref[r, row0 + al:row0 + al + rb, :]
            accs = [a + w * xk[s * SUBLANES:(s + 1) * SUBLANES] for s, a in enumerate(accs)]
        acc = jnp.concatenate(accs, axis=0)
        y_ref[row0:row0 + rb] = _silu(_layernorm(acc, lg_ref[...], lb_ref[...]))
    y = _dot(y_ref[...].astype(_BF), w2_ref[...]) + b2_ref[...]
    o_ref[0] = x_ref[0] + g1_ref[0] * y


def _cv_mixer(x, g, sc, sh, w1, b1, w_dw, b_dw, ln_g, ln_b, w2, b2, g1):
    B, S, D = x.shape
    tm = 512
    vec = lambda: _full((1, D))
    return pl.pallas_call(
        _cv_kernel,
        grid=(B, S // tm),
        in_specs=_in_specs_common(tm, D) + [_full((D, 2 * D)), _full((1, 2 * D)),
                                            _full((CONV_WIDTH, SUBLANES, D)), vec(), vec(), vec(),
                                            _full((D, D)), vec(),
                                            pl.BlockSpec((1, 1, D), lambda b, i: (b, 0, 0))],
        out_specs=pl.BlockSpec((1, tm, D), lambda b, i: (b, i, 0)),
        out_shape=jax.ShapeDtypeStruct((B, S, D), _F32),
        scratch_shapes=[pltpu.VMEM((SUBLANES, CONV_HALO + tm, D), _F32), pltpu.VMEM((tm, D), _F32)],
        compiler_params=_params(("parallel", "arbitrary")),
        name="cv_mixer",
    )(x, g, sc, sh, w1, b1, w_dw, b_dw, ln_g, ln_b, w2, b2, g1)


def _ffn_kernel(x_ref, xh_ref, g_ref, sc_ref, sh_ref, g2_ref, wup_ref, wdw_ref, bdw_ref, wdn_ref,
                fg_ref, o_ref, he_ref, act_ref, acc_ref, *, final):
    i = pl.program_id(1)
    tm = x_ref.shape[1]
    F = wdn_ref.shape[0]
    x = x_ref[0]
    he_ref[FFN_HALO:FFN_HALO + tm] = _rms_mod(x, g_ref[...], sc_ref[0], sh_ref[0]).astype(_BF)
    hh = _rms_mod(xh_ref[0], g_ref[...], sc_ref[0], sh_ref[0])
    he_ref[0:FFN_HALO] = jnp.where(i > 0, hh, jnp.zeros_like(hh)).astype(_BF)
    tf = FFN_CHUNK
    n_chunks = F // tf

    def up(c):
        he = he_ref[...]
        return (_dot(he, wup_ref[:, c * tf:(c + 1) * tf]), _dot(he, wup_ref[:, F + c * tf:F + (c + 1) * tf]))

    def conv(a, col0):
        cols = slice(col0, col0 + tf)
        w = wdw_ref[:, cols]
        return (a[FFN_HALO - 2:FFN_HALO - 2 + tm] * w[0:1]
                + a[FFN_HALO - 1:FFN_HALO - 1 + tm] * w[1:2]
                + a[FFN_HALO:FFN_HALO + tm] * w[2:3] + bdw_ref[:, cols])

    nxt = up(0)
    for c in range(n_chunks):
        cur = nxt
        if c + 1 < n_chunks:
            nxt = up(c + 1)
        act = _silu(conv(cur[0], c * tf)) * conv(cur[1], F + c * tf)
        act_ref[:, c * tf:(c + 1) * tf] = act.astype(_BF)
        if (c + 1) % FFN_DOWN_GROUP == 0 or c + 1 == n_chunks:
            lo = (c // FFN_DOWN_GROUP) * FFN_DOWN_GROUP * tf
            part = _dot(act_ref[:, lo:(c + 1) * tf], wdn_ref[lo:(c + 1) * tf, :])
            if lo == 0:
                acc_ref[...] = part
            else:
                acc_ref[...] += part
    xn = x + g2_ref[0] * acc_ref[...]
    if final:
        ms = jnp.mean(xn * xn, axis=-1, keepdims=True)
        xn = xn * lax.rsqrt(ms + EPS) * fg_ref[...]
    o_ref[0] = xn


def _ffn(x, g, sc, sh, g2, w_up, w_dw, b_dw, w_down, final_g, final, layer):
    B, S, D = x.shape
    F = w_down.shape[1]
    tm = FFN_ROWS
    hb = tm // FFN_HALO
    kern = functools.partial(_ffn_kernel, final=final)
    mod = lambda: pl.BlockSpec((1, 1, D), lambda b, i: (b, 0, 0))
    of_layer = lambda r, c: pl.BlockSpec((None, r, c), lambda b, i: (layer, 0, 0), pipeline_mode=pl.Buffered(1))
    return pl.pallas_call(
        kern,
        grid=(B, S // tm),
        in_specs=[pl.BlockSpec((1, tm, D), lambda b, i: (b, i, 0)),
                  pl.BlockSpec((1, FFN_HALO, D), lambda b, i: (b, jnp.maximum(i * hb - 1, 0), 0)),
                  _full((1, D)), mod(), mod(), mod(),
                  of_layer(D, 2 * F), _full((3, 2 * F)), _full((1, 2 * F)), of_layer(F, D),
                  _full((1, D))],
        out_specs=pl.BlockSpec((1, tm, D), lambda b, i: (b, i, 0)),
        out_shape=jax.ShapeDtypeStruct((B, S, D), _F32),
        scratch_shapes=[pltpu.VMEM((FFN_HALO + tm, D), _BF), pltpu.VMEM((tm, F), _BF),
                        pltpu.VMEM((tm, D), _F32)],
        compiler_params=_params(("parallel", "parallel")),
        name="ffn",
    )(x, x, g, sc, sh, g2, w_up, w_dw, b_dw, w_down, final_g)


def kernel(x, c, rel_table, ada_w, ada_b, norm_g, final_g, da_w_in, da_lam, da_subln_g, da_w_out, cv_w_pw1, cv_b_pw1, cv_w_dw, cv_b_dw, cv_ln_g, cv_ln_b, cv_w_pw2, cv_b_pw2, sa_w_in, sa_kv_g, sa_w_uk, sa_w_uv, sa_w_out, sg_w_in, sg_b_in, sg_ln_g, sg_ln_b, sg_w_s, sg_b_s, sg_w_out, sg_b_out, ff_w_up, ff_w_dw, ff_b_dw, ff_w_down):
    B, S, D = x.shape
    depth = ada_w.shape[0]
    n_mixers = 4
    H = N_HEADS
    mods = _ada(c, ada_w, ada_b)
    bn = _bias_tiles(rel_table)
    ff_up = ff_w_up.astype(_BF)
    ff_down = ff_w_down.astype(_BF)
    row = lambda v: v.reshape(1, -1)

    for layer in range(depth):
        kind = layer % n_mixers
        j = layer // n_mixers
        sh1, sc1, g1, sh2, sc2, g2 = [m.reshape(B, 1, D) for m in jnp.split(mods[layer], 6, axis=-1)]
        ng1 = row(norm_g[layer, 0])
        if kind == 0:
            lambda_init = 0.8 - 0.6 * math.exp(-0.3 * layer)
            qkv = _in_da(x, ng1, sc1, sh1, da_w_in[j].astype(_BF))
            x = _diff_attn(qkv, rel_table, bn, da_lam[j], row(da_subln_g[j]), lambda_init,
                           da_w_out[j].astype(_BF), x, g1)
        elif kind == 1:
            w_dw = jnp.broadcast_to(cv_w_dw[j][:, None, :], (CONV_WIDTH, SUBLANES, D))
            x = _cv_mixer(x, ng1, sc1, sh1, cv_w_pw1[j].astype(_BF), row(cv_b_pw1[j]), w_dw, row(cv_b_dw[j]),
                          row(cv_ln_g[j]), row(cv_ln_b[j]), cv_w_pw2[j].astype(_BF), row(cv_b_pw2[j]), g1)
        elif kind == 2:
            w = sa_w_in[j]
            o1 = H * HEAD_DIM
            o2 = o1 + SA_LATENT
            o3 = o2 + H * IDX_DIM
            o4 = o3 + IDX_DIM
            w_ik = w[:, o3:o4]
            w_iw = jnp.pad(w[:, o4:], ((0, 0), (0, 128 - (w.shape[1] - o4))))
            w_cat = jnp.concatenate([w[:, :o3], w_ik, w_ik, w_iw], axis=1).astype(_BF)
            q, ckv, iq, ikk, iw = _in_sa(x, ng1, sc1, sh1, w_cat, row(sa_kv_g[j]))
            wuk = jnp.transpose(sa_w_uk[j], (1, 2, 0)).astype(_BF)
            wuv = jnp.transpose(sa_w_uv[j], (1, 0, 2)).astype(_BF)
            x = _sparse_attn(q, ckv, iq, ikk, iw, wuk, wuv, rel_table, bn, sa_w_out[j].astype(_BF), x, g1)
        else:
            x = _sg_mixer(x, ng1, sc1, sh1, sg_w_in[j].astype(_BF), row(sg_b_in[j]), row(sg_ln_g[j]),
                          row(sg_ln_b[j]), sg_w_s[j], sg_b_s[j].T, sg_w_out[j].astype(_BF),
                          row(sg_b_out[j]), g1)
        x = _ffn(x, row(norm_g[layer, 1]), sc2, sh2, g2, ff_up, ff_w_dw[layer],
                 row(ff_b_dw[layer]), ff_down, row(final_g), layer == depth - 1, layer)
    return x
```
